```python
import jax, jax.numpy as jnp
from jax import lax
import numpy as np

D_MODEL = 1024
BATCH = 4
SEQ = 8192
DEPTH = 4

MLA_HEADS = 4
MLA_Q_LORA = 384
MLA_KV_LORA = 256
MLA_NOPE_DIM = 128
MLA_ROPE_DIM = 64
MLA_V_DIM = 128
MLA_WIDTH = MLA_HEADS * MLA_V_DIM
ROPE_THETA = 10000.0
Q_BLOCK = 128
MLSTM_HEADS = 4
MLSTM_HEAD_DIM = 64
MLSTM_WIDTH = MLSTM_HEADS * MLSTM_HEAD_DIM
MLSTM_CONV = 5
MLSTM_CHUNK = 64
FNET_GROUPS = 4
FNET_GROUP_DIM = 64
FNET_WIDTH = FNET_GROUPS * FNET_GROUP_DIM
MIX_WIDTH = MLA_WIDTH + MLSTM_WIDTH + FNET_WIDTH
IN_SPLITS = (MLA_Q_LORA, MLA_KV_LORA, MLA_ROPE_DIM, MLSTM_WIDTH, MLSTM_WIDTH, MLSTM_WIDTH, 4 * MLSTM_HEADS, FNET_WIDTH)
D_IN = MLA_Q_LORA + MLA_KV_LORA + MLA_ROPE_DIM + 3 * MLSTM_WIDTH + 4 * MLSTM_HEADS + FNET_WIDTH
D_FF = 2816
PLE_DIM = 256
EPS = 1e-6

kernel_name = 'hybrid_mla_mlstm_fnet_macaron_encoder'


def rmsnorm(x, g):
    xf = x.astype(jnp.float32)
    y = xf * lax.rsqrt(jnp.mean(xf * xf, axis=-1, keepdims=True) + EPS)
    return (y * g.astype(jnp.float32)).astype(x.dtype)


def swiglu(x, w_gate, w_up, w_down):
    return (jax.nn.silu(x @ w_gate) * (x @ w_up)) @ w_down


def split_cols(z, sizes):
    offs = [int(o) for o in np.cumsum(sizes)[:-1]]
    return jnp.split(z, offs, axis=-1)


def rope_tables(positions):
    inv = 1.0 / (ROPE_THETA ** (jnp.arange(0, MLA_ROPE_DIM, 2, dtype=jnp.float32) / MLA_ROPE_DIM))
    ang = positions.astype(jnp.float32)[..., None] * inv
    return jnp.cos(ang), jnp.sin(ang)


def apply_rope(x, cos, sin):
    xf = x.astype(jnp.float32)
    x1, x2 = jnp.split(xf, 2, axis=-1)
    out = jnp.concatenate([x1 * cos - x2 * sin, x2 * cos + x1 * sin], axis=-1)
    return out.astype(x.dtype)


def mla_mixer(c_q, c_kv, k_rope, cos, sin, q_norm, w_uq, kv_norm, w_ukv):
    B, S, _ = c_q.shape
    q = (rmsnorm(c_q, q_norm) @ w_uq).reshape(B, S, MLA_HEADS, MLA_NOPE_DIM + MLA_ROPE_DIM)
    q_nope, q_rope = q[..., :MLA_NOPE_DIM], q[..., MLA_NOPE_DIM:]
    q_rope = apply_rope(q_rope, cos[:, :, None, :], sin[:, :, None, :])
    kv = (rmsnorm(c_kv, kv_norm) @ w_ukv).reshape(B, S, MLA_HEADS, MLA_NOPE_DIM + MLA_V_DIM)
    k_nope, v = kv[..., :MLA_NOPE_DIM], kv[..., MLA_NOPE_DIM:]
    k_rope = apply_rope(k_rope, cos, sin)
    scale = (MLA_NOPE_DIM + MLA_ROPE_DIM) ** -0.5
    nb = S // Q_BLOCK

    def attend(args):
        qn, qr = args
        s = jnp.einsum('bqhd,bkhd->bhqk', qn, k_nope) + jnp.einsum('bqhr,bkr->bhqk', qr, k_rope)
        pr = jax.nn.softmax(s.astype(jnp.float32) * scale, axis=-1).astype(v.dtype)
        return jnp.einsum('bhqk,bkhd->bqhd', pr, v)

    qn_b = jnp.moveaxis(q_nope.reshape(B, nb, Q_BLOCK, MLA_HEADS, MLA_NOPE_DIM), 1, 0)
    qr_b = jnp.moveaxis(q_rope.reshape(B, nb, Q_BLOCK, MLA_HEADS, MLA_ROPE_DIM), 1, 0)
    o = lax.map(attend, (qn_b, qr_b))
    return jnp.moveaxis(o, 0, 1).reshape(B, S, MLA_WIDTH)


def mlstm_chunkwise(q, k, v, i_pre, f_pre):
    B, H, S, d = q.shape
    L = MLSTM_CHUNK
    nc = S // L
    q = q * d ** -0.5
    lf = jax.nn.log_sigmoid(f_pre)
    causal = jnp.tril(jnp.ones((L, L), dtype=bool))

    def chunks(t):
        return jnp.moveaxis(t.reshape((B, H, nc, L) + t.shape[3:]), 2, 0)

    def step(carry, xs):
        C, n, m = carry
        qc, kc, vc, ic, lfc = xs
        b = jnp.cumsum(lfc, axis=-1)
        D = jnp.where(causal, b[..., :, None] - b[..., None, :] + ic[..., None, :], -jnp.inf)
        m_inter = b + m[..., None]
        m_t = jnp.maximum(m_inter, jnp.max(D, axis=-1))
        w_intra = jnp.exp(D - m_t[..., None])
        w_state = jnp.exp(m_inter - m_t)
        s = jnp.einsum('bhtd,bhsd->bhts', qc, kc) * w_intra
        numer = jnp.einsum('bhts,bhsd->bhtd', s, vc) + w_state[..., None] * jnp.einsum('bhvk,bhtk->bhtv', C, qc)
        denom = jnp.sum(s, axis=-1) + w_state * jnp.einsum('bhk,bhtk->bht', n, qc)
        h = numer / jnp.maximum(jnp.abs(denom), jnp.exp(-m_t))[..., None]
        bL = b[..., -1]
        lw = bL[..., None] - b + ic
        m_new = jnp.maximum(bL + m, jnp.max(lw, axis=-1))
        ws = jnp.exp(lw - m_new[..., None])
        dec = jnp.exp(bL + m - m_new)
        C = dec[..., None, None] * C + jnp.einsum('bhs,bhsv,bhsk->bhvk', ws, vc, kc)
        n = dec[..., None] * n + jnp.einsum('bhs,bhsk->bhk', ws, kc)
        return (C, n, m_new), h

    init = (jnp.zeros((B, H, d, d), jnp.float32), jnp.zeros((B, H, d), jnp.float32), jnp.zeros((B, H), jnp.float32))
    _, h = lax.scan(step, init, (chunks(q), chunks(k), chunks(v), chunks(i_pre), chunks(lf)))
    return jnp.moveaxis(h, 0, 2).reshape(B, H, S, d)


def mlstm_mixer(m_x, m_v, m_o, m_if, conv_w, conv_b, w_q, w_k, i_bias, f_bias, head_norm, skip):
    B, S, _ = m_x.shape
    pad = MLSTM_CONV // 2
    xc = lax.conv_general_dilated(m_x, conv_w[:, None, :], window_strides=(1,), padding=[(pad, pad)],
                                  dimension_numbers=('NWC', 'WIO', 'NWC'), feature_group_count=MLSTM_WIDTH)
    xc = jax.nn.silu(xc + conv_b)
    xh = xc.reshape(B, S, MLSTM_HEADS, MLSTM_HEAD_DIM)
    q = jnp.einsum('bshd,hde->bhse', xh, w_q).astype(jnp.float32)
    k = jnp.einsum('bshd,hde->bhse', xh, w_k).astype(jnp.float32)
    v = m_v.reshape(B, S, MLSTM_HEADS, MLSTM_HEAD_DIM).transpose(0, 2, 1, 3).astype(jnp.float32)
    g = m_if.astype(jnp.float32).reshape(B, S, 2, 2, MLSTM_HEADS)
    i_pre = jnp.transpose(g[:, :, 0] + i_bias.astype(jnp.float32), (2, 0, 3, 1))
    f_pre = jnp.transpose(g[:, :, 1] + f_bias.astype(jnp.float32), (2, 0, 3, 1))
    h_fwd = mlstm_chunkwise(q, k, v, i_pre[0], f_pre[0])
    flip = lambda t: jnp.flip(t, axis=2)
    h_bwd = flip(mlstm_chunkwise(flip(q), flip(k), flip(v), flip(i_pre[1]), flip(f_pre[1])))
    h = (h_fwd + h_bwd).transpose(0, 2, 1, 3)
    h = h * lax.rsqrt(jnp.mean(h * h, axis=-1, keepdims=True) + EPS)
    h = h.reshape(B, S, MLSTM_WIDTH) * head_norm.astype(jnp.float32)
    out = (h + skip.astype(jnp.float32) * xc.astype(jnp.float32)) * jax.nn.sigmoid(m_o.astype(jnp.float32))
    return out.astype(m_x.dtype)


def fnet_mixer(z, w, b):
    B, S, _ = z.shape
    zg = z.reshape(B, S, FNET_GROUPS, FNET_GROUP_DIM).astype(jnp.float32)
    y = jnp.fft.fftn(zg, axes=(1, 3), norm='ortho').real.astype(z.dtype)
    return jnp.einsum('bsgc,gce->bsge', y, w).reshape(B, S, FNET_WIDTH) + b


def setup_inputs(seed: int = 0) -> dict:
    key = jax.random.key(seed)
    ks = iter(jax.random.split(key, 40))
    f32 = jnp.float32

    def nrm(shape, fan_in):
        return jax.random.normal(next(ks), shape, f32) * fan_in ** -0.5

    def gain(shape):
        return 1.0 + 0.02 * jax.random.normal(next(ks), shape, f32)

    Ld = DEPTH
    x = jax.random.normal(next(ks), (BATCH, SEQ, D_MODEL), f32)
    p = jax.random.normal(next(ks), (DEPTH, BATCH, SEQ, PLE_DIM), f32)
    offsets = jax.random.randint(next(ks), (BATCH, 1), 0, 4096, dtype=jnp.int32)
    positions = offsets + jnp.arange(SEQ, dtype=jnp.int32)[None, :]
    f_lin = jnp.linspace(3.0, 6.0, MLSTM_HEADS, dtype=f32)
    return {
        'x': x,
        'p': p,
        'positions': positions,
        'ffn1_norm': gain((Ld, D_MODEL)),
        'ffn1_w_gate': nrm((Ld, D_MODEL, D_FF), D_MODEL),
        'ffn1_w_up': nrm((Ld, D_MODEL, D_FF), D_MODEL),
        'ffn1_w_down': nrm((Ld, D_FF, D_MODEL), D_FF),
        'mix_norm': gain((Ld, D_MODEL)),
        'w_in': nrm((Ld, D_MODEL, D_IN), D_MODEL),
        'mla_q_norm': gain((Ld, MLA_Q_LORA)),
        'mla_w_uq': nrm((Ld, MLA_Q_LORA, MLA_HEADS * (MLA_NOPE_DIM + MLA_ROPE_DIM)), MLA_Q_LORA),
        'mla_kv_norm': gain((Ld, MLA_KV_LORA)),
        'mla_w_ukv': nrm((Ld, MLA_KV_LORA, MLA_HEADS * (MLA_NOPE_DIM + MLA_V_DIM)), MLA_KV_LORA),
        'mlstm_conv_w': nrm((Ld, MLSTM_CONV, MLSTM_WIDTH), MLSTM_CONV),
        'mlstm_conv_b': 0.02 * jax.random.normal(next(ks), (Ld, MLSTM_WIDTH), f32),
        'mlstm_w_q': nrm((Ld, MLSTM_HEADS, MLSTM_HEAD_DIM, MLSTM_HEAD_DIM), MLSTM_HEAD_DIM),
        'mlstm_w_k': nrm((Ld, MLSTM_HEADS, MLSTM_HEAD_DIM, MLSTM_HEAD_DIM), MLSTM_HEAD_DIM),
        'mlstm_i_bias': 0.1 * jax.random.normal(next(ks), (Ld, 2, MLSTM_HEADS), f32),
        'mlstm_f_bias': f_lin + 0.1 * jax.random.normal(next(ks), (Ld, 2, MLSTM_HEADS), f32),
        'mlstm_head_norm': gain((Ld, MLSTM_WIDTH)),
        'mlstm_skip': gain((Ld, MLSTM_WIDTH)),
        'fnet_w': nrm((Ld, FNET_GROUPS, FNET_GROUP_DIM, FNET_GROUP_DIM), FNET_GROUP_DIM),
        'fnet_b': 0.02 * jax.random.normal(next(ks), (Ld, FNET_WIDTH), f32),
        'w_out': nrm((Ld, MIX_WIDTH, D_MODEL), MIX_WIDTH),
        'ffn2_norm': gain((Ld, D_MODEL)),
        'ffn2_w_gate': nrm((Ld, D_MODEL, D_FF), D_MODEL),
        'ffn2_w_up': nrm((Ld, D_MODEL, D_FF), D_MODEL),
        'ffn2_w_down': nrm((Ld, D_FF, D_MODEL), D_FF),
        'ple_gate_norm': gain((Ld, D_MODEL)),
        'ple_w_gate': nrm((Ld, D_MODEL, D_MODEL), D_MODEL),
        'ple_w_proj': nrm((Ld, PLE_DIM, D_MODEL), PLE_DIM),
        'ple_post_norm': gain((Ld, D_MODEL)),
        'final_norm': gain((D_MODEL,)),
    }


def reference(x, p, positions, ffn1_norm, ffn1_w_gate, ffn1_w_up, ffn1_w_down, mix_norm, w_in,
              mla_q_norm, mla_w_uq, mla_kv_norm, mla_w_ukv, mlstm_conv_w, mlstm_conv_b, mlstm_w_q, mlstm_w_k,
              mlstm_i_bias, mlstm_f_bias, mlstm_head_norm, mlstm_skip, fnet_w, fnet_b, w_out,
              ffn2_norm, ffn2_w_gate, ffn2_w_up, ffn2_w_down, ple_gate_norm, ple_w_gate, ple_w_proj,
              ple_post_norm, final_norm):
    cos, sin = rope_tables(positions)
    h = x
    for i in range(DEPTH):
        h = h + 0.5 * swiglu(rmsnorm(h, ffn1_norm[i]), ffn1_w_gate[i], ffn1_w_up[i], ffn1_w_down[i])
        u = rmsnorm(h, mix_norm[i]) @ w_in[i]
        c_q, c_kv, k_rope, m_x, m_v, m_o, m_if, f_in = split_cols(u, IN_SPLITS)
        y_mla = mla_mixer(c_q, c_kv, k_rope, cos, sin, mla_q_norm[i], mla_w_uq[i], mla_kv_norm[i], mla_w_ukv[i])
        y_mlstm = mlstm_mixer(m_x, m_v, m_o, m_if, mlstm_conv_w[i], mlstm_conv_b[i], mlstm_w_q[i], mlstm_w_k[i],
                              mlstm_i_bias[i], mlstm_f_bias[i], mlstm_head_norm[i], mlstm_skip[i])
        y_fnet = fnet_mixer(f_in, fnet_w[i], fnet_b[i])
        y = jnp.concatenate([y_mla, y_mlstm, y_fnet], axis=-1)
        h = h + y @ w_out[i]
        h = h + 0.5 * swiglu(rmsnorm(h, ffn2_norm[i]), ffn2_w_gate[i], ffn2_w_up[i], ffn2_w_down[i])
        e = rmsnorm(p[i] @ ple_w_proj[i], ple_post_norm[i])
        gate = jax.nn.sigmoid(rmsnorm(h, ple_gate_norm[i]) @ ple_w_gate[i])
        h = h + gate * e
    return rmsnorm(h, final_norm)
```

```python
import functools

import numpy as np
import jax
import jax.numpy as jnp
from jax import lax
from jax.experimental import pallas as pl
from jax.experimental.pallas import tpu as pltpu

F32 = jnp.float32
BF16 = jnp.bfloat16

EPS = 1e-6
ROPE_THETA = 10000.0
LANE = 128
SUBLANE = 8
VMEM_LIMIT = 56 * 1024 * 1024

MLA_HEADS = 4
MLA_Q_LORA = 384
MLA_KV_LORA = 256
MLA_NOPE = 128
MLA_ROPE = 64
MLA_V = 128
MLA_QK_PAD = 256
ML_HEADS = 4
ML_DIM = 64
ML_W = ML_HEADS * ML_DIM
ML_CONV = 5
FN_GROUPS = 4
FN_DIM = 64
FN_W = FN_GROUPS * FN_DIM
FN_N1 = 64
GATE_W = 4 * ML_HEADS

U_CQ = 0
U_CKV = U_CQ + MLA_Q_LORA
U_MX = U_CKV + MLA_KV_LORA
U_MV = U_MX + ML_W
U_MO = U_MV + ML_W
U_FIN = U_MO + ML_W
U_KR = U_FIN + FN_W
U_G = U_KR + LANE
U_TOT = U_G + LANE


def _cparams(sem):
    return pltpu.CompilerParams(dimension_semantics=sem, vmem_limit_bytes=VMEM_LIMIT)


def _const_spec(shape):
    nd = len(shape)
    return pl.BlockSpec(shape, lambda *_: (0,) * nd, pipeline_mode=pl.Buffered(1))


def _rms(x, g):
    ms = jnp.mean(x * x, axis=-1, keepdims=True)
    return x * lax.rsqrt(ms + EPS) * g


def _dot(a, b):
    return jnp.dot(a, b, preferred_element_type=F32)


def _dot_f32(a, b):
    return jnp.dot(a, b, preferred_element_type=F32, precision=lax.Precision.HIGHEST)


def _swiglu_half(x, g_ref, wgu_ref, wd_ref, acc_ref):
    tf = wd_ref.shape[1]
    xn = _rms(x, g_ref[...]).astype(BF16)
    acc_ref[...] = jnp.zeros_like(acc_ref)

    def body(c, carry):
        gu = _dot(xn, wgu_ref[c])
        g = gu[:, :tf]
        u = gu[:, tf:]
        a = (g * jax.nn.sigmoid(g) * u).astype(BF16)
        acc_ref[...] += _dot(a, wd_ref[c])
        return carry

    lax.fori_loop(0, wgu_ref.shape[0], body, 0)
    return 0.5 * acc_ref[...]


def _ffn_kernel(h_ref, g_ref, wgu_ref, wd_ref, o_ref, acc_ref):
    x = h_ref[...]
    o_ref[...] = x + _swiglu_half(x, g_ref, wgu_ref, wd_ref, acc_ref)


def _ffn(h, g, wgu, wd, tm):
    T, D = h.shape
    return pl.pallas_call(
        _ffn_kernel,
        grid=(T // tm,),
        in_specs=[pl.BlockSpec((tm, D), lambda i: (i, 0)),
                  _const_spec(g.shape), _const_spec(wgu.shape), _const_spec(wd.shape)],
        out_specs=pl.BlockSpec((tm, D), lambda i: (i, 0)),
        out_shape=jax.ShapeDtypeStruct((T, D), F32),
        scratch_shapes=[pltpu.VMEM((tm, D), F32)],
        compiler_params=_cparams(("parallel",)),
        name="ffn",
    )(h, g, wgu, wd)


def _rope_kernel(pos_ref, tab_ref, cos_ref, sina_ref, sinb_ref):
    ang = pos_ref[...].astype(F32) * tab_ref[0:1, :]
    c = jnp.cos(ang)
    s = jnp.sin(ang)
    cos_ref[...] = c * tab_ref[1:2, :]
    sina_ref[...] = s * tab_ref[2:3, :]
    sinb_ref[...] = s * tab_ref[3:4, :]


def _rope_tables(pos, tab, tm):
    T = pos.shape[0]
    out = jax.ShapeDtypeStruct((T, LANE), F32)
    spec = pl.BlockSpec((tm, LANE), lambda i: (i, 0))
    return pl.pallas_call(
        _rope_kernel,
        grid=(T // tm,),
        in_specs=[pl.BlockSpec((tm, 1), lambda i: (i, 0)), _const_spec(tab.shape)],
        out_specs=[spec, spec, spec],
        out_shape=[out, out, out],
        compiler_params=_cparams(("parallel",)),
        name="rope_tables",
    )(pos, tab)


def _rope(blk, cos_t, sin_a, sin_b):
    half = MLA_ROPE // 2
    return (blk * cos_t + pltpu.roll(blk, half, 1) * sin_a
            + pltpu.roll(blk, LANE - half, 1) * sin_b)


def _mixin_kernel(h_ref, cos_ref, sina_ref, sinb_ref, g_ref, win_ref, qn_ref, wuq_ref, kvn_ref,
                  wukv_ref, dft_ref,
                  q_ref, kt_ref, v_ref, mx_ref, mv_ref, mo_ref, gate_ref, fr_ref, fi_ref, *, qscale):
    x = h_ref[0]
    xn = _rms(x, g_ref[...]).astype(BF16)
    u = _dot(xn, win_ref[...])
    cos_t, sin_a, sin_b = cos_ref[0], sina_ref[0], sinb_ref[0]

    cq = _rms(u[:, U_CQ:U_CKV], qn_ref[...]).astype(BF16)
    q = _dot(cq, wuq_ref[...])
    for hd in range(MLA_HEADS):
        o = hd * MLA_QK_PAD
        q_ref[0, :, o:o + MLA_NOPE] = (q[:, o:o + MLA_NOPE] * qscale).astype(BF16)
        qr = _rope(q[:, o + MLA_NOPE:o + MLA_QK_PAD], cos_t, sin_a, sin_b)
        q_ref[0, :, o + MLA_NOPE:o + MLA_QK_PAD] = (qr * qscale).astype(BF16)

    ckv = _rms(u[:, U_CKV:U_MX], kvn_ref[...]).astype(BF16)
    kv = _dot(ckv, wukv_ref[...])
    kr_t = _rope(u[:, U_KR:U_KR + LANE], cos_t, sin_a, sin_b).T.astype(BF16)
    for hd in range(MLA_HEADS):
        kn_t = kv[:, hd * MLA_NOPE:(hd + 1) * MLA_NOPE].T.astype(BF16)
        kt_ref[0, hd, 0:MLA_NOPE, :] = kn_t
        kt_ref[0, hd, MLA_NOPE:MLA_QK_PAD, :] = kr_t
    v_ref[0] = kv[:, MLA_HEADS * MLA_NOPE:].astype(BF16)

    mx_ref[0] = u[:, U_MX:U_MV]
    mv_ref[0] = u[:, U_MV:U_MO].astype(BF16)
    mo_ref[0] = jax.nn.sigmoid(u[:, U_MO:U_FIN])
    gate_ref[0] = u[:, U_G:U_TOT]

    f = _dot(u[:, U_FIN:U_KR].astype(BF16), dft_ref[...])
    fr_ref[0] = f[:, :FN_W].astype(BF16)
    fi_ref[0] = f[:, FN_W:].astype(BF16)


def _mixin(h3, cos_t, sin_a, sin_b, g, win, qn, wuq, kvn, wukv, dft, tm, qscale):
    B, S, D = h3.shape
    tok = lambda w: pl.BlockSpec((1, tm, w), lambda b, i: (b, i, 0))
    outs = [
        (jax.ShapeDtypeStruct((B, S, MLA_HEADS * MLA_QK_PAD), BF16), tok(MLA_HEADS * MLA_QK_PAD)),
        (jax.ShapeDtypeStruct((B, MLA_HEADS, MLA_QK_PAD, S), BF16),
         pl.BlockSpec((1, MLA_HEADS, MLA_QK_PAD, tm), lambda b, i: (b, 0, 0, i))),
        (jax.ShapeDtypeStruct((B, S, MLA_HEADS * MLA_V), BF16), tok(MLA_HEADS * MLA_V)),
        (jax.ShapeDtypeStruct((B, S, ML_W), F32), tok(ML_W)),
        (jax.ShapeDtypeStruct((B, S, ML_W), BF16), tok(ML_W)),
        (jax.ShapeDtypeStruct((B, S, ML_W), F32), tok(ML_W)),
        (jax.ShapeDtypeStruct((B, S, LANE), F32), tok(LANE)),
        (jax.ShapeDtypeStruct((B, S, FN_W), BF16), tok(FN_W)),
        (jax.ShapeDtypeStruct((B, S, FN_W), BF16), tok(FN_W)),
    ]
    consts = [g, win, qn, wuq, kvn, wukv, dft]
    return pl.pallas_call(
        functools.partial(_mixin_kernel, qscale=qscale),
        grid=(B, S // tm),
        in_specs=[tok(D), tok(LANE), tok(LANE), tok(LANE)] + [_const_spec(c.shape) for c in consts],
        out_specs=[o[1] for o in outs],
        out_shape=[o[0] for o in outs],
        compiler_params=_cparams(("parallel", "parallel")),
        name="mix_in",
    )(h3, cos_t, sin_a, sin_b, *consts)


def _attn_kernel(q_ref, kt_ref, v_ref, o_ref, m_ref, l_ref, acc_ref, *, tk):
    q = q_ref[0]
    S = kt_ref.shape[3]
    m_ref[...] = jnp.full_like(m_ref, -jnp.inf)
    l_ref[...] = jnp.zeros_like(l_ref)
    acc_ref[...] = jnp.zeros_like(acc_ref)

    def body(j, carry):
        ks = pl.multiple_of(j * tk, tk)
        s = _dot(q, kt_ref[0, 0, :, pl.ds(ks, tk)])
        m_old = m_ref[...]
        m_new = jnp.maximum(m_old, jnp.max(s, axis=-1, keepdims=True))
        p = jnp.exp(s - m_new)
        alpha = jnp.exp(m_old - m_new)
        l_ref[...] = alpha * l_ref[...] + jnp.sum(p, axis=-1, keepdims=True)
        acc_ref[...] = alpha * acc_ref[...] + _dot(p.astype(BF16), v_ref[0, pl.ds(ks, tk), :])
        m_ref[...] = m_new
        return carry

    lax.fori_loop(0, S // tk, body, 0)
    o_ref[0] = (acc_ref[...] / l_ref[...]).astype(o_ref.dtype)


def _attention(q, kt, v, tq, tk):
    B, S, _ = q.shape
    return pl.pallas_call(
        functools.partial(_attn_kernel, tk=tk),
        grid=(B, MLA_HEADS, S // tq),
        in_specs=[pl.BlockSpec((1, tq, MLA_QK_PAD), lambda b, h, i: (b, i, h)),
                  pl.BlockSpec((1, 1, MLA_QK_PAD, S), lambda b, h, i: (b, h, 0, 0)),
                  pl.BlockSpec((1, S, MLA_V), lambda b, h, i: (b, 0, h))],
        out_specs=pl.BlockSpec((1, tq, MLA_V), lambda b, h, i: (b, i, h)),
        out_shape=jax.ShapeDtypeStruct((B, S, MLA_HEADS * MLA_V), BF16),
        scratch_shapes=[pltpu.VMEM((tq, 1), F32), pltpu.VMEM((tq, 1), F32),
                        pltpu.VMEM((tq, MLA_V), F32)],
        compiler_params=_cparams(("parallel", "parallel", "arbitrary")),
        name="mla_attention",
    )(q, kt, v)


def _mlpre_kernel(cur_ref, prev_ref, next_ref, gate_ref, cw_ref, cb_ref, wqk_ref, gb_ref,
                  xc_ref, q_ref, kt_ref, gi_ref, gt_ref, ext_ref):
    i = pl.program_id(1)
    ts = cur_ref.shape[1]
    halo = SUBLANE
    pad = ML_CONV // 2
    ext_ref[0:halo, :] = jnp.where(i > 0, prev_ref[0], 0.0)
    ext_ref[halo:halo + ts, :] = cur_ref[0]
    ext_ref[halo + ts:, :] = jnp.where(i < pl.num_programs(1) - 1, next_ref[0], 0.0)
    xc = cb_ref[...] + cw_ref[0:1, :] * ext_ref[halo - pad:halo - pad + ts, :]
    for j in range(1, ML_CONV):
        o = halo - pad + j
        xc = xc + cw_ref[j:j + 1, :] * ext_ref[o:o + ts, :]
    xc = xc * jax.nn.sigmoid(xc)
    xc_ref[0] = xc
    qk = _dot(xc.astype(BF16), wqk_ref[...])
    q_ref[0] = qk[:, :ML_W].astype(BF16)
    kt_ref[0] = qk[:, ML_W:].T.astype(BF16)
    gpre = gate_ref[0] + gb_ref[...]
    lf = jnp.minimum(gpre, 0.0) - jnp.log(1.0 + jnp.exp(-jnp.abs(gpre)))
    lane = lax.broadcasted_iota(jnp.int32, gpre.shape, 1)
    gi = jnp.where(lane < 2 * ML_HEADS, gpre, lf)
    gi_ref[0] = gi
    gt_ref[0] = gi.T[0:GATE_W, :]


def _mlstm_pre(mx, gate, cw, cb, wqk, gb, ts):
    B, S, W = mx.shape
    nblk = ts // SUBLANE
    last = S // SUBLANE - 1
    tok = lambda w: pl.BlockSpec((1, ts, w), lambda b, i: (b, i, 0))
    return pl.pallas_call(
        _mlpre_kernel,
        grid=(B, S // ts),
        in_specs=[tok(W),
                  pl.BlockSpec((1, SUBLANE, W), lambda b, i: (b, jnp.maximum(i * nblk - 1, 0), 0)),
                  pl.BlockSpec((1, SUBLANE, W), lambda b, i: (b, jnp.minimum((i + 1) * nblk, last), 0)),
                  tok(LANE), _const_spec(cw.shape), _const_spec(cb.shape), _const_spec(wqk.shape),
                  _const_spec(gb.shape)],
        out_specs=[tok(W), tok(W), pl.BlockSpec((1, W, ts), lambda b, i: (b, 0, i)), tok(LANE),
                   pl.BlockSpec((1, GATE_W, ts), lambda b, i: (b, 0, i))],
        out_shape=[jax.ShapeDtypeStruct((B, S, W), F32), jax.ShapeDtypeStruct((B, S, W), BF16),
                   jax.ShapeDtypeStruct((B, W, S), BF16), jax.ShapeDtypeStruct((B, S, LANE), F32),
                   jax.ShapeDtypeStruct((B, GATE_W, S), F32)],
        scratch_shapes=[pltpu.VMEM((ts + 2 * SUBLANE, W), F32)],
        compiler_params=_cparams(("parallel", "parallel")),
        name="mlstm_pre",
    )(mx, mx, mx, gate, cw, cb, wqk, gb)


def _mlstm_direction(d, q, kt, v, gi, gt, c_ref, n_ref, m_ref, si):
    L = q.shape[0]
    H, dh = ML_HEADS, ML_DIM
    row = lax.broadcasted_iota(jnp.int32, (L, L), 0)
    col = lax.broadcasted_iota(jnp.int32, (L, L), 1)
    keep = (col <= row) if d == 0 else (col >= row)
    tri = keep.astype(F32)
    b_col = _dot_f32(tri, gi)
    b_row = _dot_f32(gt, tri.T)
    edge = L - 1 if d == 0 else 0
    lane_w = lax.broadcasted_iota(jnp.int32, (1, ML_W), 1)
    row_w = lax.broadcasted_iota(jnp.int32, (ML_W, 1), 0)

    qf = q.astype(F32)
    inter = _dot(q, c_ref[si].astype(BF16))
    sel = (lax.broadcasted_iota(jnp.int32, (ML_W, LANE), 0) // dh
           == lax.broadcasted_iota(jnp.int32, (ML_W, LANE), 1))
    nmat = jnp.where(sel, n_ref[si], 0.0).astype(BF16)
    qn = _dot(q, nmat)

    out = jnp.zeros((L, ML_W), F32)
    ws_rows, dec_lane, dec_rows = [], jnp.zeros((1, ML_W), F32), []
    m_news = []
    for hd in range(H):
        j = d * H + hd
        m_old = m_ref[si, hd:hd + 1, 0:1]
        bc = b_col[:, 2 * H + j:2 * H + j + 1]
        br = b_row[2 * H + j:2 * H + j + 1, :]
        ir = gt[j:j + 1, :]
        ic = gi[:, j:j + 1]
        dmat = jnp.where(keep, bc - br + ir, -jnp.inf)
        m_inter = bc + m_old
        m_t = jnp.maximum(m_inter, jnp.max(dmat, axis=-1, keepdims=True))
        w_intra = jnp.exp(dmat - m_t)
        w_state = jnp.exp(m_inter - m_t)
        hmask = (lane_w // dh) == hd
        qh = jnp.where(hmask, qf, 0.0).astype(BF16)
        s = _dot(qh, kt) * w_intra
        denom = jnp.sum(s, axis=-1, keepdims=True) + w_state * qn[:, hd:hd + 1]
        numer = _dot(s.astype(BF16), v) + w_state * inter
        scale = 1.0 / jnp.maximum(jnp.abs(denom), jnp.exp(-m_t))
        out = out + jnp.where(hmask, numer * scale, 0.0)
        b_edge = br[:, edge:edge + 1]
        lw = b_edge - br + ir
        m_new = jnp.maximum(b_edge + m_old, jnp.max(lw, axis=-1, keepdims=True))
        ws_rows.append(jnp.broadcast_to(jnp.exp(lw - m_new), (dh, L)))
        dec = jnp.exp(b_edge + m_old - m_new)
        dec_lane = dec_lane + jnp.where(hmask, dec, 0.0)
        dec_rows.append(jnp.broadcast_to(dec, (dh, LANE)))
        m_news.append(jnp.broadcast_to(m_new, (1, LANE)))
        del ic

    kw = kt.astype(F32) * jnp.concatenate(ws_rows, axis=0)
    blockdiag = (row_w // dh) == (lane_w // dh)
    c_new = c_ref[si] * dec_lane + _dot(kw.astype(BF16), v)
    c_ref[si] = jnp.where(blockdiag, c_new, 0.0)
    n_new = n_ref[si] * jnp.concatenate(dec_rows, axis=0) + jnp.sum(kw, axis=-1, keepdims=True)
    n_ref[si] = n_new
    m_ref[si, 0:H, :] = jnp.concatenate(m_news, axis=0)
    return out


def _mlscan_kernel(qf_ref, ktf_ref, vf_ref, gif_ref, gtf_ref, qb_ref, ktb_ref, vb_ref, gib_ref, gtb_ref,
                   hf_ref, hb_ref, c_ref, n_ref, m_ref, *, bb):
    @pl.when(pl.program_id(1) == 0)
    def _():
        c_ref[...] = jnp.zeros_like(c_ref)
        n_ref[...] = jnp.zeros_like(n_ref)
        m_ref[...] = jnp.zeros_like(m_ref)

    for b in range(bb):
        hf_ref[b] = _mlstm_direction(0, qf_ref[b], ktf_ref[b], vf_ref[b], gif_ref[b], gtf_ref[b],
                                     c_ref, n_ref, m_ref, 2 * b)
        hb_ref[b] = _mlstm_direction(1, qb_ref[b], ktb_ref[b], vb_ref[b], gib_ref[b], gtb_ref[b],
                                     c_ref, n_ref, m_ref, 2 * b + 1)


def _mlstm_scan(q, kt, v, gi, gt, L, bb):
    B, S, W = q.shape
    nc = S // L
    fw = lambda b, c: c
    bw = lambda b, c: nc - 1 - c
    def specs(cm):
        return [pl.BlockSpec((bb, L, W), lambda b, c: (b, cm(b, c), 0)),
                pl.BlockSpec((bb, W, L), lambda b, c: (b, 0, cm(b, c))),
                pl.BlockSpec((bb, L, W), lambda b, c: (b, cm(b, c), 0)),
                pl.BlockSpec((bb, L, LANE), lambda b, c: (b, cm(b, c), 0)),
                pl.BlockSpec((bb, GATE_W, L), lambda b, c: (b, 0, cm(b, c)))]
    out = jax.ShapeDtypeStruct((B, S, W), F32)
    return pl.pallas_call(
        functools.partial(_mlscan_kernel, bb=bb),
        grid=(B // bb, nc),
        in_specs=specs(fw) + specs(bw),
        out_specs=[pl.BlockSpec((bb, L, W), lambda b, c: (b, c, 0)),
                   pl.BlockSpec((bb, L, W), lambda b, c: (b, nc - 1 - c, 0))],
        out_shape=[out, out],
        scratch_shapes=[pltpu.VMEM((2 * bb, W, W), F32), pltpu.VMEM((2 * bb, W, LANE), F32),
                        pltpu.VMEM((2 * bb, SUBLANE, LANE), F32)],
        compiler_params=_cparams(("parallel", "arbitrary")),
        name="mlstm_scan",
    )(q, kt, v, gi, gt, q, kt, v, gi, gt)


def _fnet1_kernel(wr_ref, wi_ref, m1_ref, a_ref):
    n1 = wr_ref.shape[1]
    a = _dot(m1_ref[:, 0:n1], wr_ref[0]) + _dot(m1_ref[:, n1:], wi_ref[0])
    a_ref[0, 0] = a[0:n1].astype(BF16)
    a_ref[0, 1] = a[n1:].astype(BF16)


def _fnet3_kernel(a_ref, tab_ref, w_ref, b_ref, o_ref):
    for j in range(SUBLANE):
        z = jnp.concatenate([a_ref[0, 0, j], a_ref[0, 1, j]], axis=0)
        y = _dot(tab_ref[j], z)
        o_ref[0, :, j, :] = _dot(y.astype(BF16), w_ref[...]) + b_ref[...]


def _fnet(fr, fi, m1, tab3, wbd, bias, tc):
    B, S, W = fr.shape
    n1 = FN_N1
    n2 = S // n1
    cols = n2 * W
    wr = fr.reshape(B, n1, cols)
    wi = fi.reshape(B, n1, cols)
    a = pl.pallas_call(
        _fnet1_kernel,
        grid=(B, cols // tc),
        in_specs=[pl.BlockSpec((1, n1, tc), lambda b, i: (b, 0, i)),
                  pl.BlockSpec((1, n1, tc), lambda b, i: (b, 0, i)),
                  _const_spec(m1.shape)],
        out_specs=pl.BlockSpec((1, 2, n1, tc), lambda b, i: (b, 0, 0, i)),
        out_shape=jax.ShapeDtypeStruct((B, 2, n1, cols), BF16),
        compiler_params=_cparams(("parallel", "parallel")),
        name="fnet_dft1",
    )(wr, wi, m1)
    a = a.reshape(B, 2, n1, n2, W)
    y = pl.pallas_call(
        _fnet3_kernel,
        grid=(B, n1 // SUBLANE),
        in_specs=[pl.BlockSpec((1, 2, SUBLANE, n2, W), lambda b, i: (b, 0, i, 0, 0)),
                  pl.BlockSpec((SUBLANE, n2, 2 * n2), lambda b, i: (i, 0, 0)),
                  _const_spec(wbd.shape), _const_spec(bias.shape)],
        out_specs=pl.BlockSpec((1, n2, SUBLANE, W), lambda b, i: (b, 0, i, 0)),
        out_shape=jax.ShapeDtypeStruct((B, n2, n1, W), F32),
        compiler_params=_cparams(("parallel", "parallel")),
        name="fnet_dft2",
    )(a, tab3, wbd, bias)
    return y.reshape(B, S, W)


def _post_kernel(h_ref, ya_ref, hf_ref, hb_ref, xc_ref, mo_ref, yf_ref, p_ref,
                 hn_ref, sk_ref, ones_ref, wo_ref, g2_ref, wgu_ref, wd_ref,
                 gn_ref, wg_ref, wp_ref, pn_ref, fn_ref, o_ref, acc_ref, *, final):
    hm = hf_ref[...] + hb_ref[...]
    sq = hm * hm
    hi = sq.astype(BF16)
    lo = (sq - hi.astype(F32)).astype(BF16)
    ms = (_dot(hi, ones_ref[...]) + _dot(lo, ones_ref[...])) * (1.0 / ML_DIM)
    ym = (hm * lax.rsqrt(ms + EPS) * hn_ref[...] + sk_ref[...] * xc_ref[...]) * mo_ref[...]
    a_w = ya_ref.shape[1]
    m_w = a_w + ML_W
    h1 = (h_ref[...] + _dot(ya_ref[...], wo_ref[0:a_w, :])
          + _dot(ym.astype(BF16), wo_ref[a_w:m_w, :])
          + _dot(yf_ref[...].astype(BF16), wo_ref[m_w:, :]))
    h2 = h1 + _swiglu_half(h1, g2_ref, wgu_ref, wd_ref, acc_ref)
    e = _rms(_dot(p_ref[...].astype(BF16), wp_ref[...]), pn_ref[...])
    gate = jax.nn.sigmoid(_dot(_rms(h2, gn_ref[...]).astype(BF16), wg_ref[...]))
    h3 = h2 + gate * e
    if final:
        h3 = _rms(h3, fn_ref[...])
    o_ref[...] = h3


def _post(h, ya, hf, hb, xc, mo, yf, p, consts, tm, final):
    T, D = h.shape
    tok = lambda w: pl.BlockSpec((tm, w), lambda i: (i, 0))
    acts = [h, ya, hf, hb, xc, mo, yf, p]
    return pl.pallas_call(
        functools.partial(_post_kernel, final=final),
        grid=(T // tm,),
        in_specs=[tok(a.shape[1]) for a in acts] + [_const_spec(c.shape) for c in consts],
        out_specs=tok(D),
        out_shape=jax.ShapeDtypeStruct((T, D), F32),
        scratch_shapes=[pltpu.VMEM((tm, D), F32)],
        compiler_params=_cparams(("parallel",)),
        name="post",
    )(*acts, *consts)


def _block_diag(w):
    G, a, b = w.shape
    eye = jnp.eye(G, dtype=w.dtype)
    return (eye[:, None, :, None] * w[:, :, None, :]).reshape(G * a, G * b)


def _ffn_weights(w_gate, w_up, w_down, tf):
    D, FF = w_gate.shape
    nc = FF // tf
    wgu = jnp.concatenate([w_gate.reshape(D, nc, tf), w_up.reshape(D, nc, tf)], axis=-1)
    return wgu.transpose(1, 0, 2).astype(BF16), w_down.reshape(nc, tf, D).astype(BF16)


def _dft_tables(S):
    n1, n2 = FN_N1, S // FN_N1
    c = np.arange(FN_DIM)
    ang = 2.0 * np.pi * np.outer(c, c) / FN_DIM
    eye = np.eye(FN_GROUPS)
    chan = np.concatenate([np.kron(eye, np.cos(ang)), -np.kron(eye, np.sin(ang))], axis=1) / 8.0
    k1 = np.arange(n1)
    a1 = 2.0 * np.pi * np.outer(k1, k1) / n1
    m1 = np.block([[np.cos(a1), np.sin(a1)], [-np.sin(a1), np.cos(a1)]])
    k = k1[:, None, None] + n1 * np.arange(n2)[None, :, None]
    s2 = np.arange(n2)[None, None, :]
    a3 = 2.0 * np.pi * ((k * s2) % S) / S
    tab3 = np.concatenate([np.cos(a3), np.sin(a3)], axis=-1) / np.sqrt(S)
    bf = lambda t: jnp.asarray(t, dtype=F32).astype(BF16)
    return bf(chan), bf(m1), bf(tab3)


def _rope_consts():
    half = MLA_ROPE // 2
    inv = 1.0 / (ROPE_THETA ** (jnp.arange(0, MLA_ROPE, 2, dtype=F32) / MLA_ROPE))
    z = jnp.zeros((half,), F32)
    o = jnp.ones((half,), F32)
    rows = [jnp.concatenate([inv, inv, z, z]), jnp.concatenate([o, o, z, z]),
            jnp.concatenate([z, o, z, z]), jnp.concatenate([-o, z, z, z])]
    rows += [jnp.zeros((LANE,), F32)] * (SUBLANE - len(rows))
    return jnp.stack(rows)


def _tiles(B, S):
    T = B * S
    return dict(tm=min(512, T), tf=256, tmix=min(512, S), tq=min(512, S), tk=min(512, S),
                ts=min(512, S), chunk=min(128, S), bb=1, tc=min(4096, (S // FN_N1) * FN_W),
                trope=min(1024, T))


def kernel(x, p, positions, ffn1_norm, ffn1_w_gate, ffn1_w_up, ffn1_w_down, mix_norm, w_in, mla_q_norm, mla_w_uq, mla_kv_norm, mla_w_ukv, mlstm_conv_w, mlstm_conv_b, mlstm_w_q, mlstm_w_k, mlstm_i_bias, mlstm_f_bias, mlstm_head_norm, mlstm_skip, fnet_w, fnet_b, w_out, ffn2_norm, ffn2_w_gate, ffn2_w_up, ffn2_w_down, ple_gate_norm, ple_w_gate, ple_w_proj, ple_post_norm, final_norm):
    B, S, D = x.shape
    depth = p.shape[0]
    T = B * S
    t = _tiles(B, S)
    row = lambda a: a.reshape(1, -1).astype(F32)

    tab = _rope_consts()
    cos_t, sin_a, sin_b = [a.reshape(B, S, LANE) for a in
                           _rope_tables(positions.reshape(T, 1).astype(jnp.int32), tab, t["trope"])]
    dft_chan, dft_m1, dft_tab3 = _dft_tables(S)
    ones_bd = _block_diag(jnp.ones((ML_HEADS, ML_DIM, ML_DIM), BF16))
    qscale = float((MLA_NOPE + MLA_ROPE) ** -0.5)

    h = x.reshape(T, D)
    for i in range(depth):
        wgu1, wd1 = _ffn_weights(ffn1_w_gate[i], ffn1_w_up[i], ffn1_w_down[i], t["tf"])
        wgu2, wd2 = _ffn_weights(ffn2_w_gate[i], ffn2_w_up[i], ffn2_w_down[i], t["tf"])
        wi = w_in[i]
        o_kr = MLA_Q_LORA + MLA_KV_LORA
        o_mx = o_kr + MLA_ROPE
        o_g = o_mx + 3 * ML_W
        o_f = o_g + GATE_W
        zpad = lambda n: jnp.zeros((D, n), wi.dtype)
        win = jnp.concatenate([wi[:, :o_kr], wi[:, o_mx:o_g], wi[:, o_f:], wi[:, o_kr:o_mx],
                               zpad(LANE - MLA_ROPE), wi[:, o_g:o_f], zpad(LANE - GATE_W)],
                              axis=1).astype(BF16)
        wuq = jnp.pad(mla_w_uq[i].reshape(MLA_Q_LORA, MLA_HEADS, MLA_NOPE + MLA_ROPE),
                      ((0, 0), (0, 0), (0, MLA_QK_PAD - MLA_NOPE - MLA_ROPE)))
        wuq = wuq.reshape(MLA_Q_LORA, MLA_HEADS * MLA_QK_PAD).astype(BF16)
        wukv = mla_w_ukv[i].reshape(MLA_KV_LORA, MLA_HEADS, 2, MLA_NOPE).transpose(0, 2, 1, 3)
        wukv = wukv.reshape(MLA_KV_LORA, 2 * MLA_HEADS * MLA_NOPE).astype(BF16)
        wqk = jnp.concatenate([_block_diag(mlstm_w_q[i]) * (ML_DIM ** -0.5), _block_diag(mlstm_w_k[i])],
                              axis=1).astype(BF16)
        gbias = jnp.concatenate([mlstm_i_bias[i].reshape(-1), mlstm_f_bias[i].reshape(-1),
                                 jnp.zeros((LANE - GATE_W,), F32)]).reshape(1, LANE)
        fw_bd = _block_diag(fnet_w[i]).astype(BF16)

        h = _ffn(h, row(ffn1_norm[i]), wgu1, wd1, t["tm"])
        q, kt, v, mx, mv, mo, gate, fr, fi = _mixin(
            h.reshape(B, S, D), cos_t, sin_a, sin_b, row(mix_norm[i]), win, row(mla_q_norm[i]), wuq,
            row(mla_kv_norm[i]), wukv, dft_chan, t["tmix"], qscale)
        y_mla = _attention(q, kt, v, t["tq"], t["tk"])
        xc, mq, mkt, gi, gt = _mlstm_pre(mx, gate, mlstm_conv_w[i].astype(F32), row(mlstm_conv_b[i]),
                                         wqk, gbias, t["ts"])
        hf, hb = _mlstm_scan(mq, mkt, mv, gi, gt, t["chunk"], t["bb"])
        y_fnet = _fnet(fr, fi, dft_m1, dft_tab3, fw_bd, row(fnet_b[i]), t["tc"])
        consts = [row(mlstm_head_norm[i]), row(mlstm_skip[i]), ones_bd, w_out[i].astype(BF16),
                  row(ffn2_norm[i]), wgu2, wd2, row(ple_gate_norm[i]), ple_w_gate[i].astype(BF16),
                  ple_w_proj[i].astype(BF16), row(ple_post_norm[i]), row(final_norm)]
        flat = lambda a: a.reshape(T, a.shape[-1])
        h = _post(h, flat(y_mla), flat(hf), flat(hb), flat(xc), flat(mo), flat(y_fnet),
                  p[i].reshape(T, -1), consts, t["tm"], final=(i == depth - 1))
    return h.reshape(B, S, D)
```

```python
import functools

import numpy as np
import jax
import jax.numpy as jnp
from jax import lax
from jax.experimental import pallas as pl
from jax.experimental.pallas import tpu as pltpu

F32 = jnp.float32
BF16 = jnp.bfloat16

EPS = 1e-6
ROPE_THETA = 10000.0
LANE = 128
SUBLANE = 8
VMEM_LIMIT = 56 * 1024 * 1024

MLA_HEADS = 4
MLA_Q_LORA = 384
MLA_KV_LORA = 256
MLA_NOPE = 128
MLA_ROPE = 64
MLA_V = 128
MLA_QK_PAD = 256
ML_HEADS = 4
ML_DIM = 64
ML_W = ML_HEADS * ML_DIM
ML_CONV = 5
FN_GROUPS = 4
FN_DIM = 64
FN_W = FN_GROUPS * FN_DIM
FN_N1 = 64
GATE_W = 4 * ML_HEADS

U_CQ = 0
U_CKV = U_CQ + MLA_Q_LORA
U_MX = U_CKV + MLA_KV_LORA
U_MV = U_MX + ML_W
U_MO = U_MV + ML_W
U_FIN = U_MO + ML_W
U_KR = U_FIN + FN_W
U_G = U_KR + LANE
U_TOT = U_G + LANE


def _cparams(sem):
    return pltpu.CompilerParams(dimension_semantics=sem, vmem_limit_bytes=VMEM_LIMIT)


def _const_spec(shape):
    nd = len(shape)
    return pl.BlockSpec(shape, lambda *_: (0,) * nd, pipeline_mode=pl.Buffered(1))


def _rms(x, g):
    ms = jnp.mean(x * x, axis=-1, keepdims=True)
    return x * lax.rsqrt(ms + EPS) * g


def _dot(a, b):
    return jnp.dot(a, b, preferred_element_type=F32)


def _dot_f32(a, b):
    return jnp.dot(a, b, preferred_element_type=F32, precision=lax.Precision.HIGHEST)


def _swiglu_half(x, g_ref, wgu_ref, wd_ref, acc_ref):
    tf = wd_ref.shape[1]
    xn = _rms(x, g_ref[...]).astype(BF16)
    acc_ref[...] = jnp.zeros_like(acc_ref)

    def body(c, carry):
        gu = _dot(xn, wgu_ref[c])
        g = gu[:, :tf]
        u = gu[:, tf:]
        a = (g * jax.nn.sigmoid(g) * u).astype(BF16)
        acc_ref[...] += _dot(a, wd_ref[c])
        return carry

    lax.fori_loop(0, wgu_ref.shape[0], body, 0)
    return 0.5 * acc_ref[...]


def _ffn_kernel(h_ref, g_ref, wgu_ref, wd_ref, o_ref, acc_ref):
    x = h_ref[...]
    o_ref[...] = x + _swiglu_half(x, g_ref, wgu_ref, wd_ref, acc_ref)


def _ffn(h, g, wgu, wd, tm):
    T, D = h.shape
    return pl.pallas_call(
        _ffn_kernel,
        grid=(T // tm,),
        in_specs=[pl.BlockSpec((tm, D), lambda i: (i, 0)),
                  _const_spec(g.shape), _const_spec(wgu.shape), _const_spec(wd.shape)],
        out_specs=pl.BlockSpec((tm, D), lambda i: (i, 0)),
        out_shape=jax.ShapeDtypeStruct((T, D), F32),
        scratch_shapes=[pltpu.VMEM((tm, D), F32)],
        compiler_params=_cparams(("parallel",)),
        name="ffn",
    )(h, g, wgu, wd)


def _rope_kernel(pos_ref, tab_ref, cos_ref, sina_ref, sinb_ref):
    ang = pos_ref[...].astype(F32) * tab_ref[0:1, :]
    c = jnp.cos(ang)
    s = jnp.sin(ang)
    cos_ref[...] = c * tab_ref[1:2, :]
    sina_ref[...] = s * tab_ref[2:3, :]
    sinb_ref[...] = s * tab_ref[3:4, :]


def _rope_tables(pos, tab, tm):
    T = pos.shape[0]
    out = jax.ShapeDtypeStruct((T, LANE), F32)
    spec = pl.BlockSpec((tm, LANE), lambda i: (i, 0))
    return pl.pallas_call(
        _rope_kernel,
        grid=(T // tm,),
        in_specs=[pl.BlockSpec((tm, 1), lambda i: (i, 0)), _const_spec(tab.shape)],
        out_specs=[spec, spec, spec],
        out_shape=[out, out, out],
        compiler_params=_cparams(("parallel",)),
        name="rope_tables",
    )(pos, tab)


def _rope(blk, cos_t, sin_a, sin_b):
    half = MLA_ROPE // 2
    return (blk * cos_t + pltpu.roll(blk, half, 1) * sin_a
            + pltpu.roll(blk, LANE - half, 1) * sin_b)


def _mixin_kernel(h_ref, cos_ref, sina_ref, sinb_ref, g_ref, win_ref, qn_ref, wuq_ref, kvn_ref,
                  wukv_ref, dft_ref,
                  qt_ref, k_ref, vt_ref, mx_ref, mv_ref, mo_ref, gate_ref, fr_ref, fi_ref, *, qscale):
    x = h_ref[0]
    xn = _rms(x, g_ref[...]).astype(BF16)
    u = _dot(xn, win_ref[...])
    cos_t, sin_a, sin_b = cos_ref[0], sina_ref[0], sinb_ref[0]

    cq = _rms(u[:, U_CQ:U_CKV], qn_ref[...]).astype(BF16)
    q = _dot(cq, wuq_ref[...])
    for hd in range(MLA_HEADS):
        o = hd * MLA_QK_PAD
        qt_ref[0, hd, 0:MLA_NOPE, :] = (q[:, o:o + MLA_NOPE] * qscale).T.astype(BF16)
        qr = _rope(q[:, o + MLA_NOPE:o + MLA_QK_PAD], cos_t, sin_a, sin_b)
        qt_ref[0, hd, MLA_NOPE:MLA_QK_PAD, :] = (qr * qscale).T.astype(BF16)

    ckv = _rms(u[:, U_CKV:U_MX], kvn_ref[...]).astype(BF16)
    kv = _dot(ckv, wukv_ref[...])
    kr = _rope(u[:, U_KR:U_KR + LANE], cos_t, sin_a, sin_b).astype(BF16)
    for hd in range(MLA_HEADS):
        o = hd * MLA_QK_PAD
        k_ref[0, :, o:o + MLA_NOPE] = kv[:, hd * MLA_NOPE:(hd + 1) * MLA_NOPE].astype(BF16)
        k_ref[0, :, o + MLA_NOPE:o + MLA_QK_PAD] = kr
        vo = (MLA_HEADS + hd) * MLA_NOPE
        vt_ref[0, hd] = kv[:, vo:vo + MLA_V].T.astype(BF16)

    mx_ref[0] = u[:, U_MX:U_MV]
    mv_ref[0] = u[:, U_MV:U_MO].astype(BF16)
    mo_ref[0] = jax.nn.sigmoid(u[:, U_MO:U_FIN])
    gate_ref[0] = u[:, U_G:U_TOT]

    f = _dot(u[:, U_FIN:U_KR].astype(BF16), dft_ref[...])
    fr_ref[0] = f[:, :FN_W].astype(BF16)
    fi_ref[0] = f[:, FN_W:].astype(BF16)


def _mixin(h3, cos_t, sin_a, sin_b, g, win, qn, wuq, kvn, wukv, dft, tm, qscale):
    B, S, D = h3.shape
    tok = lambda w: pl.BlockSpec((1, tm, w), lambda b, i: (b, i, 0))
    outs = [
        (jax.ShapeDtypeStruct((B, MLA_HEADS, MLA_QK_PAD, S), BF16),
         pl.BlockSpec((1, MLA_HEADS, MLA_QK_PAD, tm), lambda b, i: (b, 0, 0, i))),
        (jax.ShapeDtypeStruct((B, S, MLA_HEADS * MLA_QK_PAD), BF16), tok(MLA_HEADS * MLA_QK_PAD)),
        (jax.ShapeDtypeStruct((B, MLA_HEADS, MLA_V, S), BF16),
         pl.BlockSpec((1, MLA_HEADS, MLA_V, tm), lambda b, i: (b, 0, 0, i))),
        (jax.ShapeDtypeStruct((B, S, ML_W), F32), tok(ML_W)),
        (jax.ShapeDtypeStruct((B, S, ML_W), BF16), tok(ML_W)),
        (jax.ShapeDtypeStruct((B, S, ML_W), F32), tok(ML_W)),
        (jax.ShapeDtypeStruct((B, S, LANE), F32), tok(LANE)),
        (jax.ShapeDtypeStruct((B, S, FN_W), BF16), tok(FN_W)),
        (jax.ShapeDtypeStruct((B, S, FN_W), BF16), tok(FN_W)),
    ]
    consts = [g, win, qn, wuq, kvn, wukv, dft]
    return pl.pallas_call(
        functools.partial(_mixin_kernel, qscale=qscale),
        grid=(B, S // tm),
        in_specs=[tok(D), tok(LANE), tok(LANE), tok(LANE)] + [_const_spec(c.shape) for c in consts],
        out_specs=[o[1] for o in outs],
        out_shape=[o[0] for o in outs],
        compiler_params=_cparams(("parallel", "parallel")),
        name="mix_in",
    )(h3, cos_t, sin_a, sin_b, *consts)


def _attn_kernel(qt_ref, k_ref, vt_ref, o_ref, acc_ref, s_ref, *, tk):
    qt = qt_ref[0, 0]
    S = k_ref.shape[1]
    n = S // tk
    tq = qt.shape[1]
    acc_ref[...] = jnp.zeros_like(acc_ref)

    def produce(j, slot):
        ks = pl.multiple_of(j * tk, tk)
        st = _dot(k_ref[0, pl.ds(ks, tk), :], qt)
        s_ref[slot] = st
        return jnp.max(st, axis=0, keepdims=True)

    def consume(j, slot, cmax, m_old, l_old):
        ks = pl.multiple_of(j * tk, tk)
        m_new = jnp.maximum(m_old, cmax)
        p = jnp.exp2(s_ref[slot] - m_new)
        alpha = jnp.exp2(m_old - m_new)
        l_new = alpha * l_old + jnp.sum(p, axis=0, keepdims=True)
        acc_ref[...] = alpha * acc_ref[...] + _dot(vt_ref[0, 0, :, pl.ds(ks, tk)], p.astype(BF16))
        return m_new, l_new

    def body(i, carry):
        cmax0, m, l = carry
        cmax1 = produce(2 * i + 1, 1)
        m, l = consume(2 * i, 0, cmax0, m, l)
        cmax0 = produce(2 * i + 2, 0)
        m, l = consume(2 * i + 1, 1, cmax1, m, l)
        return cmax0, m, l

    init = (produce(0, 0), jnp.full((1, tq), -jnp.inf, F32), jnp.zeros((1, tq), F32))
    cmax0, m, l = lax.fori_loop(0, n // 2 - 1, body, init)
    cmax1 = produce(n - 1, 1)
    m, l = consume(n - 2, 0, cmax0, m, l)
    m, l = consume(n - 1, 1, cmax1, m, l)
    o_ref[0] = (acc_ref[...] / l).T.astype(o_ref.dtype)


def _attention(qt, k, vt, tq, tk):
    B, S, _ = k.shape
    return pl.pallas_call(
        functools.partial(_attn_kernel, tk=tk),
        grid=(B, MLA_HEADS, S // tq),
        in_specs=[pl.BlockSpec((1, 1, MLA_QK_PAD, tq), lambda b, h, i: (b, h, 0, i)),
                  pl.BlockSpec((1, S, MLA_QK_PAD), lambda b, h, i: (b, 0, h)),
                  pl.BlockSpec((1, 1, MLA_V, S), lambda b, h, i: (b, h, 0, 0))],
        out_specs=pl.BlockSpec((1, tq, MLA_V), lambda b, h, i: (b, i, h)),
        out_shape=jax.ShapeDtypeStruct((B, S, MLA_HEADS * MLA_V), BF16),
        scratch_shapes=[pltpu.VMEM((MLA_V, tq), F32), pltpu.VMEM((2, tk, tq), F32)],
        compiler_params=_cparams(("parallel", "parallel", "arbitrary")),
        name="mla_attention",
    )(qt, k, vt)


def _mlpre_kernel(cur_ref, prev_ref, next_ref, gate_ref, cw_ref, cb_ref, wqk_ref, gb_ref,
                  xc_ref, q_ref, kt_ref, gi_ref, gt_ref, ext_ref):
    i = pl.program_id(1)
    ts = cur_ref.shape[1]
    halo = SUBLANE
    pad = ML_CONV // 2
    ext_ref[0:halo, :] = jnp.where(i > 0, prev_ref[0], 0.0)
    ext_ref[halo:halo + ts, :] = cur_ref[0]
    ext_ref[halo + ts:, :] = jnp.where(i < pl.num_programs(1) - 1, next_ref[0], 0.0)
    xc = cb_ref[...] + cw_ref[0:1, :] * ext_ref[halo - pad:halo - pad + ts, :]
    for j in range(1, ML_CONV):
        o = halo - pad + j
        xc = xc + cw_ref[j:j + 1, :] * ext_ref[o:o + ts, :]
    xc = xc * jax.nn.sigmoid(xc)
    xc_ref[0] = xc
    qk = _dot(xc.astype(BF16), wqk_ref[...])
    q_ref[0] = qk[:, :ML_W].astype(BF16)
    kt_ref[0] = qk[:, ML_W:].T.astype(BF16)
    gpre = gate_ref[0] + gb_ref[...]
    lf = jnp.minimum(gpre, 0.0) - jnp.log(1.0 + jnp.exp(-jnp.abs(gpre)))
    lane = lax.broadcasted_iota(jnp.int32, gpre.shape, 1)
    gi = jnp.where(lane < 2 * ML_HEADS, gpre, lf)
    gi_ref[0] = gi
    gt_ref[0] = gi.T[0:GATE_W, :]


def _mlstm_pre(mx, gate, cw, cb, wqk, gb, ts):
    B, S, W = mx.shape
    nblk = ts // SUBLANE
    last = S // SUBLANE - 1
    tok = lambda w: pl.BlockSpec((1, ts, w), lambda b, i: (b, i, 0))
    return pl.pallas_call(
        _mlpre_kernel,
        grid=(B, S // ts),
        in_specs=[tok(W),
                  pl.BlockSpec((1, SUBLANE, W), lambda b, i: (b, jnp.maximum(i * nblk - 1, 0), 0)),
                  pl.BlockSpec((1, SUBLANE, W), lambda b, i: (b, jnp.minimum((i + 1) * nblk, last), 0)),
                  tok(LANE), _const_spec(cw.shape), _const_spec(cb.shape), _const_spec(wqk.shape),
                  _const_spec(gb.shape)],
        out_specs=[tok(W), tok(W), pl.BlockSpec((1, W, ts), lambda b, i: (b, 0, i)), tok(LANE),
                   pl.BlockSpec((1, GATE_W, ts), lambda b, i: (b, 0, i))],
        out_shape=[jax.ShapeDtypeStruct((B, S, W), F32), jax.ShapeDtypeStruct((B, S, W), BF16),
                   jax.ShapeDtypeStruct((B, W, S), BF16), jax.ShapeDtypeStruct((B, S, LANE), F32),
                   jax.ShapeDtypeStruct((B, GATE_W, S), F32)],
        scratch_shapes=[pltpu.VMEM((ts + 2 * SUBLANE, W), F32)],
        compiler_params=_cparams(("parallel", "parallel")),
        name="mlstm_pre",
    )(mx, mx, mx, gate, cw, cb, wqk, gb)


def _mlstm_direction(d, q, kt, v, gi, gt, c_ref, n_ref, m_ref, si):
    L = q.shape[0]
    H, dh = ML_HEADS, ML_DIM
    row = lax.broadcasted_iota(jnp.int32, (L, L), 0)
    col = lax.broadcasted_iota(jnp.int32, (L, L), 1)
    keep = (col <= row) if d == 0 else (col >= row)
    tri = keep.astype(F32)
    b_col = _dot_f32(tri, gi)
    b_row = _dot_f32(gt, tri.T)
    edge = L - 1 if d == 0 else 0
    lane_w = lax.broadcasted_iota(jnp.int32, (1, ML_W), 1)
    row_w = lax.broadcasted_iota(jnp.int32, (ML_W, 1), 0)

    qf = q.astype(F32)
    inter = _dot(q, c_ref[si].astype(BF16))
    sel = (lax.broadcasted_iota(jnp.int32, (ML_W, LANE), 0) // dh
           == lax.broadcasted_iota(jnp.int32, (ML_W, LANE), 1))
    nmat = jnp.where(sel, n_ref[si], 0.0).astype(BF16)
    qn = _dot(q, nmat)

    out = jnp.zeros((L, ML_W), F32)
    ws_rows, dec_lane, dec_rows = [], jnp.zeros((1, ML_W), F32), []
    m_news = []
    for hd in range(H):
        j = d * H + hd
        m_old = m_ref[si, hd:hd + 1, 0:1]
        bc = b_col[:, 2 * H + j:2 * H + j + 1]
        br = b_row[2 * H + j:2 * H + j + 1, :]
        ir = gt[j:j + 1, :]
        ic = gi[:, j:j + 1]
        dmat = jnp.where(keep, bc - br + ir, -jnp.inf)
        m_inter = bc + m_old
        m_t = jnp.maximum(m_inter, jnp.max(dmat, axis=-1, keepdims=True))
        w_intra = jnp.exp(dmat - m_t)
        w_state = jnp.exp(m_inter - m_t)
        hmask = (lane_w // dh) == hd
        qh = jnp.where(hmask, qf, 0.0).astype(BF16)
        s = _dot(qh, kt) * w_intra
        denom = jnp.sum(s, axis=-1, keepdims=True) + w_state * qn[:, hd:hd + 1]
        numer = _dot(s.astype(BF16), v) + w_state * inter
        scale = 1.0 / jnp.maximum(jnp.abs(denom), jnp.exp(-m_t))
        out = out + jnp.where(hmask, numer * scale, 0.0)
        b_edge = br[:, edge:edge + 1]
        lw = b_edge - br + ir
        m_new = jnp.maximum(b_edge + m_old, jnp.max(lw, axis=-1, keepdims=True))
        ws_rows.append(jnp.broadcast_to(jnp.exp(lw - m_new), (dh, L)))
        dec = jnp.exp(b_edge + m_old - m_new)
        dec_lane = dec_lane + jnp.where(hmask, dec, 0.0)
        dec_rows.append(jnp.broadcast_to(dec, (dh, LANE)))
        m_news.append(jnp.broadcast_to(m_new, (1, LANE)))
        del ic

    kw = kt.astype(F32) * jnp.concatenate(ws_rows, axis=0)
    blockdiag = (row_w // dh) == (lane_w // dh)
    c_new = c_ref[si] * dec_lane + _dot(kw.astype(BF16), v)
    c_ref[si] = jnp.where(blockdiag, c_new, 0.0)
    n_new = n_ref[si] * jnp.concatenate(dec_rows, axis=0) + jnp.sum(kw, axis=-1, keepdims=True)
    n_ref[si] = n_new
    m_ref[si, 0:H, :] = jnp.concatenate(m_news, axis=0)
    return out


def _mlscan_kernel(qf_ref, ktf_ref, vf_ref, gif_ref, gtf_ref, qb_ref, ktb_ref, vb_ref, gib_ref, gtb_ref,
                   hf_ref, hb_ref, c_ref, n_ref, m_ref, *, bb):
    @pl.when(pl.program_id(1) == 0)
    def _():
        c_ref[...] = jnp.zeros_like(c_ref)
        n_ref[...] = jnp.zeros_like(n_ref)
        m_ref[...] = jnp.zeros_like(m_ref)

    for b in range(bb):
        hf_ref[b] = _mlstm_direction(0, qf_ref[b], ktf_ref[b], vf_ref[b], gif_ref[b], gtf_ref[b],
                                     c_ref, n_ref, m_ref, 2 * b)
        hb_ref[b] = _mlstm_direction(1, qb_ref[b], ktb_ref[b], vb_ref[b], gib_ref[b], gtb_ref[b],
                                     c_ref, n_ref, m_ref, 2 * b + 1)


def _mlstm_scan(q, kt, v, gi, gt, L, bb):
    B, S, W = q.shape
    nc = S // L
    fw = lambda b, c: c
    bw = lambda b, c: nc - 1 - c
    def specs(cm):
        return [pl.BlockSpec((bb, L, W), lambda b, c: (b, cm(b, c), 0)),
                pl.BlockSpec((bb, W, L), lambda b, c: (b, 0, cm(b, c))),
                pl.BlockSpec((bb, L, W), lambda b, c: (b, cm(b, c), 0)),
                pl.BlockSpec((bb, L, LANE), lambda b, c: (b, cm(b, c), 0)),
                pl.BlockSpec((bb, GATE_W, L), lambda b, c: (b, 0, cm(b, c)))]
    out = jax.ShapeDtypeStruct((B, S, W), F32)
    return pl.pallas_call(
        functools.partial(_mlscan_kernel, bb=bb),
        grid=(B // bb, nc),
        in_specs=specs(fw) + specs(bw),
        out_specs=[pl.BlockSpec((bb, L, W), lambda b, c: (b, c, 0)),
                   pl.BlockSpec((bb, L, W), lambda b, c: (b, nc - 1 - c, 0))],
        out_shape=[out, out],
        scratch_shapes=[pltpu.VMEM((2 * bb, W, W), F32), pltpu.VMEM((2 * bb, W, LANE), F32),
                        pltpu.VMEM((2 * bb, SUBLANE, LANE), F32)],
        compiler_params=_cparams(("parallel", "arbitrary")),
        name="mlstm_scan",
    )(q, kt, v, gi, gt, q, kt, v, gi, gt)


def _fnet1_kernel(wr_ref, wi_ref, m1_ref, a_ref):
    n1 = wr_ref.shape[1]
    a = _dot(m1_ref[:, 0:n1], wr_ref[0]) + _dot(m1_ref[:, n1:], wi_ref[0])
    a_ref[0, 0] = a[0:n1].astype(BF16)
    a_ref[0, 1] = a[n1:].astype(BF16)


def _fnet3_kernel(a_ref, tab_ref, w_ref, b_ref, o_ref):
    for j in range(SUBLANE):
        z = jnp.concatenate([a_ref[0, 0, j], a_ref[0, 1, j]], axis=0)
        y = _dot(tab_ref[j], z)
        o_ref[0, :, j, :] = _dot(y.astype(BF16), w_ref[...]) + b_ref[...]


def _fnet(fr, fi, m1, tab3, wbd, bias, tc):
    B, S, W = fr.shape
    n1 = FN_N1
    n2 = S // n1
    cols = n2 * W
    wr = fr.reshape(B, n1, cols)
    wi = fi.reshape(B, n1, cols)
    a = pl.pallas_call(
        _fnet1_kernel,
        grid=(B, cols // tc),
        in_specs=[pl.BlockSpec((1, n1, tc), lambda b, i: (b, 0, i)),
                  pl.BlockSpec((1, n1, tc), lambda b, i: (b, 0, i)),
                  _const_spec(m1.shape)],
        out_specs=pl.BlockSpec((1, 2, n1, tc), lambda b, i: (b, 0, 0, i)),
        out_shape=jax.ShapeDtypeStruct((B, 2, n1, cols), BF16),
        compiler_params=_cparams(("parallel", "parallel")),
        name="fnet_dft1",
    )(wr, wi, m1)
    a = a.reshape(B, 2, n1, n2, W)
    y = pl.pallas_call(
        _fnet3_kernel,
        grid=(B, n1 // SUBLANE),
        in_specs=[pl.BlockSpec((1, 2, SUBLANE, n2, W), lambda b, i: (b, 0, i, 0, 0)),
                  pl.BlockSpec((SUBLANE, n2, 2 * n2), lambda b, i: (i, 0, 0)),
                  _const_spec(wbd.shape), _const_spec(bias.shape)],
        out_specs=pl.BlockSpec((1, n2, SUBLANE, W), lambda b, i: (b, 0, i, 0)),
        out_shape=jax.ShapeDtypeStruct((B, n2, n1, W), F32),
        compiler_params=_cparams(("parallel", "parallel")),
        name="fnet_dft2",
    )(a, tab3, wbd, bias)
    return y.reshape(B, S, W)


def _post_kernel(h_ref, ya_ref, hf_ref, hb_ref, xc_ref, mo_ref, yf_ref, p_ref,
                 hn_ref, sk_ref, ones_ref, wo_ref, g2_ref, wgu_ref, wd_ref,
                 gn_ref, wg_ref, wp_ref, pn_ref, fn_ref, o_ref, acc_ref, *, final):
    hm = hf_ref[...] + hb_ref[...]
    sq = hm * hm
    hi = sq.astype(BF16)
    lo = (sq - hi.astype(F32)).astype(BF16)
    ms = (_dot(hi, ones_ref[...]) + _dot(lo, ones_ref[...])) * (1.0 / ML_DIM)
    ym = (hm * lax.rsqrt(ms + EPS) * hn_ref[...] + sk_ref[...] * xc_ref[...]) * mo_ref[...]
    a_w = ya_ref.shape[1]
    m_w = a_w + ML_W
    h1 = (h_ref[...] + _dot(ya_ref[...], wo_ref[0:a_w, :])
          + _dot(ym.astype(BF16), wo_ref[a_w:m_w, :])
          + _dot(yf_ref[...].astype(BF16), wo_ref[m_w:, :]))
    h2 = h1 + _swiglu_half(h1, g2_ref, wgu_ref, wd_ref, acc_ref)
    e = _rms(_dot(p_ref[...].astype(BF16), wp_ref[...]), pn_ref[...])
    gate = jax.nn.sigmoid(_dot(_rms(h2, gn_ref[...]).astype(BF16), wg_ref[...]))
    h3 = h2 + gate * e
    if final:
        h3 = _rms(h3, fn_ref[...])
    o_ref[...] = h3


def _post(h, ya, hf, hb, xc, mo, yf, p, consts, tm, final):
    T, D = h.shape
    tok = lambda w: pl.BlockSpec((tm, w), lambda i: (i, 0))
    acts = [h, ya, hf, hb, xc, mo, yf, p]
    return pl.pallas_call(
        functools.partial(_post_kernel, final=final),
        grid=(T // tm,),
        in_specs=[tok(a.shape[1]) for a in acts] + [_const_spec(c.shape) for c in consts],
        out_specs=tok(D),
        out_shape=jax.ShapeDtypeStruct((T, D), F32),
        scratch_shapes=[pltpu.VMEM((tm, D), F32)],
        compiler_params=_cparams(("parallel",)),
        name="post",
    )(*acts, *consts)


def _block_diag(w):
    G, a, b = w.shape
    eye = jnp.eye(G, dtype=w.dtype)
    return (eye[:, None, :, None] * w[:, :, None, :]).reshape(G * a, G * b)


def _ffn_weights(w_gate, w_up, w_down, tf):
    D, FF = w_gate.shape
    nc = FF // tf
    wgu = jnp.concatenate([w_gate.reshape(D, nc, tf), w_up.reshape(D, nc, tf)], axis=-1)
    return wgu.transpose(1, 0, 2).astype(BF16), w_down.reshape(nc, tf, D).astype(BF16)


def _dft_tables(S):
    n1, n2 = FN_N1, S // FN_N1
    c = np.arange(FN_DIM)
    ang = 2.0 * np.pi * np.outer(c, c) / FN_DIM
    eye = np.eye(FN_GROUPS)
    chan = np.concatenate([np.kron(eye, np.cos(ang)), -np.kron(eye, np.sin(ang))], axis=1) / 8.0
    k1 = np.arange(n1)
    a1 = 2.0 * np.pi * np.outer(k1, k1) / n1
    m1 = np.block([[np.cos(a1), np.sin(a1)], [-np.sin(a1), np.cos(a1)]])
    k = k1[:, None, None] + n1 * np.arange(n2)[None, :, None]
    s2 = np.arange(n2)[None, None, :]
    a3 = 2.0 * np.pi * ((k * s2) % S) / S
    tab3 = np.concatenate([np.cos(a3), np.sin(a3)], axis=-1) / np.sqrt(S)
    bf = lambda t: jnp.asarray(t, dtype=F32).astype(BF16)
    return bf(chan), bf(m1), bf(tab3)


def _rope_consts():
    half = MLA_ROPE // 2
    inv = 1.0 / (ROPE_THETA ** (jnp.arange(0, MLA_ROPE, 2, dtype=F32) / MLA_ROPE))
    z = jnp.zeros((half,), F32)
    o = jnp.ones((half,), F32)
    rows = [jnp.concatenate([inv, inv, z, z]), jnp.concatenate([o, o, z, z]),
            jnp.concatenate([z, o, z, z]), jnp.concatenate([-o, z, z, z])]
    rows += [jnp.zeros((LANE,), F32)] * (SUBLANE - len(rows))
    return jnp.stack(rows)


def _tiles(B, S):
    T = B * S
    return dict(tm=min(512, T), tf=256, tmix=min(512, S), tq=min(512, S), tk=min(512, S),
                ts=min(512, S), chunk=min(128, S), bb=1, tc=min(4096, (S // FN_N1) * FN_W),
                trope=min(1024, T))


def kernel(x, p, positions, ffn1_norm, ffn1_w_gate, ffn1_w_up, ffn1_w_down, mix_norm, w_in, mla_q_norm, mla_w_uq, mla_kv_norm, mla_w_ukv, mlstm_conv_w, mlstm_conv_b, mlstm_w_q, mlstm_w_k, mlstm_i_bias, mlstm_f_bias, mlstm_head_norm, mlstm_skip, fnet_w, fnet_b, w_out, ffn2_norm, ffn2_w_gate, ffn2_w_up, ffn2_w_down, ple_gate_norm, ple_w_gate, ple_w_proj, ple_post_norm, final_norm):
    B, S, D = x.shape
    depth = p.shape[0]
    T = B * S
    t = _tiles(B, S)
    row = lambda a: a.reshape(1, -1).astype(F32)

    tab = _rope_consts()
    cos_t, sin_a, sin_b = [a.reshape(B, S, LANE) for a in
                           _rope_tables(positions.reshape(T, 1).astype(jnp.int32), tab, t["trope"])]
    dft_chan, dft_m1, dft_tab3 = _dft_tables(S)
    ones_bd = _block_diag(jnp.ones((ML_HEADS, ML_DIM, ML_DIM), BF16))
    qscale = float((MLA_NOPE + MLA_ROPE) ** -0.5 * np.log2(np.e))

    h = x.reshape(T, D)
    for i in range(depth):
        wgu1, wd1 = _ffn_weights(ffn1_w_gate[i], ffn1_w_up[i], ffn1_w_down[i], t["tf"])
        wgu2, wd2 = _ffn_weights(ffn2_w_gate[i], ffn2_w_up[i], ffn2_w_down[i], t["tf"])
        wi = w_in[i]
        o_kr = MLA_Q_LORA + MLA_KV_LORA
        o_mx = o_kr + MLA_ROPE
        o_g = o_mx + 3 * ML_W
        o_f = o_g + GATE_W
        zpad = lambda n: jnp.zeros((D, n), wi.dtype)
        win = jnp.concatenate([wi[:, :o_kr], wi[:, o_mx:o_g], wi[:, o_f:], wi[:, o_kr:o_mx],
                               zpad(LANE - MLA_ROPE), wi[:, o_g:o_f], zpad(LANE - GATE_W)],
                              axis=1).astype(BF16)
        wuq = jnp.pad(mla_w_uq[i].reshape(MLA_Q_LORA, MLA_HEADS, MLA_NOPE + MLA_ROPE),
                      ((0, 0), (0, 0), (0, MLA_QK_PAD - MLA_NOPE - MLA_ROPE)))
        wuq = wuq.reshape(MLA_Q_LORA, MLA_HEADS * MLA_QK_PAD).astype(BF16)
        wukv = mla_w_ukv[i].reshape(MLA_KV_LORA, MLA_HEADS, 2, MLA_NOPE).transpose(0, 2, 1, 3)
        wukv = wukv.reshape(MLA_KV_LORA, 2 * MLA_HEADS * MLA_NOPE).astype(BF16)
        wqk = jnp.concatenate([_block_diag(mlstm_w_q[i]) * (ML_DIM ** -0.5), _block_diag(mlstm_w_k[i])],
                              axis=1).astype(BF16)
        gbias = jnp.concatenate([mlstm_i_bias[i].reshape(-1), mlstm_f_bias[i].reshape(-1),
                                 jnp.zeros((LANE - GATE_W,), F32)]).reshape(1, LANE)
        fw_bd = _block_diag(fnet_w[i]).astype(BF16)

        h = _ffn(h, row(ffn1_norm[i]), wgu1, wd1, t["tm"])
        qt, k, vt, mx, mv, mo, gate, fr, fi = _mixin(
            h.reshape(B, S, D), cos_t, sin_a, sin_b, row(mix_norm[i]), win, row(mla_q_norm[i]), wuq,
            row(mla_kv_norm[i]), wukv, dft_chan, t["tmix"], qscale)
        y_mla = _attention(qt, k, vt, t["tq"], t["tk"])
        xc, mq, mkt, gi, gt = _mlstm_pre(mx, gate, mlstm_conv_w[i].astype(F32), row(mlstm_conv_b[i]),
                                         wqk, gbias, t["ts"])
        hf, hb = _mlstm_scan(mq, mkt, mv, gi, gt, t["chunk"], t["bb"])
        y_fnet = _fnet(fr, fi, dft_m1, dft_tab3, fw_bd, row(fnet_b[i]), t["tc"])
        consts = [row(mlstm_head_norm[i]), row(mlstm_skip[i]), ones_bd, w_out[i].astype(BF16),
                  row(ffn2_norm[i]), wgu2, wd2, row(ple_gate_norm[i]), ple_w_gate[i].astype(BF16),
                  ple_w_proj[i].astype(BF16), row(ple_post_norm[i]), row(final_norm)]
        flat = lambda a: a.reshape(T, a.shape[-1])
        h = _post(h, flat(y_mla), flat(hf), flat(hb), flat(xc), flat(mo), flat(y_fnet),
                  p[i].reshape(T, -1), consts, t["tm"], final=(i == depth - 1))
    return h.reshape(B, S, D)
```

```python
import functools

import numpy as np
import jax
import jax.numpy as jnp
from jax import lax
from jax.experimental import pallas as pl
from jax.experimental.pallas import tpu as pltpu

F32 = jnp.float32
BF16 = jnp.bfloat16

EPS = 1e-6
ROPE_THETA = 10000.0
LOG2E = float(np.log2(np.e))
LANE = 128
SUBLANE = 8
VMEM_LIMIT = 56 * 1024 * 1024

MLA_HEADS = 4
MLA_Q_LORA = 384
MLA_KV_LORA = 256
MLA_NOPE = 128
MLA_ROPE = 64
MLA_V = 128
MLA_QK_PAD = 256
ML_HEADS = 4
ML_DIM = 64
ML_W = ML_HEADS * ML_DIM
ML_CONV = 5
FN_GROUPS = 4
FN_DIM = 64
FN_W = FN_GROUPS * FN_DIM
FN_N1 = 64
GATE_W = 4 * ML_HEADS

U_CQ = 0
U_CKV = U_CQ + MLA_Q_LORA
U_MX = U_CKV + MLA_KV_LORA
U_MV = U_MX + ML_W
U_MO = U_MV + ML_W
U_FIN = U_MO + ML_W
U_KR = U_FIN + FN_W
U_G = U_KR + LANE
U_TOT = U_G + LANE


def _cparams(sem):
    return pltpu.CompilerParams(dimension_semantics=sem, vmem_limit_bytes=VMEM_LIMIT)


def _const_spec(shape):
    nd = len(shape)
    return pl.BlockSpec(shape, lambda *_: (0,) * nd, pipeline_mode=pl.Buffered(1))


def _rms(x, g):
    ms = jnp.mean(x * x, axis=-1, keepdims=True)
    return x * lax.rsqrt(ms + EPS) * g


def _dot(a, b):
    return jnp.dot(a, b, preferred_element_type=F32)


def _dot_f32(a, b):
    return jnp.dot(a, b, preferred_element_type=F32, precision=lax.Precision.HIGHEST)


def _swiglu_half(x, g_ref, wgu_ref, wd_ref, acc_ref):
    tf = wd_ref.shape[1]
    xn = _rms(x, g_ref[...]).astype(BF16)
    acc_ref[...] = jnp.zeros_like(acc_ref)

    def body(c, carry):
        gu = _dot(xn, wgu_ref[c])
        g = gu[:, :tf]
        u = gu[:, tf:]
        a = (g * jax.nn.sigmoid(g) * u).astype(BF16)
        acc_ref[...] += _dot(a, wd_ref[c])
        return carry

    lax.fori_loop(0, wgu_ref.shape[0], body, 0, unroll=True)
    return 0.5 * acc_ref[...]


def _ffn_kernel(h_ref, g_ref, wgu_ref, wd_ref, o_ref, acc_ref):
    x = h_ref[...]
    o_ref[...] = x + _swiglu_half(x, g_ref, wgu_ref, wd_ref, acc_ref)


def _ffn(h, g, wgu, wd, tm):
    T, D = h.shape
    return pl.pallas_call(
        _ffn_kernel,
        grid=(T // tm,),
        in_specs=[pl.BlockSpec((tm, D), lambda i: (i, 0)),
                  _const_spec(g.shape), _const_spec(wgu.shape), _const_spec(wd.shape)],
        out_specs=pl.BlockSpec((tm, D), lambda i: (i, 0)),
        out_shape=jax.ShapeDtypeStruct((T, D), F32),
        scratch_shapes=[pltpu.VMEM((tm, D), F32)],
        compiler_params=_cparams(("parallel",)),
        name="ffn",
    )(h, g, wgu, wd)


def _rope_kernel(pos_ref, tab_ref, cos_ref, sina_ref, sinb_ref):
    ang = pos_ref[...].astype(F32) * tab_ref[0:1, :]
    c = jnp.cos(ang)
    s = jnp.sin(ang)
    cos_ref[...] = c * tab_ref[1:2, :]
    sina_ref[...] = s * tab_ref[2:3, :]
    sinb_ref[...] = s * tab_ref[3:4, :]


def _rope_tables(pos, tab, tm):
    T = pos.shape[0]
    out = jax.ShapeDtypeStruct((T, LANE), F32)
    spec = pl.BlockSpec((tm, LANE), lambda i: (i, 0))
    return pl.pallas_call(
        _rope_kernel,
        grid=(T // tm,),
        in_specs=[pl.BlockSpec((tm, 1), lambda i: (i, 0)), _const_spec(tab.shape)],
        out_specs=[spec, spec, spec],
        out_shape=[out, out, out],
        compiler_params=_cparams(("parallel",)),
        name="rope_tables",
    )(pos, tab)


def _rope(blk, cos_t, sin_a, sin_b):
    half = MLA_ROPE // 2
    return (blk * cos_t + pltpu.roll(blk, half, 1) * sin_a
            + pltpu.roll(blk, LANE - half, 1) * sin_b)


def _mixin_kernel(h_ref, cos_ref, sina_ref, sinb_ref, g_ref, win_ref, qn_ref, wuq_ref, kvn_ref,
                  wukv_ref, dft_ref,
                  qt_ref, k_ref, vt_ref, mx_ref, mv_ref, mo_ref, gate_ref, fr_ref, fi_ref, *, qscale):
    x = h_ref[0]
    xn = _rms(x, g_ref[...]).astype(BF16)
    u = _dot(xn, win_ref[...])
    cos_t, sin_a, sin_b = cos_ref[0], sina_ref[0], sinb_ref[0]

    cq = _rms(u[:, U_CQ:U_CKV], qn_ref[...]).astype(BF16)
    q = _dot(cq, wuq_ref[...])
    for hd in range(MLA_HEADS):
        o = hd * MLA_QK_PAD
        qt_ref[0, hd, 0:MLA_NOPE, :] = (q[:, o:o + MLA_NOPE] * qscale).T.astype(BF16)
        qr = _rope(q[:, o + MLA_NOPE:o + MLA_QK_PAD], cos_t, sin_a, sin_b)
        qt_ref[0, hd, MLA_NOPE:MLA_QK_PAD, :] = (qr * qscale).T.astype(BF16)

    ckv = _rms(u[:, U_CKV:U_MX], kvn_ref[...]).astype(BF16)
    kv = _dot(ckv, wukv_ref[...])
    kr = _rope(u[:, U_KR:U_KR + LANE], cos_t, sin_a, sin_b).astype(BF16)
    for hd in range(MLA_HEADS):
        o = hd * MLA_QK_PAD
        k_ref[0, :, o:o + MLA_NOPE] = kv[:, hd * MLA_NOPE:(hd + 1) * MLA_NOPE].astype(BF16)
        k_ref[0, :, o + MLA_NOPE:o + MLA_QK_PAD] = kr
        vo = (MLA_HEADS + hd) * MLA_NOPE
        vt_ref[0, hd] = kv[:, vo:vo + MLA_V].T.astype(BF16)

    mx_ref[0] = u[:, U_MX:U_MV]
    mv_ref[0] = u[:, U_MV:U_MO].astype(BF16)
    mo_ref[0] = jax.nn.sigmoid(u[:, U_MO:U_FIN])
    gate_ref[0] = u[:, U_G:U_TOT]

    f = _dot(u[:, U_FIN:U_KR].astype(BF16), dft_ref[...])
    fr_ref[0] = f[:, :FN_W].astype(BF16)
    fi_ref[0] = f[:, FN_W:].astype(BF16)


def _mixin(h3, cos_t, sin_a, sin_b, g, win, qn, wuq, kvn, wukv, dft, tm, qscale):
    B, S, D = h3.shape
    tok = lambda w: pl.BlockSpec((1, tm, w), lambda b, i: (b, i, 0))
    outs = [
        (jax.ShapeDtypeStruct((B, MLA_HEADS, MLA_QK_PAD, S), BF16),
         pl.BlockSpec((1, MLA_HEADS, MLA_QK_PAD, tm), lambda b, i: (b, 0, 0, i))),
        (jax.ShapeDtypeStruct((B, S, MLA_HEADS * MLA_QK_PAD), BF16), tok(MLA_HEADS * MLA_QK_PAD)),
        (jax.ShapeDtypeStruct((B, MLA_HEADS, MLA_V, S), BF16),
         pl.BlockSpec((1, MLA_HEADS, MLA_V, tm), lambda b, i: (b, 0, 0, i))),
        (jax.ShapeDtypeStruct((B, S, ML_W), F32), tok(ML_W)),
        (jax.ShapeDtypeStruct((B, S, ML_W), BF16), tok(ML_W)),
        (jax.ShapeDtypeStruct((B, S, ML_W), F32), tok(ML_W)),
        (jax.ShapeDtypeStruct((B, S, LANE), F32), tok(LANE)),
        (jax.ShapeDtypeStruct((B, S, FN_W), BF16), tok(FN_W)),
        (jax.ShapeDtypeStruct((B, S, FN_W), BF16), tok(FN_W)),
    ]
    consts = [g, win, qn, wuq, kvn, wukv, dft]
    return pl.pallas_call(
        functools.partial(_mixin_kernel, qscale=qscale),
        grid=(B, S // tm),
        in_specs=[tok(D), tok(LANE), tok(LANE), tok(LANE)] + [_const_spec(c.shape) for c in consts],
        out_specs=[o[1] for o in outs],
        out_shape=[o[0] for o in outs],
        compiler_params=_cparams(("parallel", "parallel")),
        name="mix_in",
    )(h3, cos_t, sin_a, sin_b, *consts)


def _attn_kernel(qt_ref, k_ref, vt_ref, o_ref, acc_ref, s_ref, *, tk):
    qt = qt_ref[0, 0]
    S = k_ref.shape[1]
    n = S // tk
    tq = qt.shape[1]
    acc_ref[...] = jnp.zeros_like(acc_ref)

    def produce(j, slot):
        ks = pl.multiple_of(j * tk, tk)
        st = _dot(k_ref[0, pl.ds(ks, tk), :], qt)
        s_ref[slot] = st
        return jnp.max(st, axis=0, keepdims=True)

    def consume(j, slot, cmax, m_old, l_old):
        ks = pl.multiple_of(j * tk, tk)
        m_new = jnp.maximum(m_old, cmax)
        p = jnp.exp2(s_ref[slot] - m_new)
        alpha = jnp.exp2(m_old - m_new)
        l_new = alpha * l_old + jnp.sum(p, axis=0, keepdims=True)
        acc_ref[...] = alpha * acc_ref[...] + _dot(vt_ref[0, 0, :, pl.ds(ks, tk)], p.astype(BF16))
        return m_new, l_new

    def body(i, carry):
        cmax0, m, l = carry
        cmax1 = produce(2 * i + 1, 1)
        m, l = consume(2 * i, 0, cmax0, m, l)
        cmax0 = produce(2 * i + 2, 0)
        m, l = consume(2 * i + 1, 1, cmax1, m, l)
        return cmax0, m, l

    init = (produce(0, 0), jnp.full((1, tq), -jnp.inf, F32), jnp.zeros((1, tq), F32))
    cmax0, m, l = lax.fori_loop(0, n // 2 - 1, body, init)
    cmax1 = produce(n - 1, 1)
    m, l = consume(n - 2, 0, cmax0, m, l)
    m, l = consume(n - 1, 1, cmax1, m, l)
    o_ref[0] = (acc_ref[...] / l).T.astype(o_ref.dtype)


def _attention(qt, k, vt, tq, tk):
    B, S, _ = k.shape
    return pl.pallas_call(
        functools.partial(_attn_kernel, tk=tk),
        grid=(B, MLA_HEADS, S // tq),
        in_specs=[pl.BlockSpec((1, 1, MLA_QK_PAD, tq), lambda b, h, i: (b, h, 0, i)),
                  pl.BlockSpec((1, S, MLA_QK_PAD), lambda b, h, i: (b, 0, h)),
                  pl.BlockSpec((1, 1, MLA_V, S), lambda b, h, i: (b, h, 0, 0))],
        out_specs=pl.BlockSpec((1, tq, MLA_V), lambda b, h, i: (b, i, h)),
        out_shape=jax.ShapeDtypeStruct((B, S, MLA_HEADS * MLA_V), BF16),
        scratch_shapes=[pltpu.VMEM((MLA_V, tq), F32), pltpu.VMEM((2, tk, tq), F32)],
        compiler_params=_cparams(("parallel", "parallel", "arbitrary")),
        name="mla_attention",
    )(qt, k, vt)


def _mlpre_kernel(cur_ref, prev_ref, next_ref, gate_ref, cw_ref, cb_ref, wqk_ref, gb_ref,
                  xc_ref, q_ref, kt_ref, gi_ref, gt_ref, ext_ref, *, chunk):
    i = pl.program_id(1)
    ts = cur_ref.shape[1]
    halo = SUBLANE
    pad = ML_CONV // 2
    ext_ref[0:halo, :] = jnp.where(i > 0, prev_ref[0], 0.0)
    ext_ref[halo:halo + ts, :] = cur_ref[0]
    ext_ref[halo + ts:, :] = jnp.where(i < pl.num_programs(1) - 1, next_ref[0], 0.0)
    xc = cb_ref[...] + cw_ref[0:1, :] * ext_ref[halo - pad:halo - pad + ts, :]
    for j in range(1, ML_CONV):
        o = halo - pad + j
        xc = xc + cw_ref[j:j + 1, :] * ext_ref[o:o + ts, :]
    xc = xc * jax.nn.sigmoid(xc)
    xc_ref[0] = xc
    qk = _dot(xc.astype(BF16), wqk_ref[...])
    q_ref[0] = qk[:, :ML_W].astype(BF16)
    kt_ref[0] = qk[:, ML_W:].T.astype(BF16)
    gpre = gate_ref[0] + gb_ref[...]
    lf = jnp.minimum(gpre, 0.0) - jnp.log(1.0 + jnp.exp(-jnp.abs(gpre)))
    lane = lax.broadcasted_iota(jnp.int32, gpre.shape, 1)
    r = lax.broadcasted_iota(jnp.int32, (ts, ts), 0)
    c = lax.broadcasted_iota(jnp.int32, (ts, ts), 1)
    same = (r // chunk) == (c // chunk)
    cum_f = _dot_f32((same & (c <= r)).astype(F32), lf)
    cum_b = _dot_f32((same & (c >= r)).astype(F32), lf)
    gi = jnp.where(lane < 2 * ML_HEADS, gpre, jnp.where(lane < 3 * ML_HEADS, cum_f, cum_b))
    gi = gi * LOG2E
    gi_ref[0] = gi
    gt_ref[0] = gi.T[0:GATE_W, :]


def _mlstm_pre(mx, gate, cw, cb, wqk, gb, ts, chunk):
    B, S, W = mx.shape
    nblk = ts // SUBLANE
    last = S // SUBLANE - 1
    tok = lambda w: pl.BlockSpec((1, ts, w), lambda b, i: (b, i, 0))
    return pl.pallas_call(
        functools.partial(_mlpre_kernel, chunk=chunk),
        grid=(B, S // ts),
        in_specs=[tok(W),
                  pl.BlockSpec((1, SUBLANE, W), lambda b, i: (b, jnp.maximum(i * nblk - 1, 0), 0)),
                  pl.BlockSpec((1, SUBLANE, W), lambda b, i: (b, jnp.minimum((i + 1) * nblk, last), 0)),
                  tok(LANE), _const_spec(cw.shape), _const_spec(cb.shape), _const_spec(wqk.shape),
                  _const_spec(gb.shape)],
        out_specs=[tok(W), tok(W), pl.BlockSpec((1, W, ts), lambda b, i: (b, 0, i)), tok(LANE),
                   pl.BlockSpec((1, GATE_W, ts), lambda b, i: (b, 0, i))],
        out_shape=[jax.ShapeDtypeStruct((B, S, W), F32), jax.ShapeDtypeStruct((B, S, W), BF16),
                   jax.ShapeDtypeStruct((B, W, S), BF16), jax.ShapeDtypeStruct((B, S, LANE), F32),
                   jax.ShapeDtypeStruct((B, GATE_W, S), F32)],
        scratch_shapes=[pltpu.VMEM((ts + 2 * SUBLANE, W), F32)],
        compiler_params=_cparams(("parallel", "parallel")),
        name="mlstm_pre",
    )(mx, mx, mx, gate, cw, cb, wqk, gb)


def _mlstm_direction(d, q, kt, v, gi, gt, c_ref, n_ref, m_ref):
    bb = len(q)
    L = q[0].shape[0]
    H, dh = ML_HEADS, ML_DIM
    row = lax.broadcasted_iota(jnp.int32, (L, L), 0)
    col = lax.broadcasted_iota(jnp.int32, (L, L), 1)
    keep = (col <= row) if d == 0 else (col >= row)
    edge = L - 1 if d == 0 else 0
    lane_w = lax.broadcasted_iota(jnp.int32, (1, ML_W), 1)
    row_w = lax.broadcasted_iota(jnp.int32, (ML_W, 1), 0)
    hmasks = [(lane_w // dh) == hd for hd in range(H)]
    zero_b = jnp.zeros_like(q[0])
    sel = (lax.broadcasted_iota(jnp.int32, (ML_W, LANE), 0) // dh
           == lax.broadcasted_iota(jnp.int32, (ML_W, LANE), 1))
    probs_of = [(b, hd) for b in range(bb) for hd in range(H)]
    lo = 2 * H + d * H

    inter, qn = [], []
    for b in range(bb):
        si = 2 * b + d
        inter.append(_dot(q[b], c_ref[si].astype(BF16)))
        nmat = jnp.where(sel, n_ref[si], 0.0).astype(BF16)
        qn.append(_dot(q[b], nmat))

    m_old = jnp.stack([m_ref[2 * b + d, hd:hd + 1, 0:1] for b, hd in probs_of])
    bc = jnp.stack([gi[b][:, lo + hd:lo + hd + 1] for b, hd in probs_of])
    br = jnp.stack([gt[b][lo + hd:lo + hd + 1, :] for b, hd in probs_of])
    ir = jnp.stack([gt[b][d * H + hd:d * H + hd + 1, :] for b, hd in probs_of])
    qn_c = jnp.stack([qn[b][:, hd:hd + 1] for b, hd in probs_of])
    qk = jnp.stack([_dot(jnp.where(hmasks[hd], q[b], zero_b), kt[b]) for b, hd in probs_of])

    dmat = jnp.where(keep, bc + (ir - br), -jnp.inf)
    m_inter = bc + m_old
    m_t = jnp.maximum(m_inter, jnp.max(dmat, axis=-1, keepdims=True))
    w_state = jnp.exp2(m_inter - m_t)
    s = qk * jnp.exp2(dmat - m_t)
    denom = jnp.sum(s, axis=-1, keepdims=True) + w_state * qn_c
    scale = 1.0 / jnp.maximum(jnp.abs(denom), jnp.exp2(-m_t))
    wscale = w_state * scale
    probs = s.astype(BF16)

    b_edge = br[:, :, edge:edge + 1]
    lw = b_edge - br + ir
    m_new = jnp.maximum(b_edge + m_old, jnp.max(lw, axis=-1, keepdims=True))
    ws = jnp.exp2(lw - m_new)
    dec = jnp.exp2(b_edge + m_old - m_new)

    def per_head_lanes(cols, b):
        e = cols[b * H + H - 1]
        for hd in reversed(range(H - 1)):
            e = jnp.where(hmasks[hd], cols[b * H + hd], e)
        return e

    blockdiag = (row_w // dh) == (lane_w // dh)
    outs = []
    for b in range(bb):
        si = 2 * b + d
        v_heads = jnp.concatenate([jnp.where(hmasks[hd], v[b], zero_b) for hd in range(H)], axis=0)
        intra = _dot(jnp.concatenate([probs[b * H + hd] for hd in range(H)], axis=1), v_heads)
        outs.append(intra * per_head_lanes(scale, b) + inter[b] * per_head_lanes(wscale, b))

        ws_rows = jnp.concatenate([jnp.broadcast_to(ws[b * H + hd], (dh, L)) for hd in range(H)], axis=0)
        kw = kt[b].astype(F32) * ws_rows
        c_new = c_ref[si] * per_head_lanes(dec, b) + _dot(kw.astype(BF16), v[b])
        c_ref[si] = jnp.where(blockdiag, c_new, 0.0)
        dec_rows = jnp.concatenate([jnp.broadcast_to(dec[b * H + hd], (dh, LANE)) for hd in range(H)], axis=0)
        n_ref[si] = n_ref[si] * dec_rows + jnp.sum(kw, axis=-1, keepdims=True)
        m_ref[si, 0:H, :] = jnp.concatenate(
            [jnp.broadcast_to(m_new[b * H + hd], (1, LANE)) for hd in range(H)], axis=0)
    return outs


def _mlscan_kernel(qf_ref, ktf_ref, vf_ref, gif_ref, gtf_ref, qb_ref, ktb_ref, vb_ref, gib_ref, gtb_ref,
                   hf_ref, hb_ref, c_ref, n_ref, m_ref, *, bb):
    @pl.when(pl.program_id(1) == 0)
    def _():
        c_ref[...] = jnp.zeros_like(c_ref)
        n_ref[...] = jnp.zeros_like(n_ref)
        m_ref[...] = jnp.zeros_like(m_ref)

    rows = lambda ref: [ref[b] for b in range(bb)]
    out_f = _mlstm_direction(0, rows(qf_ref), rows(ktf_ref), rows(vf_ref), rows(gif_ref), rows(gtf_ref),
                             c_ref, n_ref, m_ref)
    out_b = _mlstm_direction(1, rows(qb_ref), rows(ktb_ref), rows(vb_ref), rows(gib_ref), rows(gtb_ref),
                             c_ref, n_ref, m_ref)
    for b in range(bb):
        hf_ref[b] = out_f[b]
        hb_ref[b] = out_b[b]


def _mlstm_scan(q, kt, v, gi, gt, L, bb):
    B, S, W = q.shape
    nc = S // L
    fw = lambda b, c: c
    bw = lambda b, c: nc - 1 - c
    def specs(cm):
        return [pl.BlockSpec((bb, L, W), lambda b, c: (b, cm(b, c), 0)),
                pl.BlockSpec((bb, W, L), lambda b, c: (b, 0, cm(b, c))),
                pl.BlockSpec((bb, L, W), lambda b, c: (b, cm(b, c), 0)),
                pl.BlockSpec((bb, L, LANE), lambda b, c: (b, cm(b, c), 0)),
                pl.BlockSpec((bb, GATE_W, L), lambda b, c: (b, 0, cm(b, c)))]
    out = jax.ShapeDtypeStruct((B, S, W), F32)
    return pl.pallas_call(
        functools.partial(_mlscan_kernel, bb=bb),
        grid=(B // bb, nc),
        in_specs=specs(fw) + specs(bw),
        out_specs=[pl.BlockSpec((bb, L, W), lambda b, c: (b, c, 0)),
                   pl.BlockSpec((bb, L, W), lambda b, c: (b, nc - 1 - c, 0))],
        out_shape=[out, out],
        scratch_shapes=[pltpu.VMEM((2 * bb, W, W), F32), pltpu.VMEM((2 * bb, W, LANE), F32),
                        pltpu.VMEM((2 * bb, SUBLANE, LANE), F32)],
        compiler_params=_cparams(("parallel", "arbitrary")),
        name="mlstm_scan",
    )(q, kt, v, gi, gt, q, kt, v, gi, gt)


def _fnet1_kernel(wr_ref, wi_ref, m1_ref, a_ref):
    n1 = wr_ref.shape[1]
    a = _dot(m1_ref[:, 0:n1], wr_ref[0]) + _dot(m1_ref[:, n1:], wi_ref[0])
    a_ref[0, 0] = a[0:n1].astype(BF16)
    a_ref[0, 1] = a[n1:].astype(BF16)


def _fnet3_kernel(a_ref, tab_ref, w_ref, b_ref, o_ref):
    for j in range(SUBLANE):
        z = jnp.concatenate([a_ref[0, 0, j], a_ref[0, 1, j]], axis=0)
        y = _dot(tab_ref[j], z)
        o_ref[0, :, j, :] = _dot(y.astype(BF16), w_ref[...]) + b_ref[...]


def _fnet(fr, fi, m1, tab3, wbd, bias, tc):
    B, S, W = fr.shape
    n1 = FN_N1
    n2 = S // n1
    cols = n2 * W
    wr = fr.reshape(B, n1, cols)
    wi = fi.reshape(B, n1, cols)
    a = pl.pallas_call(
        _fnet1_kernel,
        grid=(B, cols // tc),
        in_specs=[pl.BlockSpec((1, n1, tc), lambda b, i: (b, 0, i)),
                  pl.BlockSpec((1, n1, tc), lambda b, i: (b, 0, i)),
                  _const_spec(m1.shape)],
        out_specs=pl.BlockSpec((1, 2, n1, tc), lambda b, i: (b, 0, 0, i)),
        out_shape=jax.ShapeDtypeStruct((B, 2, n1, cols), BF16),
        compiler_params=_cparams(("parallel", "parallel")),
        name="fnet_dft1",
    )(wr, wi, m1)
    a = a.reshape(B, 2, n1, n2, W)
    y = pl.pallas_call(
        _fnet3_kernel,
        grid=(B, n1 // SUBLANE),
        in_specs=[pl.BlockSpec((1, 2, SUBLANE, n2, W), lambda b, i: (b, 0, i, 0, 0)),
                  pl.BlockSpec((SUBLANE, n2, 2 * n2), lambda b, i: (i, 0, 0)),
                  _const_spec(wbd.shape), _const_spec(bias.shape)],
        out_specs=pl.BlockSpec((1, n2, SUBLANE, W), lambda b, i: (b, 0, i, 0)),
        out_shape=jax.ShapeDtypeStruct((B, n2, n1, W), F32),
        compiler_params=_cparams(("parallel", "parallel")),
        name="fnet_dft2",
    )(a, tab3, wbd, bias)
    return y.reshape(B, S, W)


def _post_kernel(h_ref, ya_ref, hf_ref, hb_ref, xc_ref, mo_ref, yf_ref, p_ref,
                 hn_ref, sk_ref, ones_ref, wo_ref, g2_ref, wgu_ref, wd_ref,
                 gn_ref, wg_ref, wp_ref, pn_ref, fn_ref, o_ref, acc_ref, *, final):
    hm = hf_ref[...] + hb_ref[...]
    sq = hm * hm
    hi = sq.astype(BF16)
    lo = (sq - hi.astype(F32)).astype(BF16)
    ms = (_dot(hi, ones_ref[...]) + _dot(lo, ones_ref[...])) * (1.0 / ML_DIM)
    ym = (hm * lax.rsqrt(ms + EPS) * hn_ref[...] + sk_ref[...] * xc_ref[...]) * mo_ref[...]
    a_w = ya_ref.shape[1]
    m_w = a_w + ML_W
    h1 = (h_ref[...] + _dot(ya_ref[...], wo_ref[0:a_w, :])
          + _dot(ym.astype(BF16), wo_ref[a_w:m_w, :])
          + _dot(yf_ref[...].astype(BF16), wo_ref[m_w:, :]))
    h2 = h1 + _swiglu_half(h1, g2_ref, wgu_ref, wd_ref, acc_ref)
    e = _rms(_dot(p_ref[...].astype(BF16), wp_ref[...]), pn_ref[...])
    gate = jax.nn.sigmoid(_dot(_rms(h2, gn_ref[...]).astype(BF16), wg_ref[...]))
    h3 = h2 + gate * e
    if final:
        h3 = _rms(h3, fn_ref[...])
    o_ref[...] = h3


def _post(h, ya, hf, hb, xc, mo, yf, p, consts, tm, final):
    T, D = h.shape
    tok = lambda w: pl.BlockSpec((tm, w), lambda i: (i, 0))
    acts = [h, ya, hf, hb, xc, mo, yf, p]
    return pl.pallas_call(
        functools.partial(_post_kernel, final=final),
        grid=(T // tm,),
        in_specs=[tok(a.shape[1]) for a in acts] + [_const_spec(c.shape) for c in consts],
        out_specs=tok(D),
        out_shape=jax.ShapeDtypeStruct((T, D), F32),
        scratch_shapes=[pltpu.VMEM((tm, D), F32)],
        compiler_params=_cparams(("parallel",)),
        name="post",
    )(*acts, *consts)


def _block_diag(w):
    G, a, b = w.shape
    eye = jnp.eye(G, dtype=w.dtype)
    return (eye[:, None, :, None] * w[:, :, None, :]).reshape(G * a, G * b)


def _ffn_weights(w_gate, w_up, w_down, tf):
    D, FF = w_gate.shape
    nc = FF // tf
    wgu = jnp.concatenate([w_gate.reshape(D, nc, tf), w_up.reshape(D, nc, tf)], axis=-1)
    return wgu.transpose(1, 0, 2).astype(BF16), w_down.reshape(nc, tf, D).astype(BF16)


def _dft_tables(S):
    n1, n2 = FN_N1, S // FN_N1
    c = np.arange(FN_DIM)
    ang = 2.0 * np.pi * np.outer(c, c) / FN_DIM
    eye = np.eye(FN_GROUPS)
    chan = np.concatenate([np.kron(eye, np.cos(ang)), -np.kron(eye, np.sin(ang))], axis=1) / 8.0
    k1 = np.arange(n1)
    a1 = 2.0 * np.pi * np.outer(k1, k1) / n1
    m1 = np.block([[np.cos(a1), np.sin(a1)], [-np.sin(a1), np.cos(a1)]])
    k = k1[:, None, None] + n1 * np.arange(n2)[None, :, None]
    s2 = np.arange(n2)[None, None, :]
    a3 = 2.0 * np.pi * ((k * s2) % S) / S
    tab3 = np.concatenate([np.cos(a3), np.sin(a3)], axis=-1) / np.sqrt(S)
    bf = lambda t: jnp.asarray(t, dtype=F32).astype(BF16)
    return bf(chan), bf(m1), bf(tab3)


def _rope_consts():
    half = MLA_ROPE // 2
    inv = 1.0 / (ROPE_THETA ** (jnp.arange(0, MLA_ROPE, 2, dtype=F32) / MLA_ROPE))
    z = jnp.zeros((half,), F32)
    o = jnp.ones((half,), F32)
    rows = [jnp.concatenate([inv, inv, z, z]), jnp.concatenate([o, o, z, z]),
            jnp.concatenate([z, o, z, z]), jnp.concatenate([-o, z, z, z])]
    rows += [jnp.zeros((LANE,), F32)] * (SUBLANE - len(rows))
    return jnp.stack(rows)


def _tiles(B, S):
    T = B * S
    return dict(tm=min(512, T), tf=256, tmix=min(512, S), tq=min(2048, S), tk=min(512, S // 2),
                ts=min(512, S), chunk=min(256, S), bb=2 if B % 2 == 0 else 1,
                tc=min(4096, (S // FN_N1) * FN_W),
                trope=min(1024, T))


def kernel(x, p, positions, ffn1_norm, ffn1_w_gate, ffn1_w_up, ffn1_w_down, mix_norm, w_in, mla_q_norm, mla_w_uq, mla_kv_norm, mla_w_ukv, mlstm_conv_w, mlstm_conv_b, mlstm_w_q, mlstm_w_k, mlstm_i_bias, mlstm_f_bias, mlstm_head_norm, mlstm_skip, fnet_w, fnet_b, w_out, ffn2_norm, ffn2_w_gate, ffn2_w_up, ffn2_w_down, ple_gate_norm, ple_w_gate, ple_w_proj, ple_post_norm, final_norm):
    B, S, D = x.shape
    depth = p.shape[0]
    T = B * S
    t = _tiles(B, S)
    row = lambda a: a.reshape(1, -1).astype(F32)

    tab = _rope_consts()
    cos_t, sin_a, sin_b = [a.reshape(B, S, LANE) for a in
                           _rope_tables(positions.reshape(T, 1).astype(jnp.int32), tab, t["trope"])]
    dft_chan, dft_m1, dft_tab3 = _dft_tables(S)
    ones_bd = _block_diag(jnp.ones((ML_HEADS, ML_DIM, ML_DIM), BF16))
    qscale = float((MLA_NOPE + MLA_ROPE) ** -0.5 * np.log2(np.e))

    h = x.reshape(T, D)
    for i in range(depth):
        wgu1, wd1 = _ffn_weights(ffn1_w_gate[i], ffn1_w_up[i], ffn1_w_down[i], t["tf"])
        wgu2, wd2 = _ffn_weights(ffn2_w_gate[i], ffn2_w_up[i], ffn2_w_down[i], t["tf"])
        wi = w_in[i]
        o_kr = MLA_Q_LORA + MLA_KV_LORA
        o_mx = o_kr + MLA_ROPE
        o_g = o_mx + 3 * ML_W
        o_f = o_g + GATE_W
        zpad = lambda n: jnp.zeros((D, n), wi.dtype)
        win = jnp.concatenate([wi[:, :o_kr], wi[:, o_mx:o_g], wi[:, o_f:], wi[:, o_kr:o_mx],
                               zpad(LANE - MLA_ROPE), wi[:, o_g:o_f], zpad(LANE - GATE_W)],
                              axis=1).astype(BF16)
        wuq = jnp.pad(mla_w_uq[i].reshape(MLA_Q_LORA, MLA_HEADS, MLA_NOPE + MLA_ROPE),
                      ((0, 0), (0, 0), (0, MLA_QK_PAD - MLA_NOPE - MLA_ROPE)))
        wuq = wuq.reshape(MLA_Q_LORA, MLA_HEADS * MLA_QK_PAD).astype(BF16)
        wukv = mla_w_ukv[i].reshape(MLA_KV_LORA, MLA_HEADS, 2, MLA_NOPE).transpose(0, 2, 1, 3)
        wukv = wukv.reshape(MLA_KV_LORA, 2 * MLA_HEADS * MLA_NOPE).astype(BF16)
        wqk = jnp.concatenate([_block_diag(mlstm_w_q[i]) * (ML_DIM ** -0.5), _block_diag(mlstm_w_k[i])],
                              axis=1).astype(BF16)
        gbias = jnp.concatenate([mlstm_i_bias[i].reshape(-1), mlstm_f_bias[i].reshape(-1),
                                 jnp.zeros((LANE - GATE_W,), F32)]).reshape(1, LANE)
        fw_bd = _block_diag(fnet_w[i]).astype(BF16)

        h = _ffn(h, row(ffn1_norm[i]), wgu1, wd1, t["tm"])
        qt, k, vt, mx, mv, mo, gate, fr, fi = _mixin(
            h.reshape(B, S, D), cos_t, sin_a, sin_b, row(mix_norm[i]), win, row(mla_q_norm[i]), wuq,
            row(mla_kv_norm[i]), wukv, dft_chan, t["tmix"], qscale)
        y_mla = _attention(qt, k, vt, t["tq"], t["tk"])
        xc, mq, mkt, gi, gt = _mlstm_pre(mx, gate, mlstm_conv_w[i].astype(F32), row(mlstm_conv_b[i]),
                                         wqk, gbias, t["ts"], t["chunk"])
        hf, hb = _mlstm_scan(mq, mkt, mv, gi, gt, t["chunk"], t["bb"])
        y_fnet = _fnet(fr, fi, dft_m1, dft_tab3, fw_bd, row(fnet_b[i]), t["tc"])
        consts = [row(mlstm_head_norm[i]), row(mlstm_skip[i]), ones_bd, w_out[i].astype(BF16),
                  row(ffn2_norm[i]), wgu2, wd2, row(ple_gate_norm[i]), ple_w_gate[i].astype(BF16),
                  ple_w_proj[i].astype(BF16), row(ple_post_norm[i]), row(final_norm)]
        flat = lambda a: a.reshape(T, a.shape[-1])
        h = _post(h, flat(y_mla), flat(hf), flat(hb), flat(xc), flat(mo), flat(y_fnet),
                  p[i].reshape(T, -1), consts, t["tm"], final=(i == depth - 1))
    return h.reshape(B, S, D)
```

```python
import functools

import numpy as np
import jax
import jax.numpy as jnp
from jax import lax
from jax.experimental import pallas as pl
from jax.experimental.pallas import tpu as pltpu

F32 = jnp.float32
BF16 = jnp.bfloat16

EPS = 1e-6
ROPE_THETA = 10000.0
LOG2E = float(np.log2(np.e))
LANE = 128
SUBLANE = 8
VMEM_LIMIT = 56 * 1024 * 1024

MLA_HEADS = 4
MLA_Q_LORA = 384
MLA_KV_LORA = 256
MLA_NOPE = 128
MLA_ROPE = 64
MLA_V = 128
MLA_QK_PAD = 256
ML_HEADS = 4
ML_DIM = 64
ML_W = ML_HEADS * ML_DIM
ML_CONV = 5
FN_GROUPS = 4
FN_DIM = 64
FN_W = FN_GROUPS * FN_DIM
FN_N1 = 64
GATE_W = 4 * ML_HEADS
ROW_GROUPS = 2

U_CQ = 0
U_CKV = U_CQ + MLA_Q_LORA
U_MX = U_CKV + MLA_KV_LORA
U_MV = U_MX + ML_W
U_MO = U_MV + ML_W
U_FIN = U_MO + ML_W
U_KR = U_FIN + FN_W
U_G = U_KR + LANE
U_TOT = U_G + LANE


def _cparams(sem):
    return pltpu.CompilerParams(dimension_semantics=sem, vmem_limit_bytes=VMEM_LIMIT)


def _const_spec(shape):
    nd = len(shape)
    return pl.BlockSpec(shape, lambda *_: (0,) * nd, pipeline_mode=pl.Buffered(1))


def _layer_spec(a, layer):
    nd = a.ndim - 1
    return pl.BlockSpec((None,) + a.shape[1:], lambda *_: (layer,) + (0,) * nd,
                        pipeline_mode=pl.Buffered(1))


def _rms(x, g):
    ms = jnp.mean(x * x, axis=-1, keepdims=True)
    return x * lax.rsqrt(ms + EPS) * g


def _dot(a, b):
    return jnp.dot(a, b, preferred_element_type=F32)


def _dot_f32(a, b):
    return jnp.dot(a, b, preferred_element_type=F32, precision=lax.Precision.HIGHEST)


def _swiglu_half(x, g_ref, wgu_ref, wd_ref, acc_ref, tf):
    xn = _rms(x, g_ref[...]).astype(BF16)
    for c in range(wd_ref.shape[0] // tf):
        gu = _dot(xn, wgu_ref[:, 2 * c * tf:2 * (c + 1) * tf])
        g = gu[:, :tf]
        u = gu[:, tf:]
        a = (g * jax.nn.sigmoid(g) * u).astype(BF16)
        part = _dot(a, wd_ref[c * tf:(c + 1) * tf, :])
        if c == 0:
            acc_ref[...] = part
        else:
            acc_ref[...] += part
    return 0.5 * acc_ref[...]


def _ffn_kernel(h_ref, g_ref, wgu_ref, wd_ref, o_ref, acc_ref, *, tf):
    x = h_ref[...]
    o_ref[...] = x + _swiglu_half(x, g_ref, wgu_ref, wd_ref, acc_ref, tf)


def _ffn(h, g, wgu, wd, layer, tm, tf):
    T, D = h.shape
    return pl.pallas_call(
        functools.partial(_ffn_kernel, tf=tf),
        grid=(T // tm,),
        in_specs=[pl.BlockSpec((tm, D), lambda i: (i, 0)),
                  _layer_spec(g, layer), _layer_spec(wgu, layer), _layer_spec(wd, layer)],
        out_specs=pl.BlockSpec((tm, D), lambda i: (i, 0)),
        out_shape=jax.ShapeDtypeStruct((T, D), F32),
        scratch_shapes=[pltpu.VMEM((tm, D), F32)],
        compiler_params=_cparams(("parallel",)),
        name="ffn",
    )(h, g, wgu, wd)


def _rope_kernel(pos_ref, tab_ref, cos_ref, sina_ref, sinb_ref):
    ang = pos_ref[...].astype(F32) * tab_ref[0:1, :]
    c = jnp.cos(ang)
    s = jnp.sin(ang)
    cos_ref[...] = c * tab_ref[1:2, :]
    sina_ref[...] = s * tab_ref[2:3, :]
    sinb_ref[...] = s * tab_ref[3:4, :]


def _rope_tables(pos, tab, tm):
    T = pos.shape[0]
    out = jax.ShapeDtypeStruct((T, LANE), F32)
    spec = pl.BlockSpec((tm, LANE), lambda i: (i, 0))
    return pl.pallas_call(
        _rope_kernel,
        grid=(T // tm,),
        in_specs=[pl.BlockSpec((tm, 1), lambda i: (i, 0)), _const_spec(tab.shape)],
        out_specs=[spec, spec, spec],
        out_shape=[out, out, out],
        compiler_params=_cparams(("parallel",)),
        name="rope_tables",
    )(pos, tab)


def _rope(blk, cos_t, sin_a, sin_b):
    half = MLA_ROPE // 2
    return (blk * cos_t + pltpu.roll(blk, half, 1) * sin_a
            + pltpu.roll(blk, LANE - half, 1) * sin_b)


def _mixin_kernel(h_ref, cos_ref, sina_ref, sinb_ref, g_ref, win_ref, qn_ref, wuq_ref, kvn_ref,
                  wukv_ref, dft_ref,
                  qt_ref, k_ref, vt_ref, mx_ref, mv_ref, mo_ref, gate_ref, fr_ref, fi_ref, *, qscale):
    tm = h_ref.shape[1]
    for r0 in range(0, tm, tm // ROW_GROUPS):
        r = slice(r0, r0 + tm // ROW_GROUPS)
        x = h_ref[0, r, :]
        xn = _rms(x, g_ref[...]).astype(BF16)
        u = _dot(xn, win_ref[...])
        cos_t, sin_a, sin_b = cos_ref[0, r, :], sina_ref[0, r, :], sinb_ref[0, r, :]

        cq = _rms(u[:, U_CQ:U_CKV], qn_ref[...]).astype(BF16)
        q = _dot(cq, wuq_ref[...])
        for hd in range(MLA_HEADS):
            o = hd * MLA_QK_PAD
            qt_ref[0, hd, 0:MLA_NOPE, r] = (q[:, o:o + MLA_NOPE] * qscale).T.astype(BF16)
            qr = _rope(q[:, o + MLA_NOPE:o + MLA_QK_PAD], cos_t, sin_a, sin_b)
            qt_ref[0, hd, MLA_NOPE:MLA_QK_PAD, r] = (qr * qscale).T.astype(BF16)

        ckv = _rms(u[:, U_CKV:U_MX], kvn_ref[...]).astype(BF16)
        kv = _dot(ckv, wukv_ref[...])
        kr = _rope(u[:, U_KR:U_KR + LANE], cos_t, sin_a, sin_b).astype(BF16)
        for hd in range(MLA_HEADS):
            o = hd * MLA_QK_PAD
            k_ref[0, r, o:o + MLA_NOPE] = kv[:, hd * MLA_NOPE:(hd + 1) * MLA_NOPE].astype(BF16)
            k_ref[0, r, o + MLA_NOPE:o + MLA_QK_PAD] = kr
            vo = (MLA_HEADS + hd) * MLA_NOPE
            vt_ref[0, hd, :, r] = kv[:, vo:vo + MLA_V].T.astype(BF16)

        mx_ref[0, r, :] = u[:, U_MX:U_MV]
        mv_ref[0, r, :] = u[:, U_MV:U_MO].astype(BF16)
        mo_ref[0, r, :] = jax.nn.sigmoid(u[:, U_MO:U_FIN])
        gate_ref[0, r, :] = u[:, U_G:U_TOT]

        f = _dot(u[:, U_FIN:U_KR].astype(BF16), dft_ref[...])
        fr_ref[0, r, :] = f[:, :FN_W].astype(BF16)
        fi_ref[0, r, :] = f[:, FN_W:].astype(BF16)


def _mixin(h3, cos_t, sin_a, sin_b, g, win, qn, wuq, kvn, wukv, dft, layer, tm, qscale):
    B, S, D = h3.shape
    tok = lambda w: pl.BlockSpec((1, tm, w), lambda b, i: (b, i, 0))
    outs = [
        (jax.ShapeDtypeStruct((B, MLA_HEADS, MLA_QK_PAD, S), BF16),
         pl.BlockSpec((1, MLA_HEADS, MLA_QK_PAD, tm), lambda b, i: (b, 0, 0, i))),
        (jax.ShapeDtypeStruct((B, S, MLA_HEADS * MLA_QK_PAD), BF16), tok(MLA_HEADS * MLA_QK_PAD)),
        (jax.ShapeDtypeStruct((B, MLA_HEADS, MLA_V, S), BF16),
         pl.BlockSpec((1, MLA_HEADS, MLA_V, tm), lambda b, i: (b, 0, 0, i))),
        (jax.ShapeDtypeStruct((B, S, ML_W), F32), tok(ML_W)),
        (jax.ShapeDtypeStruct((B, S, ML_W), BF16), tok(ML_W)),
        (jax.ShapeDtypeStruct((B, S, ML_W), F32), tok(ML_W)),
        (jax.ShapeDtypeStruct((B, S, LANE), F32), tok(LANE)),
        (jax.ShapeDtypeStruct((B, S, FN_W), BF16), tok(FN_W)),
        (jax.ShapeDtypeStruct((B, S, FN_W), BF16), tok(FN_W)),
    ]
    consts = [g, win, qn, wuq, kvn, wukv]
    return pl.pallas_call(
        functools.partial(_mixin_kernel, qscale=qscale),
        grid=(B, S // tm),
        in_specs=([tok(D), tok(LANE), tok(LANE), tok(LANE)] + [_layer_spec(c, layer) for c in consts]
                  + [_const_spec(dft.shape)]),
        out_specs=[o[1] for o in outs],
        out_shape=[o[0] for o in outs],
        compiler_params=_cparams(("parallel", "parallel")),
        name="mix_in",
    )(h3, cos_t, sin_a, sin_b, *consts, dft)


def _attn_kernel(qt_ref, k_ref, vt_ref, o_ref, acc_ref, s_ref, *, tk, unroll):
    qt = qt_ref[0, 0]
    S = k_ref.shape[1]
    n = S // tk
    tq = qt.shape[1]
    acc_ref[...] = jnp.zeros_like(acc_ref)

    def produce(j, slot):
        ks = pl.multiple_of(j * tk, tk)
        st = _dot(k_ref[0, pl.ds(ks, tk), :], qt)
        s_ref[slot] = st
        return jnp.max(st, axis=0, keepdims=True)

    def consume(j, slot, cmax, m_old, l_old):
        ks = pl.multiple_of(j * tk, tk)
        m_new = jnp.maximum(m_old, cmax)
        p = jnp.exp2(s_ref[slot] - m_new)
        alpha = jnp.exp2(m_old - m_new)
        l_new = alpha * l_old + jnp.sum(p, axis=0, keepdims=True)
        acc_ref[...] = alpha * acc_ref[...] + _dot(vt_ref[0, 0, :, pl.ds(ks, tk)], p.astype(BF16))
        return m_new, l_new

    def body(i, carry):
        cmax, m, l = carry
        for u in range(unroll):
            j = unroll * i + u
            nxt = produce(j + 1, (u + 1) % 2)
            m, l = consume(j, u % 2, cmax, m, l)
            cmax = nxt
        return cmax, m, l

    trips = (n - 1) // unroll
    carry = (produce(0, 0), jnp.full((1, tq), -jnp.inf, F32), jnp.zeros((1, tq), F32))
    cmax, m, l = lax.fori_loop(0, trips, body, carry)
    for j in range(unroll * trips, n):
        nxt = produce(j + 1, (j + 1) % 2) if j + 1 < n else None
        m, l = consume(j, j % 2, cmax, m, l)
        cmax = nxt
    o_ref[0] = (acc_ref[...] / l).T.astype(o_ref.dtype)


def _attention(qt, k, vt, tq, tk, unroll):
    B, S, _ = k.shape
    return pl.pallas_call(
        functools.partial(_attn_kernel, tk=tk, unroll=unroll),
        grid=(B, MLA_HEADS, S // tq),
        in_specs=[pl.BlockSpec((1, 1, MLA_QK_PAD, tq), lambda b, h, i: (b, h, 0, i)),
                  pl.BlockSpec((1, S, MLA_QK_PAD), lambda b, h, i: (b, 0, h)),
                  pl.BlockSpec((1, 1, MLA_V, S), lambda b, h, i: (b, h, 0, 0))],
        out_specs=pl.BlockSpec((1, tq, MLA_V), lambda b, h, i: (b, i, h)),
        out_shape=jax.ShapeDtypeStruct((B, S, MLA_HEADS * MLA_V), BF16),
        scratch_shapes=[pltpu.VMEM((MLA_V, tq), F32), pltpu.VMEM((2, tk, tq), F32)],
        compiler_params=_cparams(("parallel", "parallel", "arbitrary")),
        name="mla_attention",
    )(qt, k, vt)


def _mlpre_kernel(cur_ref, prev_ref, next_ref, gate_ref, cw_ref, cb_ref, wqk_ref, gb_ref,
                  xc_ref, q_ref, kt_ref, gi_ref, gt_ref, ext_ref, *, chunk):
    i = pl.program_id(1)
    ts = cur_ref.shape[1]
    halo = SUBLANE
    pad = ML_CONV // 2
    ext_ref[0:halo, :] = jnp.where(i > 0, prev_ref[0], 0.0)
    ext_ref[halo:halo + ts, :] = cur_ref[0]
    ext_ref[halo + ts:, :] = jnp.where(i < pl.num_programs(1) - 1, next_ref[0], 0.0)
    xc = cb_ref[...] + cw_ref[0:1, :] * ext_ref[halo - pad:halo - pad + ts, :]
    for j in range(1, ML_CONV):
        o = halo - pad + j
        xc = xc + cw_ref[j:j + 1, :] * ext_ref[o:o + ts, :]
    xc = xc * jax.nn.sigmoid(xc)
    xc_ref[0] = xc
    qk = _dot(xc.astype(BF16), wqk_ref[...])
    q_ref[0] = qk[:, :ML_W].astype(BF16)
    kt_ref[0] = qk[:, ML_W:].T.astype(BF16)
    gpre = gate_ref[0] + gb_ref[...]
    lf = jnp.minimum(gpre, 0.0) - jnp.log(1.0 + jnp.exp(-jnp.abs(gpre)))
    g_t = gpre.T[0:GATE_W, :]
    lf_t = lf.T[0:GATE_W, :]
    p1 = lf_t.astype(BF16)
    r1 = lf_t - p1.astype(F32)
    p2 = r1.astype(BF16)
    p3 = (r1 - p2.astype(F32)).astype(BF16)
    r = lax.broadcasted_iota(jnp.int32, (chunk, chunk), 0)
    c = lax.broadcasted_iota(jnp.int32, (chunk, chunk), 1)
    tri_f = (r <= c).astype(BF16)
    tri_b = (r >= c).astype(BF16)
    sub = lax.broadcasted_iota(jnp.int32, (GATE_W, chunk), 0)
    cols = []
    for k in range(ts // chunk):
        sl = slice(k * chunk, (k + 1) * chunk)
        cum_f = _dot(p1[:, sl], tri_f) + _dot(p2[:, sl], tri_f) + _dot(p3[:, sl], tri_f)
        cum_b = _dot(p1[:, sl], tri_b) + _dot(p2[:, sl], tri_b) + _dot(p3[:, sl], tri_b)
        cols.append(jnp.where(sub < 2 * ML_HEADS, g_t[:, sl], jnp.where(sub < 3 * ML_HEADS, cum_f, cum_b)))
    gt = jnp.concatenate(cols, axis=1) * LOG2E
    gt_ref[0] = gt
    gi_ref[0] = jnp.concatenate([gt, jnp.zeros((LANE - GATE_W, ts), F32)], axis=0).T


def _mlstm_pre(mx, gate, cw, cb, wqk, gb, layer, ts, chunk):
    B, S, W = mx.shape
    nblk = ts // SUBLANE
    last = S // SUBLANE - 1
    tok = lambda w: pl.BlockSpec((1, ts, w), lambda b, i: (b, i, 0))
    return pl.pallas_call(
        functools.partial(_mlpre_kernel, chunk=chunk),
        grid=(B, S // ts),
        in_specs=[tok(W),
                  pl.BlockSpec((1, SUBLANE, W), lambda b, i: (b, jnp.maximum(i * nblk - 1, 0), 0)),
                  pl.BlockSpec((1, SUBLANE, W), lambda b, i: (b, jnp.minimum((i + 1) * nblk, last), 0)),
                  tok(LANE), _layer_spec(cw, layer), _layer_spec(cb, layer), _layer_spec(wqk, layer),
                  _layer_spec(gb, layer)],
        out_specs=[tok(W), tok(W), pl.BlockSpec((1, W, ts), lambda b, i: (b, 0, i)), tok(LANE),
                   pl.BlockSpec((1, GATE_W, ts), lambda b, i: (b, 0, i))],
        out_shape=[jax.ShapeDtypeStruct((B, S, W), F32), jax.ShapeDtypeStruct((B, S, W), BF16),
                   jax.ShapeDtypeStruct((B, W, S), BF16), jax.ShapeDtypeStruct((B, S, LANE), F32),
                   jax.ShapeDtypeStruct((B, GATE_W, S), F32)],
        scratch_shapes=[pltpu.VMEM((ts + 2 * SUBLANE, W), F32)],
        compiler_params=_cparams(("parallel", "parallel")),
        name="mlstm_pre",
    )(mx, mx, mx, gate, cw, cb, wqk, gb)


def _mlstm_direction(d, q, kt, v, gi, gt, c_ref, n_ref, m_ref):
    bb = len(q)
    L = q[0].shape[0]
    H, dh = ML_HEADS, ML_DIM
    row = lax.broadcasted_iota(jnp.int32, (L, L), 0)
    col = lax.broadcasted_iota(jnp.int32, (L, L), 1)
    keep = (col <= row) if d == 0 else (col >= row)
    edge = L - 1 if d == 0 else 0
    lane_w = lax.broadcasted_iota(jnp.int32, (1, ML_W), 1)
    row_w = lax.broadcasted_iota(jnp.int32, (ML_W, 1), 0)
    hmasks = [(lane_w // dh) == hd for hd in range(H)]
    zero_b = jnp.zeros_like(q[0])
    sel = (lax.broadcasted_iota(jnp.int32, (ML_W, LANE), 0) // dh
           == lax.broadcasted_iota(jnp.int32, (ML_W, LANE), 1))
    probs_of = [(b, hd) for b in range(bb) for hd in range(H)]
    lo = 2 * H + d * H

    inter, qn = [], []
    for b in range(bb):
        si = 2 * b + d
        inter.append(_dot(q[b], c_ref[si].astype(BF16)))
        nmat = jnp.where(sel, n_ref[si], 0.0).astype(BF16)
        qn.append(_dot(q[b], nmat))

    m_old = jnp.stack([m_ref[2 * b + d, hd:hd + 1, 0:1] for b, hd in probs_of])
    bc = jnp.stack([gi[b][:, lo + hd:lo + hd + 1] for b, hd in probs_of])
    br = jnp.stack([gt[b][lo + hd:lo + hd + 1, :] for b, hd in probs_of])
    ir = jnp.stack([gt[b][d * H + hd:d * H + hd + 1, :] for b, hd in probs_of])
    qn_c = jnp.stack([qn[b][:, hd:hd + 1] for b, hd in probs_of])
    qk = jnp.stack([_dot(jnp.where(hmasks[hd], q[b], zero_b), kt[b]) for b, hd in probs_of])

    dmat = jnp.where(keep, bc + (ir - br), -jnp.inf)
    m_inter = bc + m_old
    m_t = jnp.maximum(m_inter, jnp.max(dmat, axis=-1, keepdims=True))
    w_state = jnp.exp2(m_inter - m_t)
    s = qk * jnp.exp2(dmat - m_t)
    denom = jnp.sum(s, axis=-1, keepdims=True) + w_state * qn_c
    scale = 1.0 / jnp.maximum(jnp.abs(denom), jnp.exp2(-m_t))
    wscale = w_state * scale
    probs = s.astype(BF16)

    b_edge = br[:, :, edge:edge + 1]
    lw = b_edge - br + ir
    m_new = jnp.maximum(b_edge + m_old, jnp.max(lw, axis=-1, keepdims=True))
    ws = jnp.exp2(lw - m_new)
    dec = jnp.exp2(b_edge + m_old - m_new)

    def per_head_lanes(cols, b):
        e = cols[b * H + H - 1]
        for hd in reversed(range(H - 1)):
            e = jnp.where(hmasks[hd], cols[b * H + hd], e)
        return e

    blockdiag = (row_w // dh) == (lane_w // dh)
    outs = []
    for b in range(bb):
        si = 2 * b + d
        v_heads = jnp.concatenate([jnp.where(hmasks[hd], v[b], zero_b) for hd in range(H)], axis=0)
        intra = _dot(jnp.concatenate([probs[b * H + hd] for hd in range(H)], axis=1), v_heads)
        outs.append(intra * per_head_lanes(scale, b) + inter[b] * per_head_lanes(wscale, b))

        ws_rows = jnp.concatenate([jnp.broadcast_to(ws[b * H + hd], (dh, L)) for hd in range(H)], axis=0)
        kw = kt[b].astype(F32) * ws_rows
        c_new = c_ref[si] * per_head_lanes(dec, b) + _dot(kw.astype(BF16), v[b])
        c_ref[si] = jnp.where(blockdiag, c_new, 0.0)
        dec_rows = jnp.concatenate([jnp.broadcast_to(dec[b * H + hd], (dh, LANE)) for hd in range(H)], axis=0)
        n_ref[si] = n_ref[si] * dec_rows + jnp.sum(kw, axis=-1, keepdims=True)
        m_ref[si, 0:H, :] = jnp.concatenate(
            [jnp.broadcast_to(m_new[b * H + hd], (1, LANE)) for hd in range(H)], axis=0)
    return outs


def _mlscan_kernel(qf_ref, ktf_ref, vf_ref, gif_ref, gtf_ref, qb_ref, ktb_ref, vb_ref, gib_ref, gtb_ref,
                   hf_ref, hb_ref, c_ref, n_ref, m_ref, *, bb):
    @pl.when(pl.program_id(1) == 0)
    def _():
        c_ref[...] = jnp.zeros_like(c_ref)
        n_ref[...] = jnp.zeros_like(n_ref)
        m_ref[...] = jnp.zeros_like(m_ref)

    rows = lambda ref: [ref[b] for b in range(bb)]
    out_f = _mlstm_direction(0, rows(qf_ref), rows(ktf_ref), rows(vf_ref), rows(gif_ref), rows(gtf_ref),
                             c_ref, n_ref, m_ref)
    out_b = _mlstm_direction(1, rows(qb_ref), rows(ktb_ref), rows(vb_ref), rows(gib_ref), rows(gtb_ref),
                             c_ref, n_ref, m_ref)
    for b in range(bb):
        hf_ref[b] = out_f[b]
        hb_ref[b] = out_b[b]


def _mlstm_scan(q, kt, v, gi, gt, L, bb):
    B, S, W = q.shape
    nc = S // L
    fw = lambda b, c: c
    bw = lambda b, c: nc - 1 - c
    def specs(cm):
        return [pl.BlockSpec((bb, L, W), lambda b, c: (b, cm(b, c), 0)),
                pl.BlockSpec((bb, W, L), lambda b, c: (b, 0, cm(b, c))),
                pl.BlockSpec((bb, L, W), lambda b, c: (b, cm(b, c), 0)),
                pl.BlockSpec((bb, L, LANE), lambda b, c: (b, cm(b, c), 0)),
                pl.BlockSpec((bb, GATE_W, L), lambda b, c: (b, 0, cm(b, c)))]
    out = jax.ShapeDtypeStruct((B, S, W), F32)
    return pl.pallas_call(
        functools.partial(_mlscan_kernel, bb=bb),
        grid=(B // bb, nc),
        in_specs=specs(fw) + specs(bw),
        out_specs=[pl.BlockSpec((bb, L, W), lambda b, c: (b, c, 0)),
                   pl.BlockSpec((bb, L, W), lambda b, c: (b, nc - 1 - c, 0))],
        out_shape=[out, out],
        scratch_shapes=[pltpu.VMEM((2 * bb, W, W), F32), pltpu.VMEM((2 * bb, W, LANE), F32),
                        pltpu.VMEM((2 * bb, SUBLANE, LANE), F32)],
        compiler_params=_cparams(("parallel", "arbitrary")),
        name="mlstm_scan",
    )(q, kt, v, gi, gt, q, kt, v, gi, gt)


def _fnet1_kernel(wr_ref, wi_ref, m1_ref, a_ref):
    n1 = wr_ref.shape[1]
    a = _dot(m1_ref[:, 0:n1], wr_ref[0]) + _dot(m1_ref[:, n1:], wi_ref[0])
    a_ref[0, 0] = a[0:n1].astype(BF16)
    a_ref[0, 1] = a[n1:].astype(BF16)


def _fnet3_kernel(a_ref, tab_ref, w_ref, b_ref, o_ref):
    for j in range(SUBLANE):
        z = jnp.concatenate([a_ref[0, 0, j], a_ref[0, 1, j]], axis=0)
        y = _dot(tab_ref[j], z)
        o_ref[0, :, j, :] = _dot(y.astype(BF16), w_ref[...]) + b_ref[...]


def _fnet(fr, fi, m1, tab3, wbd, bias, layer, tc):
    B, S, W = fr.shape
    n1 = FN_N1
    n2 = S // n1
    cols = n2 * W
    wr = fr.reshape(B, n1, cols)
    wi = fi.reshape(B, n1, cols)
    a = pl.pallas_call(
        _fnet1_kernel,
        grid=(B, cols // tc),
        in_specs=[pl.BlockSpec((1, n1, tc), lambda b, i: (b, 0, i)),
                  pl.BlockSpec((1, n1, tc), lambda b, i: (b, 0, i)),
                  _const_spec(m1.shape)],
        out_specs=pl.BlockSpec((1, 2, n1, tc), lambda b, i: (b, 0, 0, i)),
        out_shape=jax.ShapeDtypeStruct((B, 2, n1, cols), BF16),
        compiler_params=_cparams(("parallel", "parallel")),
        name="fnet_dft1",
    )(wr, wi, m1)
    a = a.reshape(B, 2, n1, n2, W)
    y = pl.pallas_call(
        _fnet3_kernel,
        grid=(B, n1 // SUBLANE),
        in_specs=[pl.BlockSpec((1, 2, SUBLANE, n2, W), lambda b, i: (b, 0, i, 0, 0)),
                  pl.BlockSpec((SUBLANE, n2, 2 * n2), lambda b, i: (i, 0, 0)),
                  _layer_spec(wbd, layer), _layer_spec(bias, layer)],
        out_specs=pl.BlockSpec((1, n2, SUBLANE, W), lambda b, i: (b, 0, i, 0)),
        out_shape=jax.ShapeDtypeStruct((B, n2, n1, W), F32),
        compiler_params=_cparams(("parallel", "parallel")),
        name="fnet_dft2",
    )(a, tab3, wbd, bias)
    return y.reshape(B, S, W)


def _post_kernel(h_ref, ya_ref, hf_ref, hb_ref, xc_ref, mo_ref, yf_ref, p_ref,
                 hn_ref, sk_ref, ones_ref, wo_ref, g2_ref, wgu_ref, wd_ref,
                 gn_ref, wg_ref, wp_ref, pn_ref, fn_ref, o_ref, acc_ref, *, final, tf):
    tm = h_ref.shape[0]
    a_w = ya_ref.shape[1]
    m_w = a_w + ML_W
    groups = [slice(r, r + tm // ROW_GROUPS) for r in range(0, tm, tm // ROW_GROUPS)]

    def project(r):
        hm = hf_ref[r, :] + hb_ref[r, :]
        sq = hm * hm
        hi = sq.astype(BF16)
        lo = (sq - hi.astype(F32)).astype(BF16)
        ms = (_dot(hi, ones_ref[...]) + _dot(lo, ones_ref[...])) * (1.0 / ML_DIM)
        ym = (hm * lax.rsqrt(ms + EPS) * hn_ref[...] + sk_ref[...] * xc_ref[r, :]) * mo_ref[r, :]
        return (h_ref[r, :] + _dot(ya_ref[r, :], wo_ref[0:a_w, :])
                + _dot(ym.astype(BF16), wo_ref[a_w:m_w, :])
                + _dot(yf_ref[r, :].astype(BF16), wo_ref[m_w:, :]))

    h1 = jnp.concatenate([project(r) for r in groups], axis=0)
    h2 = h1 + _swiglu_half(h1, g2_ref, wgu_ref, wd_ref, acc_ref, tf)

    for r in groups:
        h2r = h2[r, :]
        e = _rms(_dot(p_ref[r, :].astype(BF16), wp_ref[...]), pn_ref[...])
        gate = jax.nn.sigmoid(_dot(_rms(h2r, gn_ref[...]).astype(BF16), wg_ref[...]))
        h3 = h2r + gate * e
        if final:
            h3 = _rms(h3, fn_ref[...])
        o_ref[r, :] = h3


def _post(h, ya, hf, hb, xc, mo, yf, p, ones_bd, params, final_norm, layer, tm, tf, final):
    T, D = h.shape
    tok = lambda w: pl.BlockSpec((tm, w), lambda i: (i, 0))
    acts = [h, ya, hf, hb, xc, mo, yf]
    hn, sk, wo, g2, wgu, wd, gn, wg, wp, pn = params
    consts = [hn, sk, ones_bd, wo, g2, wgu, wd, gn, wg, wp, pn, final_norm]
    shared = (2, len(consts) - 1)
    specs = [_const_spec(c.shape) if j in shared else _layer_spec(c, layer) for j, c in enumerate(consts)]
    return pl.pallas_call(
        functools.partial(_post_kernel, final=final, tf=tf),
        grid=(T // tm,),
        in_specs=([tok(a.shape[1]) for a in acts]
                  + [pl.BlockSpec((None, tm, p.shape[2]), lambda i: (layer, i, 0))] + specs),
        out_specs=tok(D),
        out_shape=jax.ShapeDtypeStruct((T, D), F32),
        scratch_shapes=[pltpu.VMEM((tm, D), F32)],
        compiler_params=_cparams(("parallel",)),
        name="post",
    )(*acts, p, *consts)


def _block_diag(w):
    *lead, G, a, b = w.shape
    eye = jnp.eye(G, dtype=w.dtype)
    return (eye[:, None, :, None] * w[..., :, :, None, :]).reshape(*lead, G * a, G * b)


def _ffn_weights(w_gate, w_up, w_down, tf):
    Ld, D, FF = w_gate.shape
    nc = FF // tf
    wgu = jnp.concatenate([w_gate.astype(BF16).reshape(Ld, D, nc, tf),
                           w_up.astype(BF16).reshape(Ld, D, nc, tf)], axis=-1)
    return wgu.reshape(Ld, D, 2 * FF), w_down.astype(BF16)


def _dft_tables(S):
    n1, n2 = FN_N1, S // FN_N1
    c = np.arange(FN_DIM)
    ang = 2.0 * np.pi * np.outer(c, c) / FN_DIM
    eye = np.eye(FN_GROUPS)
    chan = np.concatenate([np.kron(eye, np.cos(ang)), -np.kron(eye, np.sin(ang))], axis=1) / 8.0
    k1 = np.arange(n1)
    a1 = 2.0 * np.pi * np.outer(k1, k1) / n1
    m1 = np.block([[np.cos(a1), np.sin(a1)], [-np.sin(a1), np.cos(a1)]])
    k = k1[:, None, None] + n1 * np.arange(n2)[None, :, None]
    s2 = np.arange(n2)[None, None, :]
    a3 = 2.0 * np.pi * ((k * s2) % S) / S
    tab3 = np.concatenate([np.cos(a3), np.sin(a3)], axis=-1) / np.sqrt(S)
    bf = lambda t: jnp.asarray(t, dtype=F32).astype(BF16)
    return bf(chan), bf(m1), bf(tab3)


def _rope_consts():
    half = MLA_ROPE // 2
    inv = 1.0 / (ROPE_THETA ** (jnp.arange(0, MLA_ROPE, 2, dtype=F32) / MLA_ROPE))
    z = jnp.zeros((half,), F32)
    o = jnp.ones((half,), F32)
    rows = [jnp.concatenate([inv, inv, z, z]), jnp.concatenate([o, o, z, z]),
            jnp.concatenate([z, o, z, z]), jnp.concatenate([-o, z, z, z])]
    rows += [jnp.zeros((LANE,), F32)] * (SUBLANE - len(rows))
    return jnp.stack(rows)


def _tiles(B, S):
    T = B * S
    return dict(tm=min(512, T), tf=256, tmix=min(512, S), tq=min(2048, S), tk=min(512, S), attn_unroll=2,
                ts=min(512, S), chunk=min(256, S), bb=2 if B % 2 == 0 else 1,
                tc=min(4096, (S // FN_N1) * FN_W),
                trope=min(1024, T))


def kernel(x, p, positions, ffn1_norm, ffn1_w_gate, ffn1_w_up, ffn1_w_down, mix_norm, w_in, mla_q_norm, mla_w_uq, mla_kv_norm, mla_w_ukv, mlstm_conv_w, mlstm_conv_b, mlstm_w_q, mlstm_w_k, mlstm_i_bias, mlstm_f_bias, mlstm_head_norm, mlstm_skip, fnet_w, fnet_b, w_out, ffn2_norm, ffn2_w_gate, ffn2_w_up, ffn2_w_down, ple_gate_norm, ple_w_gate, ple_w_proj, ple_post_norm, final_norm):
    B, S, D = x.shape
    depth = p.shape[0]
    T = B * S
    t = _tiles(B, S)
    Ld = depth
    rows = lambda a: a.reshape(Ld, 1, -1).astype(F32)

    tab = _rope_consts()
    cos_t, sin_a, sin_b = [a.reshape(B, S, LANE) for a in
                           _rope_tables(positions.reshape(T, 1).astype(jnp.int32), tab, t["trope"])]
    dft_chan, dft_m1, dft_tab3 = _dft_tables(S)
    ones_bd = _block_diag(jnp.ones((ML_HEADS, ML_DIM, ML_DIM), BF16))
    qscale = float((MLA_NOPE + MLA_ROPE) ** -0.5 * LOG2E)

    wgu1, wd1 = _ffn_weights(ffn1_w_gate, ffn1_w_up, ffn1_w_down, t["tf"])
    wgu2, wd2 = _ffn_weights(ffn2_w_gate, ffn2_w_up, ffn2_w_down, t["tf"])
    o_kr = MLA_Q_LORA + MLA_KV_LORA
    o_mx = o_kr + MLA_ROPE
    o_g = o_mx + 3 * ML_W
    o_f = o_g + GATE_W
    wi = w_in.astype(BF16)
    zpad = lambda n: jnp.zeros((Ld, D, n), BF16)
    win = jnp.concatenate([wi[..., :o_kr], wi[..., o_mx:o_g], wi[..., o_f:], wi[..., o_kr:o_mx],
                           zpad(LANE - MLA_ROPE), wi[..., o_g:o_f], zpad(LANE - GATE_W)], axis=-1)
    wuq = jnp.pad(mla_w_uq.astype(BF16).reshape(Ld, MLA_Q_LORA, MLA_HEADS, MLA_NOPE + MLA_ROPE),
                  ((0, 0), (0, 0), (0, 0), (0, MLA_QK_PAD - MLA_NOPE - MLA_ROPE)))
    wuq = wuq.reshape(Ld, MLA_Q_LORA, MLA_HEADS * MLA_QK_PAD)
    wukv = mla_w_ukv.astype(BF16).reshape(Ld, MLA_KV_LORA, MLA_HEADS, 2, MLA_NOPE)
    wukv = wukv.transpose(0, 1, 3, 2, 4).reshape(Ld, MLA_KV_LORA, 2 * MLA_HEADS * MLA_NOPE)
    wqk = jnp.concatenate([_block_diag(mlstm_w_q) * (ML_DIM ** -0.5), _block_diag(mlstm_w_k)],
                          axis=-1).astype(BF16)
    gbias = jnp.concatenate([mlstm_i_bias.reshape(Ld, -1), mlstm_f_bias.reshape(Ld, -1),
                             jnp.zeros((Ld, LANE - GATE_W), F32)], axis=-1).reshape(Ld, 1, LANE)
    fw_bd = _block_diag(fnet_w).astype(BF16)
    post_params = [rows(mlstm_head_norm), rows(mlstm_skip), w_out.astype(BF16), rows(ffn2_norm), wgu2, wd2,
                   rows(ple_gate_norm), ple_w_gate.astype(BF16), ple_w_proj.astype(BF16),
                   rows(ple_post_norm)]
    g1, gmix, gq, gkv = rows(ffn1_norm), rows(mix_norm), rows(mla_q_norm), rows(mla_kv_norm)
    conv_w, conv_b, fnet_bias = mlstm_conv_w.astype(F32), rows(mlstm_conv_b), rows(fnet_b)
    p_flat = p.reshape(Ld, T, p.shape[-1])
    fin = final_norm.reshape(1, -1).astype(F32)
    flat = lambda a: a.reshape(T, a.shape[-1])

    h = x.reshape(T, D)
    for i in range(depth):
        h = _ffn(h, g1, wgu1, wd1, i, t["tm"], t["tf"])
        qt, k, vt, mx, mv, mo, gate, fr, fi = _mixin(
            h.reshape(B, S, D), cos_t, sin_a, sin_b, gmix, win, gq, wuq, gkv, wukv, dft_chan, i,
            t["tmix"], qscale)
        y_mla = _attention(qt, k, vt, t["tq"], t["tk"], t["attn_unroll"])
        xc, mq, mkt, gi, gt = _mlstm_pre(mx, gate, conv_w, conv_b, wqk, gbias, i, t["ts"], t["chunk"])
        hf, hb = _mlstm_scan(mq, mkt, mv, gi, gt, t["chunk"], t["bb"])
        y_fnet = _fnet(fr, fi, dft_m1, dft_tab3, fw_bd, fnet_bias, i, t["tc"])
        h = _post(h, flat(y_mla), flat(hf), flat(hb), flat(xc), flat(mo), flat(y_fnet), p_flat, ones_bd,
                  post_params, fin, i, t["tm"], t["tf"], final=(i == depth - 1))
    return h.reshape(B, S, D)
```

```python
import functools

import numpy as np
import jax
import jax.numpy as jnp
from jax import lax
from jax.experimental import pallas as pl
from jax.experimental.pallas import tpu as pltpu

F32 = jnp.float32
BF16 = jnp.bfloat16

EPS = 1e-6
ROPE_THETA = 10000.0
LOG2E = float(np.log2(np.e))
LANE = 128
SUBLANE = 8
VMEM_LIMIT = 56 * 1024 * 1024

MLA_HEADS = 4
MLA_Q_LORA = 384
MLA_KV_LORA = 256
MLA_NOPE = 128
MLA_ROPE = 64
MLA_V = 128
MLA_QK_PAD = 256
ML_HEADS = 4
ML_DIM = 64
ML_W = ML_HEADS * ML_DIM
ML_CONV = 5
FN_GROUPS = 4
FN_DIM = 64
FN_W = FN_GROUPS * FN_DIM
FN_N1 = 64
GATE_W = 4 * ML_HEADS
GATE_ROWS = 6 * ML_HEADS
ROW_GROUPS = 2

U_CQ = 0
U_CKV = U_CQ + MLA_Q_LORA
U_MX = U_CKV + MLA_KV_LORA
U_MV = U_MX + ML_W
U_MO = U_MV + ML_W
U_FIN = U_MO + ML_W
U_KR = U_FIN + FN_W
U_G = U_KR + LANE
U_TOT = U_G + LANE


def _cparams(sem):
    return pltpu.CompilerParams(dimension_semantics=sem, vmem_limit_bytes=VMEM_LIMIT)


def _const_spec(shape):
    nd = len(shape)
    return pl.BlockSpec(shape, lambda *_: (0,) * nd, pipeline_mode=pl.Buffered(1))


def _layer_spec(a, layer):
    nd = a.ndim - 1
    return pl.BlockSpec((None,) + a.shape[1:], lambda *_: (layer,) + (0,) * nd,
                        pipeline_mode=pl.Buffered(1))


def _rms(x, g):
    ms = jnp.mean(x * x, axis=-1, keepdims=True)
    return x * lax.rsqrt(ms + EPS) * g


def _dot(a, b):
    return jnp.dot(a, b, preferred_element_type=F32)


def _dot_f32(a, b):
    return jnp.dot(a, b, preferred_element_type=F32, precision=lax.Precision.HIGHEST)


def _swiglu_half(x, g_ref, wg_ref, wu_ref, wd_ref, acc_ref, tf):
    xn = _rms(x, g_ref[...]).astype(BF16)
    for c in range(wd_ref.shape[0] // tf):
        g = _dot(xn, wg_ref[:, c * tf:(c + 1) * tf])
        u = _dot(xn, wu_ref[:, c * tf:(c + 1) * tf])
        a = (g * jax.nn.sigmoid(g) * u).astype(BF16)
        part = _dot(a, wd_ref[c * tf:(c + 1) * tf, :])
        if c == 0:
            acc_ref[...] = part
        else:
            acc_ref[...] += part
    return 0.5 * acc_ref[...]


def _ffn_kernel(h_ref, g_ref, wg_ref, wu_ref, wd_ref, o_ref, acc_ref, *, tf):
    x = h_ref[...]
    o_ref[...] = x + _swiglu_half(x, g_ref, wg_ref, wu_ref, wd_ref, acc_ref, tf)


def _ffn(h, g, wg, wu, wd, layer, tm, tf):
    T, D = h.shape
    return pl.pallas_call(
        functools.partial(_ffn_kernel, tf=tf),
        grid=(T // tm,),
        in_specs=[pl.BlockSpec((tm, D), lambda i: (i, 0)),
                  _layer_spec(g, layer), _layer_spec(wg, layer), _layer_spec(wu, layer),
                  _layer_spec(wd, layer)],
        out_specs=pl.BlockSpec((tm, D), lambda i: (i, 0)),
        out_shape=jax.ShapeDtypeStruct((T, D), F32),
        scratch_shapes=[pltpu.VMEM((tm, D), F32)],
        compiler_params=_cparams(("parallel",)),
        name="ffn",
    )(h, g, wg, wu, wd)


def _rope_kernel(pos_ref, tab_ref, cos_ref, sina_ref, sinb_ref):
    ang = pos_ref[...].astype(F32) * tab_ref[0:1, :]
    c = jnp.cos(ang)
    s = jnp.sin(ang)
    cos_ref[...] = c * tab_ref[1:2, :]
    sina_ref[...] = s * tab_ref[2:3, :]
    sinb_ref[...] = s * tab_ref[3:4, :]


def _rope_tables(pos, tab, tm):
    T = pos.shape[0]
    out = jax.ShapeDtypeStruct((T, LANE), F32)
    spec = pl.BlockSpec((tm, LANE), lambda i: (i, 0))
    return pl.pallas_call(
        _rope_kernel,
        grid=(T // tm,),
        in_specs=[pl.BlockSpec((tm, 1), lambda i: (i, 0)), _const_spec(tab.shape)],
        out_specs=[spec, spec, spec],
        out_shape=[out, out, out],
        compiler_params=_cparams(("parallel",)),
        name="rope_tables",
    )(pos, tab)


def _rope(blk, cos_t, sin_a, sin_b):
    half = MLA_ROPE // 2
    return (blk * cos_t + pltpu.roll(blk, half, 1) * sin_a
            + pltpu.roll(blk, LANE - half, 1) * sin_b)


def _mixin_kernel(h_ref, cos_ref, sina_ref, sinb_ref, g_ref, win_ref, qn_ref, wuq_ref, kvn_ref,
                  wukv_ref, dft_ref,
                  qt_ref, k_ref, vt_ref, mx_ref, mv_ref, mo_ref, gate_ref, fr_ref, fi_ref, *, qscale):
    tm = h_ref.shape[1]
    for r0 in range(0, tm, tm // ROW_GROUPS):
        r = slice(r0, r0 + tm // ROW_GROUPS)
        x = h_ref[0, r, :]
        xn = _rms(x, g_ref[...]).astype(BF16)
        u = _dot(xn, win_ref[...])
        cos_t, sin_a, sin_b = cos_ref[0, r, :], sina_ref[0, r, :], sinb_ref[0, r, :]

        cq = _rms(u[:, U_CQ:U_CKV], qn_ref[...]).astype(BF16)
        q = _dot(cq, wuq_ref[...])
        for hd in range(MLA_HEADS):
            o = hd * MLA_QK_PAD
            qt_ref[0, hd, 0:MLA_NOPE, r] = (q[:, o:o + MLA_NOPE] * qscale).T.astype(BF16)
            qr = _rope(q[:, o + MLA_NOPE:o + MLA_QK_PAD], cos_t, sin_a, sin_b)
            qt_ref[0, hd, MLA_NOPE:MLA_QK_PAD, r] = (qr * qscale).T.astype(BF16)

        ckv = _rms(u[:, U_CKV:U_MX], kvn_ref[...]).astype(BF16)
        kv = _dot(ckv, wukv_ref[...])
        kr = _rope(u[:, U_KR:U_KR + LANE], cos_t, sin_a, sin_b).astype(BF16)
        for hd in range(MLA_HEADS):
            o = hd * MLA_QK_PAD
            k_ref[0, r, o:o + MLA_NOPE] = kv[:, hd * MLA_NOPE:(hd + 1) * MLA_NOPE].astype(BF16)
            k_ref[0, r, o + MLA_NOPE:o + MLA_QK_PAD] = kr
            vo = (MLA_HEADS + hd) * MLA_NOPE
            vt_ref[0, hd, :, r] = kv[:, vo:vo + MLA_V].T.astype(BF16)

        mx_ref[0, r, :] = u[:, U_MX:U_MV]
        mv_ref[0, :, r] = u[:, U_MV:U_MO].T.astype(BF16)
        mo_ref[0, r, :] = jax.nn.sigmoid(u[:, U_MO:U_FIN])
        gate_ref[0, r, :] = u[:, U_G:U_TOT]

        f = _dot(u[:, U_FIN:U_KR].astype(BF16), dft_ref[...])
        fr_ref[0, r, :] = f[:, :FN_W].astype(BF16)
        fi_ref[0, r, :] = f[:, FN_W:].astype(BF16)


def _mixin(h3, cos_t, sin_a, sin_b, g, win, qn, wuq, kvn, wukv, dft, layer, tm, qscale):
    B, S, D = h3.shape
    tok = lambda w: pl.BlockSpec((1, tm, w), lambda b, i: (b, i, 0))
    outs = [
        (jax.ShapeDtypeStruct((B, MLA_HEADS, MLA_QK_PAD, S), BF16),
         pl.BlockSpec((1, MLA_HEADS, MLA_QK_PAD, tm), lambda b, i: (b, 0, 0, i))),
        (jax.ShapeDtypeStruct((B, S, MLA_HEADS * MLA_QK_PAD), BF16), tok(MLA_HEADS * MLA_QK_PAD)),
        (jax.ShapeDtypeStruct((B, MLA_HEADS, MLA_V, S), BF16),
         pl.BlockSpec((1, MLA_HEADS, MLA_V, tm), lambda b, i: (b, 0, 0, i))),
        (jax.ShapeDtypeStruct((B, S, ML_W), F32), tok(ML_W)),
        (jax.ShapeDtypeStruct((B, ML_W, S), BF16), pl.BlockSpec((1, ML_W, tm), lambda b, i: (b, 0, i))),
        (jax.ShapeDtypeStruct((B, S, ML_W), F32), tok(ML_W)),
        (jax.ShapeDtypeStruct((B, S, LANE), F32), tok(LANE)),
        (jax.ShapeDtypeStruct((B, S, FN_W), BF16), tok(FN_W)),
        (jax.ShapeDtypeStruct((B, S, FN_W), BF16), tok(FN_W)),
    ]
    consts = [g, win, qn, wuq, kvn, wukv]
    return pl.pallas_call(
        functools.partial(_mixin_kernel, qscale=qscale),
        grid=(B, S // tm),
        in_specs=([tok(D), tok(LANE), tok(LANE), tok(LANE)] + [_layer_spec(c, layer) for c in consts]
                  + [_const_spec(dft.shape)]),
        out_specs=[o[1] for o in outs],
        out_shape=[o[0] for o in outs],
        compiler_params=_cparams(("parallel", "parallel")),
        name="mix_in",
    )(h3, cos_t, sin_a, sin_b, *consts, dft)


def _attn_kernel(qt_ref, k_ref, vt_ref, o_ref, acc_ref, s_ref, *, tk, unroll):
    qt = qt_ref[0, 0]
    S = k_ref.shape[1]
    n = S // tk
    tq = qt.shape[1]
    acc_ref[...] = jnp.zeros_like(acc_ref)

    def produce(j, slot):
        ks = pl.multiple_of(j * tk, tk)
        st = _dot(k_ref[0, pl.ds(ks, tk), :], qt)
        s_ref[slot] = st
        return jnp.max(st, axis=0, keepdims=True)

    def consume(j, slot, cmax, m_old, l_old):
        ks = pl.multiple_of(j * tk, tk)
        m_new = jnp.maximum(m_old, cmax)
        p = jnp.exp2(s_ref[slot] - m_new)
        alpha = jnp.exp2(m_old - m_new)
        l_new = alpha * l_old + jnp.sum(p, axis=0, keepdims=True)
        acc_ref[...] = alpha * acc_ref[...] + _dot(vt_ref[0, 0, :, pl.ds(ks, tk)], p.astype(BF16))
        return m_new, l_new

    def body(i, carry):
        cmax, m, l = carry
        for u in range(unroll):
            j = unroll * i + u
            nxt = produce(j + 1, (u + 1) % 2)
            m, l = consume(j, u % 2, cmax, m, l)
            cmax = nxt
        return cmax, m, l

    trips = (n - 1) // unroll
    carry = (produce(0, 0), jnp.full((1, tq), -jnp.inf, F32), jnp.zeros((1, tq), F32))
    cmax, m, l = lax.fori_loop(0, trips, body, carry)
    for j in range(unroll * trips, n):
        nxt = produce(j + 1, (j + 1) % 2) if j + 1 < n else None
        m, l = consume(j, j % 2, cmax, m, l)
        cmax = nxt
    o_ref[0] = (acc_ref[...] / l).T.astype(o_ref.dtype)


def _attention(qt, k, vt, tq, tk, unroll):
    B, S, _ = k.shape
    return pl.pallas_call(
        functools.partial(_attn_kernel, tk=tk, unroll=unroll),
        grid=(B, MLA_HEADS, S // tq),
        in_specs=[pl.BlockSpec((1, 1, MLA_QK_PAD, tq), lambda b, h, i: (b, h, 0, i)),
                  pl.BlockSpec((1, S, MLA_QK_PAD), lambda b, h, i: (b, 0, h)),
                  pl.BlockSpec((1, 1, MLA_V, S), lambda b, h, i: (b, h, 0, 0))],
        out_specs=pl.BlockSpec((1, tq, MLA_V), lambda b, h, i: (b, i, h)),
        out_shape=jax.ShapeDtypeStruct((B, S, MLA_HEADS * MLA_V), BF16),
        scratch_shapes=[pltpu.VMEM((MLA_V, tq), F32), pltpu.VMEM((2, tk, tq), F32)],
        compiler_params=_cparams(("parallel", "parallel", "arbitrary")),
        name="mla_attention",
    )(qt, k, vt)


def _mlpre_kernel(cur_ref, prev_ref, next_ref, gate_ref, cw_ref, cb_ref, wqk_ref, gb_ref,
                  xc_ref, qt_ref, k_ref, gi_ref, gt_ref, ext_ref, *, chunk):
    i = pl.program_id(1)
    ts = cur_ref.shape[1]
    halo = SUBLANE
    pad = ML_CONV // 2
    ext_ref[0:halo, :] = jnp.where(i > 0, prev_ref[0], 0.0)
    ext_ref[halo:halo + ts, :] = cur_ref[0]
    ext_ref[halo + ts:, :] = jnp.where(i < pl.num_programs(1) - 1, next_ref[0], 0.0)
    xc = cb_ref[...] + cw_ref[0:1, :] * ext_ref[halo - pad:halo - pad + ts, :]
    for j in range(1, ML_CONV):
        o = halo - pad + j
        xc = xc + cw_ref[j:j + 1, :] * ext_ref[o:o + ts, :]
    xc = xc * jax.nn.sigmoid(xc)
    xc_ref[0] = xc
    qk = _dot(xc.astype(BF16), wqk_ref[...])
    qt_ref[0] = qk[:, :ML_W].T.astype(BF16)
    k_ref[0] = qk[:, ML_W:].astype(BF16)
    gpre = gate_ref[0] + gb_ref[...]
    lf = jnp.minimum(gpre, 0.0) - jnp.log(1.0 + jnp.exp(-jnp.abs(gpre)))
    g_t = gpre.T[0:GATE_W, :]
    lf_t = lf.T[0:GATE_W, :]
    p1 = lf_t.astype(BF16)
    r1 = lf_t - p1.astype(F32)
    p2 = r1.astype(BF16)
    p3 = (r1 - p2.astype(F32)).astype(BF16)
    r = lax.broadcasted_iota(jnp.int32, (chunk, chunk), 0)
    c = lax.broadcasted_iota(jnp.int32, (chunk, chunk), 1)
    tri_f = (r <= c).astype(BF16)
    tri_b = (r >= c).astype(BF16)
    sub = lax.broadcasted_iota(jnp.int32, (GATE_W, chunk), 0)
    cols = []
    for k in range(ts // chunk):
        sl = slice(k * chunk, (k + 1) * chunk)
        cum_f = _dot(p1[:, sl], tri_f) + _dot(p2[:, sl], tri_f) + _dot(p3[:, sl], tri_f)
        cum_b = _dot(p1[:, sl], tri_b) + _dot(p2[:, sl], tri_b) + _dot(p3[:, sl], tri_b)
        cols.append(jnp.where(sub < 2 * ML_HEADS, g_t[:, sl], jnp.where(sub < 3 * ML_HEADS, cum_f, cum_b)))
    gt = jnp.concatenate(cols, axis=1) * LOG2E
    gt = jnp.concatenate([gt, gt[0:2 * ML_HEADS] - gt[2 * ML_HEADS:]], axis=0)
    gt_ref[0] = gt
    gi_ref[0] = jnp.concatenate([gt, jnp.zeros((LANE - GATE_ROWS, ts), F32)], axis=0).T


def _mlstm_pre(mx, gate, cw, cb, wqk, gb, layer, ts, chunk):
    B, S, W = mx.shape
    nblk = ts // SUBLANE
    last = S // SUBLANE - 1
    tok = lambda w: pl.BlockSpec((1, ts, w), lambda b, i: (b, i, 0))
    return pl.pallas_call(
        functools.partial(_mlpre_kernel, chunk=chunk),
        grid=(B, S // ts),
        in_specs=[tok(W),
                  pl.BlockSpec((1, SUBLANE, W), lambda b, i: (b, jnp.maximum(i * nblk - 1, 0), 0)),
                  pl.BlockSpec((1, SUBLANE, W), lambda b, i: (b, jnp.minimum((i + 1) * nblk, last), 0)),
                  tok(LANE), _layer_spec(cw, layer), _layer_spec(cb, layer), _layer_spec(wqk, layer),
                  _layer_spec(gb, layer)],
        out_specs=[tok(W), pl.BlockSpec((1, W, ts), lambda b, i: (b, 0, i)), tok(W), tok(LANE),
                   pl.BlockSpec((1, GATE_ROWS, ts), lambda b, i: (b, 0, i))],
        out_shape=[jax.ShapeDtypeStruct((B, S, W), F32), jax.ShapeDtypeStruct((B, W, S), BF16),
                   jax.ShapeDtypeStruct((B, S, W), BF16), jax.ShapeDtypeStruct((B, S, LANE), F32),
                   jax.ShapeDtypeStruct((B, GATE_ROWS, S), F32)],
        scratch_shapes=[pltpu.VMEM((ts + 2 * SUBLANE, W), F32)],
        compiler_params=_cparams(("parallel", "parallel")),
        name="mlstm_pre",
    )(mx, mx, mx, gate, cw, cb, wqk, gb)


def _mlstm_chunk(blocks, c_ref, n_ref, m_ref):
    nslot = len(blocks)
    bb = nslot // 2
    L = blocks[0][1].shape[0]
    H, dh = ML_HEADS, ML_DIM
    key = lax.broadcasted_iota(jnp.int32, (L, L), 0)
    qry = lax.broadcasted_iota(jnp.int32, (L, L), 1)
    lane_w = lax.broadcasted_iota(jnp.int32, (1, ML_W), 1)
    row_w = lax.broadcasted_iota(jnp.int32, (ML_W, 1), 0)
    hlanes = [(lane_w // dh) == hd for hd in range(H)]
    hrows = [(row_w // dh) == hd for hd in range(H)]
    zero_k = jnp.zeros_like(blocks[0][1])
    zero_v = jnp.zeros_like(blocks[0][2])
    sel8 = (lax.broadcasted_iota(jnp.int32, (SUBLANE, ML_W), 0)
            == lax.broadcasted_iota(jnp.int32, (SUBLANE, ML_W), 1) // dh)
    probs_of = [(si, hd) for si in range(nslot) for hd in range(H)]
    direction = lambda si: si // bb
    gate = lambda kind, si, hd: kind * 2 * H + direction(si) * H + hd
    edge = lambda si: L - 1 if direction(si) == 0 else 0

    QT, K, VT, GI, GT = range(5)
    inter_t, qn_t = [], []
    for si, blk in enumerate(blocks):
        inter_t.append(_dot(c_ref[si].astype(BF16), blk[QT]))
        qn_t.append(_dot(n_ref[si].astype(BF16), blk[QT]))

    def rows_of(kind):
        return jnp.stack([blocks[si][GT][gate(kind, si, hd):gate(kind, si, hd) + 1, :]
                          for si, hd in probs_of])

    m_old = jnp.stack([m_ref[si, hd:hd + 1, 0:1] for si, hd in probs_of])
    bq = rows_of(1)
    imb_r = rows_of(2)
    imb_c = jnp.stack([blocks[si][GI][:, gate(2, si, hd):gate(2, si, hd) + 1]
                       for si, hd in probs_of])
    b_edge = jnp.stack([blocks[si][GT][gate(1, si, hd):gate(1, si, hd) + 1, edge(si):edge(si) + 1]
                        for si, hd in probs_of])
    qn_r = jnp.stack([qn_t[si][hd:hd + 1, :] for si, hd in probs_of])
    qk = jnp.stack([_dot(jnp.where(hlanes[hd], blocks[si][K], zero_k), blocks[si][QT])
                    for si, hd in probs_of])

    raw = imb_c + bq
    half = bb * H
    dmat = jnp.concatenate([jnp.where(key <= qry, raw[:half], -jnp.inf),
                            jnp.where(key >= qry, raw[half:], -jnp.inf)], axis=0)
    m_inter = bq + m_old
    m_t = jnp.maximum(m_inter, jnp.max(dmat, axis=1, keepdims=True))
    w_state = jnp.exp2(m_inter - m_t)
    s = qk * jnp.exp2(dmat - m_t)
    denom = jnp.sum(s, axis=1, keepdims=True) + w_state * qn_r
    scale = 1.0 / jnp.maximum(jnp.abs(denom), jnp.exp2(-m_t))
    wscale = w_state * scale
    probs = s.astype(BF16)

    lw = b_edge + imb_r
    m_new = jnp.maximum(b_edge + m_old, jnp.max(lw, axis=-1, keepdims=True))
    ws = jnp.exp2(lw - m_new)
    dec = jnp.exp2(b_edge + m_old - m_new)

    def per_head_rows(rows, si):
        return jnp.concatenate([jnp.broadcast_to(rows[si * H + hd], (dh, L)) for hd in range(H)], axis=0)

    def per_head_lanes(vals, si):
        e = vals[si * H + H - 1]
        for hd in reversed(range(H - 1)):
            e = jnp.where(hlanes[hd], vals[si * H + hd], e)
        return e

    blockdiag = (row_w // dh) == (lane_w // dh)
    outs = []
    for si, blk in enumerate(blocks):
        k, vt = blk[K], blk[VT]
        v_heads = jnp.concatenate([jnp.where(hrows[hd], vt, zero_v) for hd in range(H)], axis=1)
        intra_t = _dot(v_heads, jnp.concatenate([probs[si * H + hd] for hd in range(H)], axis=0))
        out_t = intra_t * per_head_rows(scale, si) + inter_t[si] * per_head_rows(wscale, si)
        outs.append(out_t.T)

        dec_l = per_head_lanes(dec, si)
        vw = (vt.astype(F32) * per_head_rows(ws, si)).astype(BF16)
        c_new = c_ref[si] * dec_l + _dot(vw, k)
        c_ref[si] = jnp.where(blockdiag, c_new, 0.0)
        ws8 = jnp.concatenate([ws[si * H + hd] for hd in range(H)]
                              + [jnp.zeros((SUBLANE - H, L), F32)], axis=0).astype(BF16)
        n_ref[si] = n_ref[si] * dec_l + jnp.where(sel8, _dot(ws8, k), 0.0)
        m_ref[si, 0:H, :] = jnp.concatenate(
            [jnp.broadcast_to(m_new[si * H + hd], (1, LANE)) for hd in range(H)], axis=0)
    return outs


def _mlscan_kernel(qf_ref, kf_ref, vf_ref, gif_ref, gtf_ref, qb_ref, kb_ref, vb_ref, gib_ref, gtb_ref,
                   hf_ref, hb_ref, c_ref, n_ref, m_ref, *, bb):
    @pl.when(pl.program_id(1) == 0)
    def _():
        c_ref[...] = jnp.zeros_like(c_ref)
        n_ref[...] = jnp.zeros_like(n_ref)
        m_ref[...] = jnp.zeros_like(m_ref)

    fwd = [(qf_ref[b], kf_ref[b], vf_ref[b], gif_ref[b], gtf_ref[b]) for b in range(bb)]
    bwd = [(qb_ref[b], kb_ref[b], vb_ref[b], gib_ref[b], gtb_ref[b]) for b in range(bb)]
    outs = _mlstm_chunk(fwd + bwd, c_ref, n_ref, m_ref)
    for b in range(bb):
        hf_ref[b] = outs[b]
        hb_ref[b] = outs[bb + b]


def _mlstm_scan(qt, k, vt, gi, gt, L, bb):
    B, S, W = k.shape
    nc = S // L
    fw = lambda b, c: c
    bw = lambda b, c: nc - 1 - c
    def specs(cm):
        return [pl.BlockSpec((bb, W, L), lambda b, c: (b, 0, cm(b, c))),
                pl.BlockSpec((bb, L, W), lambda b, c: (b, cm(b, c), 0)),
                pl.BlockSpec((bb, W, L), lambda b, c: (b, 0, cm(b, c))),
                pl.BlockSpec((bb, L, LANE), lambda b, c: (b, cm(b, c), 0)),
                pl.BlockSpec((bb, GATE_ROWS, L), lambda b, c: (b, 0, cm(b, c)))]
    out = jax.ShapeDtypeStruct((B, S, W), F32)
    return pl.pallas_call(
        functools.partial(_mlscan_kernel, bb=bb),
        grid=(B // bb, nc),
        in_specs=specs(fw) + specs(bw),
        out_specs=[pl.BlockSpec((bb, L, W), lambda b, c: (b, c, 0)),
                   pl.BlockSpec((bb, L, W), lambda b, c: (b, nc - 1 - c, 0))],
        out_shape=[out, out],
        scratch_shapes=[pltpu.VMEM((2 * bb, W, W), F32), pltpu.VMEM((2 * bb, SUBLANE, W), F32),
                        pltpu.VMEM((2 * bb, SUBLANE, LANE), F32)],
        compiler_params=_cparams(("parallel", "arbitrary")),
        name="mlstm_scan",
    )(qt, k, vt, gi, gt, qt, k, vt, gi, gt)


def _fnet1_kernel(wr_ref, wi_ref, m1_ref, a_ref):
    n1 = wr_ref.shape[1]
    a = _dot(m1_ref[:, 0:n1], wr_ref[0]) + _dot(m1_ref[:, n1:], wi_ref[0])
    a_ref[0, 0] = a[0:n1].astype(BF16)
    a_ref[0, 1] = a[n1:].astype(BF16)


def _fnet3_kernel(a_ref, tab_ref, w_ref, b_ref, o_ref):
    for j in range(SUBLANE):
        z = jnp.concatenate([a_ref[0, 0, j], a_ref[0, 1, j]], axis=0)
        y = _dot(tab_ref[j], z)
        o_ref[0, :, j, :] = _dot(y.astype(BF16), w_ref[...]) + b_ref[...]


def _fnet(fr, fi, m1, tab3, wbd, bias, layer, tc):
    B, S, W = fr.shape
    n1 = FN_N1
    n2 = S // n1
    cols = n2 * W
    wr = fr.reshape(B, n1, cols)
    wi = fi.reshape(B, n1, cols)
    a = pl.pallas_call(
        _fnet1_kernel,
        grid=(B, cols // tc),
        in_specs=[pl.BlockSpec((1, n1, tc), lambda b, i: (b, 0, i)),
                  pl.BlockSpec((1, n1, tc), lambda b, i: (b, 0, i)),
                  _const_spec(m1.shape)],
        out_specs=pl.BlockSpec((1, 2, n1, tc), lambda b, i: (b, 0, 0, i)),
        out_shape=jax.ShapeDtypeStruct((B, 2, n1, cols), BF16),
        compiler_params=_cparams(("parallel", "parallel")),
        name="fnet_dft1",
    )(wr, wi, m1)
    a = a.reshape(B, 2, n1, n2, W)
    y = pl.pallas_call(
        _fnet3_kernel,
        grid=(B, n1 // SUBLANE),
        in_specs=[pl.BlockSpec((1, 2, SUBLANE, n2, W), lambda b, i: (b, 0, i, 0, 0)),
                  pl.BlockSpec((SUBLANE, n2, 2 * n2), lambda b, i: (i, 0, 0)),
                  _layer_spec(wbd, layer), _layer_spec(bias, layer)],
        out_specs=pl.BlockSpec((1, n2, SUBLANE, W), lambda b, i: (b, 0, i, 0)),
        out_shape=jax.ShapeDtypeStruct((B, n2, n1, W), F32),
        compiler_params=_cparams(("parallel", "parallel")),
        name="fnet_dft2",
    )(a, tab3, wbd, bias)
    return y.reshape(B, S, W)


def _post_kernel(h_ref, ya_ref, hf_ref, hb_ref, xc_ref, mo_ref, yf_ref, p_ref,
                 hn_ref, sk_ref, ones_ref, wo_ref, g2_ref, wg2_ref, wu2_ref, wd_ref,
                 gn_ref, wg_ref, wp_ref, pn_ref, fn_ref, o_ref, acc_ref, *, final, tf):
    tm = h_ref.shape[0]
    a_w = ya_ref.shape[1]
    m_w = a_w + ML_W
    groups = [slice(r, r + tm // ROW_GROUPS) for r in range(0, tm, tm // ROW_GROUPS)]

    def project(r):
        hm = hf_ref[r, :] + hb_ref[r, :]
        sq = hm * hm
        hi = sq.astype(BF16)
        lo = (sq - hi.astype(F32)).astype(BF16)
        ms = (_dot(hi, ones_ref[...]) + _dot(lo, ones_ref[...])) * (1.0 / ML_DIM)
        ym = (hm * lax.rsqrt(ms + EPS) * hn_ref[...] + sk_ref[...] * xc_ref[r, :]) * mo_ref[r, :]
        return (h_ref[r, :] + _dot(ya_ref[r, :], wo_ref[0:a_w, :])
                + _dot(ym.astype(BF16), wo_ref[a_w:m_w, :])
                + _dot(yf_ref[r, :].astype(BF16), wo_ref[m_w:, :]))

    h1 = jnp.concatenate([project(r) for r in groups], axis=0)
    h2 = h1 + _swiglu_half(h1, g2_ref, wg2_ref, wu2_ref, wd_ref, acc_ref, tf)

    for r in groups:
        h2r = h2[r, :]
        e = _rms(_dot(p_ref[r, :].astype(BF16), wp_ref[...]), pn_ref[...])
        gate = jax.nn.sigmoid(_dot(_rms(h2r, gn_ref[...]).astype(BF16), wg_ref[...]))
        h3 = h2r + gate * e
        if final:
            h3 = _rms(h3, fn_ref[...])
        o_ref[r, :] = h3


def _post(h, ya, hf, hb, xc, mo, yf, p, ones_bd, params, final_norm, layer, tm, tf, final):
    T, D = h.shape
    tok = lambda w: pl.BlockSpec((tm, w), lambda i: (i, 0))
    acts = [h, ya, hf, hb, xc, mo, yf]
    hn, sk, wo, g2, wg2, wu2, wd, gn, wg, wp, pn = params
    consts = [hn, sk, ones_bd, wo, g2, wg2, wu2, wd, gn, wg, wp, pn, final_norm]
    shared = (2, len(consts) - 1)
    specs = [_const_spec(c.shape) if j in shared else _layer_spec(c, layer) for j, c in enumerate(consts)]
    return pl.pallas_call(
        functools.partial(_post_kernel, final=final, tf=tf),
        grid=(T // tm,),
        in_specs=([tok(a.shape[1]) for a in acts]
                  + [pl.BlockSpec((None, tm, p.shape[2]), lambda i: (layer, i, 0))] + specs),
        out_specs=tok(D),
        out_shape=jax.ShapeDtypeStruct((T, D), F32),
        scratch_shapes=[pltpu.VMEM((tm, D), F32)],
        compiler_params=_cparams(("parallel",)),
        name="post",
    )(*acts, p, *consts)


def _block_diag(w):
    *lead, G, a, b = w.shape
    eye = jnp.eye(G, dtype=w.dtype)
    return (eye[:, None, :, None] * w[..., :, :, None, :]).reshape(*lead, G * a, G * b)


def _dft_tables(S):
    n1, n2 = FN_N1, S // FN_N1
    c = np.arange(FN_DIM)
    ang = 2.0 * np.pi * np.outer(c, c) / FN_DIM
    eye = np.eye(FN_GROUPS)
    chan = np.concatenate([np.kron(eye, np.cos(ang)), -np.kron(eye, np.sin(ang))], axis=1) / 8.0
    k1 = np.arange(n1)
    a1 = 2.0 * np.pi * np.outer(k1, k1) / n1
    m1 = np.block([[np.cos(a1), np.sin(a1)], [-np.sin(a1), np.cos(a1)]])
    k = k1[:, None, None] + n1 * np.arange(n2)[None, :, None]
    s2 = np.arange(n2)[None, None, :]
    a3 = 2.0 * np.pi * ((k * s2) % S) / S
    tab3 = np.concatenate([np.cos(a3), np.sin(a3)], axis=-1) / np.sqrt(S)
    bf = lambda t: jnp.asarray(t, dtype=F32).astype(BF16)
    return bf(chan), bf(m1), bf(tab3)


def _rope_consts():
    half = MLA_ROPE // 2
    inv = 1.0 / (ROPE_THETA ** (jnp.arange(0, MLA_ROPE, 2, dtype=F32) / MLA_ROPE))
    z = jnp.zeros((half,), F32)
    o = jnp.ones((half,), F32)
    rows = [jnp.concatenate([inv, inv, z, z]), jnp.concatenate([o, o, z, z]),
            jnp.concatenate([z, o, z, z]), jnp.concatenate([-o, z, z, z])]
    rows += [jnp.zeros((LANE,), F32)] * (SUBLANE - len(rows))
    return jnp.stack(rows)


def _tiles(B, S):
    T = B * S
    return dict(tm=min(512, T), tf=256, tmix=min(512, S), tq=min(2048, S), tk=min(512, S), attn_unroll=2,
                ts=min(512, S), chunk=min(256, S), bb=2 if B % 2 == 0 else 1,
                tc=min(4096, (S // FN_N1) * FN_W),
                trope=min(1024, T))


def kernel(x, p, positions, ffn1_norm, ffn1_w_gate, ffn1_w_up, ffn1_w_down, mix_norm, w_in, mla_q_norm, mla_w_uq, mla_kv_norm, mla_w_ukv, mlstm_conv_w, mlstm_conv_b, mlstm_w_q, mlstm_w_k, mlstm_i_bias, mlstm_f_bias, mlstm_head_norm, mlstm_skip, fnet_w, fnet_b, w_out, ffn2_norm, ffn2_w_gate, ffn2_w_up, ffn2_w_down, ple_gate_norm, ple_w_gate, ple_w_proj, ple_post_norm, final_norm):
    B, S, D = x.shape
    depth = p.shape[0]
    T = B * S
    t = _tiles(B, S)
    Ld = depth
    rows = lambda a: a.reshape(Ld, 1, -1).astype(F32)

    tab = _rope_consts()
    cos_t, sin_a, sin_b = [a.reshape(B, S, LANE) for a in
                           _rope_tables(positions.reshape(T, 1).astype(jnp.int32), tab, t["trope"])]
    dft_chan, dft_m1, dft_tab3 = _dft_tables(S)
    ones_bd = _block_diag(jnp.ones((ML_HEADS, ML_DIM, ML_DIM), BF16))
    qscale = float((MLA_NOPE + MLA_ROPE) ** -0.5 * LOG2E)

    wg1, wu1, wd1 = ffn1_w_gate.astype(BF16), ffn1_w_up.astype(BF16), ffn1_w_down.astype(BF16)
    wg2, wu2, wd2 = ffn2_w_gate.astype(BF16), ffn2_w_up.astype(BF16), ffn2_w_down.astype(BF16)
    o_kr = MLA_Q_LORA + MLA_KV_LORA
    o_mx = o_kr + MLA_ROPE
    o_g = o_mx + 3 * ML_W
    o_f = o_g + GATE_W
    wi = w_in.astype(BF16)
    zpad = lambda n: jnp.zeros((Ld, D, n), BF16)
    win = jnp.concatenate([wi[..., :o_kr], wi[..., o_mx:o_g], wi[..., o_f:], wi[..., o_kr:o_mx],
                           zpad(LANE - MLA_ROPE), wi[..., o_g:o_f], zpad(LANE - GATE_W)], axis=-1)
    wuq = jnp.pad(mla_w_uq.astype(BF16).reshape(Ld, MLA_Q_LORA, MLA_HEADS, MLA_NOPE + MLA_ROPE),
                  ((0, 0), (0, 0), (0, 0), (0, MLA_QK_PAD - MLA_NOPE - MLA_ROPE)))
    wuq = wuq.reshape(Ld, MLA_Q_LORA, MLA_HEADS * MLA_QK_PAD)
    wukv = mla_w_ukv.astype(BF16).reshape(Ld, MLA_KV_LORA, MLA_HEADS, 2, MLA_NOPE)
    wukv = wukv.transpose(0, 1, 3, 2, 4).reshape(Ld, MLA_KV_LORA, 2 * MLA_HEADS * MLA_NOPE)
    wqk = jnp.concatenate([_block_diag(mlstm_w_q) * (ML_DIM ** -0.5), _block_diag(mlstm_w_k)],
                          axis=-1).astype(BF16)
    gbias = jnp.concatenate([mlstm_i_bias.reshape(Ld, -1), mlstm_f_bias.reshape(Ld, -1),
                             jnp.zeros((Ld, LANE - GATE_W), F32)], axis=-1).reshape(Ld, 1, LANE)
    fw_bd = _block_diag(fnet_w).astype(BF16)
    post_params = [rows(mlstm_head_norm), rows(mlstm_skip), w_out.astype(BF16), rows(ffn2_norm), wg2, wu2, wd2,
                   rows(ple_gate_norm), ple_w_gate.astype(BF16), ple_w_proj.astype(BF16),
                   rows(ple_post_norm)]
    g1, gmix, gq, gkv = rows(ffn1_norm), rows(mix_norm), rows(mla_q_norm), rows(mla_kv_norm)
    conv_w, conv_b, fnet_bias = mlstm_conv_w.astype(F32), rows(mlstm_conv_b), rows(fnet_b)
    p_flat = p.reshape(Ld, T, p.shape[-1])
    fin = final_norm.reshape(1, -1).astype(F32)
    flat = lambda a: a.reshape(T, a.shape[-1])

    h = x.reshape(T, D)
    for i in range(depth):
        h = _ffn(h, g1, wg1, wu1, wd1, i, t["tm"], t["tf"])
        qt, k, vt, mx, mv, mo, gate, fr, fi = _mixin(
            h.reshape(B, S, D), cos_t, sin_a, sin_b, gmix, win, gq, wuq, gkv, wukv, dft_chan, i,
            t["tmix"], qscale)
        y_mla = _attention(qt, k, vt, t["tq"], t["tk"], t["attn_unroll"])
        xc, mqt, mk, gi, gt = _mlstm_pre(mx, gate, conv_w, conv_b, wqk, gbias, i, t["ts"], t["chunk"])
        hf, hb = _mlstm_scan(mqt, mk, mv, gi, gt, t["chunk"], t["bb"])
        y_fnet = _fnet(fr, fi, dft_m1, dft_tab3, fw_bd, fnet_bias, i, t["tc"])
        h = _post(h, flat(y_mla), flat(hf), flat(hb), flat(xc), flat(mo), flat(y_fnet), p_flat, ones_bd,
                  post_params, fin, i, t["tm"], t["tf"], final=(i == depth - 1))
    return h.reshape(B, S, D)
```

```python
import functools

import numpy as np
import jax
import jax.numpy as jnp
from jax import lax
from jax.experimental import pallas as pl
from jax.experimental.pallas import tpu as pltpu

F32 = jnp.float32
BF16 = jnp.bfloat16

EPS = 1e-6
ROPE_THETA = 10000.0
LOG2E = float(np.log2(np.e))
LANE = 128
SUBLANE = 8
VMEM_LIMIT = 56 * 1024 * 1024

MLA_HEADS = 4
MLA_Q_LORA = 384
MLA_KV_LORA = 256
MLA_NOPE = 128
MLA_ROPE = 64
MLA_V = 128
MLA_QK_PAD = 256
ML_HEADS = 4
ML_DIM = 64
ML_W = ML_HEADS * ML_DIM
ML_CONV = 5
FN_GROUPS = 4
FN_DIM = 64
FN_W = FN_GROUPS * FN_DIM
FN_N1 = 64
GATE_W = 4 * ML_HEADS
GATE_ROWS = 6 * ML_HEADS
ROW_GROUPS = 2

U_CQ = 0
U_CKV = U_CQ + MLA_Q_LORA
U_MX = U_CKV + MLA_KV_LORA
U_MV = U_MX + ML_W
U_MO = U_MV + ML_W
U_FIN = U_MO + ML_W
U_KR = U_FIN + FN_W
G_OFF = MLA_ROPE
U_TOT = U_KR + LANE


def _cparams(sem):
    return pltpu.CompilerParams(dimension_semantics=sem, vmem_limit_bytes=VMEM_LIMIT)


def _const_spec(shape):
    nd = len(shape)
    return pl.BlockSpec(shape, lambda *_: (0,) * nd, pipeline_mode=pl.Buffered(1))


def _layer_spec(a, layer):
    nd = a.ndim - 1
    return pl.BlockSpec((None,) + a.shape[1:], lambda *_: (layer,) + (0,) * nd,
                        pipeline_mode=pl.Buffered(1))


def _rms(x, g):
    ms = jnp.mean(x * x, axis=-1, keepdims=True)
    return x * lax.rsqrt(ms + EPS) * g


def _dot(a, b):
    return jnp.dot(a, b, preferred_element_type=F32)


def _dot_f32(a, b):
    return jnp.dot(a, b, preferred_element_type=F32, precision=lax.Precision.HIGHEST)


def _swiglu_half(x, g_ref, wg_ref, wu_ref, wd_ref, acc_ref, tf):
    xn = _rms(x, g_ref[...]).astype(BF16)
    for c in range(wd_ref.shape[0] // tf):
        g = _dot(xn, wg_ref[:, c * tf:(c + 1) * tf])
        u = _dot(xn, wu_ref[:, c * tf:(c + 1) * tf])
        a = (g * jax.nn.sigmoid(g) * u).astype(BF16)
        part = _dot(a, wd_ref[c * tf:(c + 1) * tf, :])
        if c == 0:
            acc_ref[...] = part
        else:
            acc_ref[...] += part
    return 0.5 * acc_ref[...]


def _ffn_kernel(h_ref, g_ref, wg_ref, wu_ref, wd_ref, o_ref, acc_ref, *, tf):
    x = h_ref[...]
    o_ref[...] = x + _swiglu_half(x, g_ref, wg_ref, wu_ref, wd_ref, acc_ref, tf)


def _ffn(h, g, wg, wu, wd, layer, tm, tf):
    T, D = h.shape
    return pl.pallas_call(
        functools.partial(_ffn_kernel, tf=tf),
        grid=(T // tm,),
        in_specs=[pl.BlockSpec((tm, D), lambda i: (i, 0)),
                  _layer_spec(g, layer), _layer_spec(wg, layer), _layer_spec(wu, layer),
                  _layer_spec(wd, layer)],
        out_specs=pl.BlockSpec((tm, D), lambda i: (i, 0)),
        out_shape=jax.ShapeDtypeStruct((T, D), F32),
        scratch_shapes=[pltpu.VMEM((tm, D), F32)],
        compiler_params=_cparams(("parallel",)),
        name="ffn",
    )(h, g, wg, wu, wd)


def _rope_kernel(pos_ref, tab_ref, cos_ref, sina_ref, sinb_ref):
    ang = pos_ref[...].astype(F32) * tab_ref[0:1, :]
    c = jnp.cos(ang)
    s = jnp.sin(ang)
    cos_ref[...] = c * tab_ref[1:2, :]
    sina_ref[...] = s * tab_ref[2:3, :]
    sinb_ref[...] = s * tab_ref[3:4, :]


def _rope_tables(pos, tab, tm):
    T = pos.shape[0]
    out = jax.ShapeDtypeStruct((T, LANE), F32)
    spec = pl.BlockSpec((tm, LANE), lambda i: (i, 0))
    return pl.pallas_call(
        _rope_kernel,
        grid=(T // tm,),
        in_specs=[pl.BlockSpec((tm, 1), lambda i: (i, 0)), _const_spec(tab.shape)],
        out_specs=[spec, spec, spec],
        out_shape=[out, out, out],
        compiler_params=_cparams(("parallel",)),
        name="rope_tables",
    )(pos, tab)


def _rope(blk, cos_t, sin_a, sin_b):
    half = MLA_ROPE // 2
    return (blk * cos_t + pltpu.roll(blk, half, 1) * sin_a
            + pltpu.roll(blk, LANE - half, 1) * sin_b)


def _mixin_kernel(h_ref, cos_ref, sina_ref, sinb_ref, g_ref, win_ref, qn_ref, wuq_ref, kvn_ref,
                  wukv_ref, dft_ref,
                  qt_ref, k_ref, vt_ref, mx_ref, mv_ref, mo_ref, gate_ref, fr_ref, fi_ref, *, qscale):
    tm = h_ref.shape[1]
    for r0 in range(0, tm, tm // ROW_GROUPS):
        r = slice(r0, r0 + tm // ROW_GROUPS)
        x = h_ref[0, r, :]
        xn = _rms(x, g_ref[...]).astype(BF16)
        u = _dot(xn, win_ref[...])
        cos_t, sin_a, sin_b = cos_ref[0, r, :], sina_ref[0, r, :], sinb_ref[0, r, :]

        cq = _rms(u[:, U_CQ:U_CKV], qn_ref[...]).astype(BF16)
        q = _dot(cq, wuq_ref[...])
        for hd in range(MLA_HEADS):
            o = hd * MLA_QK_PAD
            qt_ref[0, hd, 0:MLA_NOPE, r] = (q[:, o:o + MLA_NOPE] * qscale).T.astype(BF16)
            qr = _rope(q[:, o + MLA_NOPE:o + MLA_QK_PAD], cos_t, sin_a, sin_b)
            qt_ref[0, hd, MLA_NOPE:MLA_QK_PAD, r] = (qr * qscale).T.astype(BF16)

        ckv = _rms(u[:, U_CKV:U_MX], kvn_ref[...]).astype(BF16)
        kv = _dot(ckv, wukv_ref[...])
        kr = _rope(u[:, U_KR:U_KR + LANE], cos_t, sin_a, sin_b).astype(BF16)
        for hd in range(MLA_HEADS):
            o = hd * MLA_QK_PAD
            k_ref[0, r, o:o + MLA_NOPE] = kv[:, hd * MLA_NOPE:(hd + 1) * MLA_NOPE].astype(BF16)
            k_ref[0, r, o + MLA_NOPE:o + MLA_QK_PAD] = kr
            vo = (MLA_HEADS + hd) * MLA_NOPE
            vt_ref[0, hd, :, r] = kv[:, vo:vo + MLA_V].T.astype(BF16)

        mx_ref[0, r, :] = u[:, U_MX:U_MV]
        mv_ref[0, :, r] = u[:, U_MV:U_MO].T.astype(BF16)
        mo_ref[0, r, :] = jax.nn.sigmoid(u[:, U_MO:U_FIN])
        gate_ref[0, r, :] = u[:, U_KR:U_TOT]

        f = _dot(u[:, U_FIN:U_KR].astype(BF16), dft_ref[...])
        fr_ref[0, r, :] = f[:, :FN_W].astype(BF16)
        fi_ref[0, r, :] = f[:, FN_W:].astype(BF16)


def _mixin(h3, cos_t, sin_a, sin_b, g, win, qn, wuq, kvn, wukv, dft, layer, tm, qscale):
    B, S, D = h3.shape
    tok = lambda w: pl.BlockSpec((1, tm, w), lambda b, i: (b, i, 0))
    outs = [
        (jax.ShapeDtypeStruct((B, MLA_HEADS, MLA_QK_PAD, S), BF16),
         pl.BlockSpec((1, MLA_HEADS, MLA_QK_PAD, tm), lambda b, i: (b, 0, 0, i))),
        (jax.ShapeDtypeStruct((B, S, MLA_HEADS * MLA_QK_PAD), BF16), tok(MLA_HEADS * MLA_QK_PAD)),
        (jax.ShapeDtypeStruct((B, MLA_HEADS, MLA_V, S), BF16),
         pl.BlockSpec((1, MLA_HEADS, MLA_V, tm), lambda b, i: (b, 0, 0, i))),
        (jax.ShapeDtypeStruct((B, S, ML_W), F32), tok(ML_W)),
        (jax.ShapeDtypeStruct((B, ML_W, S), BF16), pl.BlockSpec((1, ML_W, tm), lambda b, i: (b, 0, i))),
        (jax.ShapeDtypeStruct((B, S, ML_W), F32), tok(ML_W)),
        (jax.ShapeDtypeStruct((B, S, LANE), F32), tok(LANE)),
        (jax.ShapeDtypeStruct((B, S, FN_W), BF16), tok(FN_W)),
        (jax.ShapeDtypeStruct((B, S, FN_W), BF16), tok(FN_W)),
    ]
    consts = [g, win, qn, wuq, kvn, wukv]
    return pl.pallas_call(
        functools.partial(_mixin_kernel, qscale=qscale),
        grid=(B, S // tm),
        in_specs=([tok(D), tok(LANE), tok(LANE), tok(LANE)] + [_layer_spec(c, layer) for c in consts]
                  + [_const_spec(dft.shape)]),
        out_specs=[o[1] for o in outs],
        out_shape=[o[0] for o in outs],
        compiler_params=_cparams(("parallel", "parallel")),
        name="mix_in",
    )(h3, cos_t, sin_a, sin_b, *consts, dft)


def _attn_kernel(qt_ref, k_ref, vt_ref, o_ref, acc_ref, s_ref, *, tk, unroll):
    qt = qt_ref[0, 0]
    S = k_ref.shape[1]
    n = S // tk
    tq = qt.shape[1]
    acc_ref[...] = jnp.zeros_like(acc_ref)

    def produce(j, slot):
        ks = pl.multiple_of(j * tk, tk)
        st = _dot(k_ref[0, pl.ds(ks, tk), :], qt)
        s_ref[slot] = st
        return jnp.max(st, axis=0, keepdims=True)

    def consume(j, slot, cmax, m_old, l_old):
        ks = pl.multiple_of(j * tk, tk)
        m_new = jnp.maximum(m_old, cmax)
        p = jnp.exp2(s_ref[slot] - m_new)
        alpha = jnp.exp2(m_old - m_new)
        l_new = alpha * l_old + jnp.sum(p, axis=0, keepdims=True)
        acc_ref[...] = alpha * acc_ref[...] + _dot(vt_ref[0, 0, :, pl.ds(ks, tk)], p.astype(BF16))
        return m_new, l_new

    def body(i, carry):
        cmax, m, l = carry
        for u in range(unroll):
            j = unroll * i + u
            nxt = produce(j + 1, (u + 1) % 2)
            m, l = consume(j, u % 2, cmax, m, l)
            cmax = nxt
        return cmax, m, l

    trips = (n - 1) // unroll
    carry = (produce(0, 0), jnp.full((1, tq), -jnp.inf, F32), jnp.zeros((1, tq), F32))
    cmax, m, l = lax.fori_loop(0, trips, body, carry)
    for j in range(unroll * trips, n):
        nxt = produce(j + 1, (j + 1) % 2) if j + 1 < n else None
        m, l = consume(j, j % 2, cmax, m, l)
        cmax = nxt
    o_ref[0] = (acc_ref[...] / l).T.astype(o_ref.dtype)


def _attention(qt, k, vt, tq, tk, unroll):
    B, S, _ = k.shape
    return pl.pallas_call(
        functools.partial(_attn_kernel, tk=tk, unroll=unroll),
        grid=(B, MLA_HEADS, S // tq),
        in_specs=[pl.BlockSpec((1, 1, MLA_QK_PAD, tq), lambda b, h, i: (b, h, 0, i)),
                  pl.BlockSpec((1, S, MLA_QK_PAD), lambda b, h, i: (b, 0, h)),
                  pl.BlockSpec((1, 1, MLA_V, S), lambda b, h, i: (b, h, 0, 0))],
        out_specs=pl.BlockSpec((1, tq, MLA_V), lambda b, h, i: (b, i, h)),
        out_shape=jax.ShapeDtypeStruct((B, S, MLA_HEADS * MLA_V), BF16),
        scratch_shapes=[pltpu.VMEM((MLA_V, tq), F32), pltpu.VMEM((2, tk, tq), F32)],
        compiler_params=_cparams(("parallel", "parallel", "arbitrary")),
        name="mla_attention",
    )(qt, k, vt)


def _mlpre_kernel(cur_ref, prev_ref, next_ref, gate_ref, cw_ref, cb_ref, wqk_ref, gb_ref,
                  xc_ref, qt_ref, k_ref, gi_ref, gt_ref, ext_ref, *, chunk):
    i = pl.program_id(1)
    ts = cur_ref.shape[1]
    halo = SUBLANE
    pad = ML_CONV // 2
    ext_ref[0:halo, :] = jnp.where(i > 0, prev_ref[0], 0.0)
    ext_ref[halo:halo + ts, :] = cur_ref[0]
    ext_ref[halo + ts:, :] = jnp.where(i < pl.num_programs(1) - 1, next_ref[0], 0.0)
    xc = cb_ref[...] + cw_ref[0:1, :] * ext_ref[halo - pad:halo - pad + ts, :]
    for j in range(1, ML_CONV):
        o = halo - pad + j
        xc = xc + cw_ref[j:j + 1, :] * ext_ref[o:o + ts, :]
    xc = xc * jax.nn.sigmoid(xc)
    xc_ref[0] = xc
    qk = _dot(xc.astype(BF16), wqk_ref[...])
    qt_ref[0] = qk[:, :ML_W].T.astype(BF16)
    k_ref[0] = qk[:, ML_W:].astype(BF16)
    gpre = gate_ref[0] + gb_ref[...]
    lf = jnp.minimum(gpre, 0.0) - jnp.log(1.0 + jnp.exp(-jnp.abs(gpre)))
    g_t = gpre.T[G_OFF:G_OFF + GATE_W, :]
    lf_t = lf.T[G_OFF:G_OFF + GATE_W, :]
    p1 = lf_t.astype(BF16)
    r1 = lf_t - p1.astype(F32)
    p2 = r1.astype(BF16)
    p3 = (r1 - p2.astype(F32)).astype(BF16)
    r = lax.broadcasted_iota(jnp.int32, (chunk, chunk), 0)
    c = lax.broadcasted_iota(jnp.int32, (chunk, chunk), 1)
    tri_f = (r <= c).astype(BF16)
    tri_b = (r >= c).astype(BF16)
    sub = lax.broadcasted_iota(jnp.int32, (GATE_W, chunk), 0)
    cols = []
    for k in range(ts // chunk):
        sl = slice(k * chunk, (k + 1) * chunk)
        cum_f = _dot(p1[:, sl], tri_f) + _dot(p2[:, sl], tri_f) + _dot(p3[:, sl], tri_f)
        cum_b = _dot(p1[:, sl], tri_b) + _dot(p2[:, sl], tri_b) + _dot(p3[:, sl], tri_b)
        cols.append(jnp.where(sub < 2 * ML_HEADS, g_t[:, sl], jnp.where(sub < 3 * ML_HEADS, cum_f, cum_b)))
    gt = jnp.concatenate(cols, axis=1) * LOG2E
    gt = jnp.concatenate([gt, gt[0:2 * ML_HEADS] - gt[2 * ML_HEADS:]], axis=0)
    gt_ref[0] = gt
    gi_ref[0] = jnp.concatenate([gt, jnp.zeros((LANE - GATE_ROWS, ts), F32)], axis=0).T


def _mlstm_pre(mx, gate, cw, cb, wqk, gb, layer, ts, chunk):
    B, S, W = mx.shape
    nblk = ts // SUBLANE
    last = S // SUBLANE - 1
    tok = lambda w: pl.BlockSpec((1, ts, w), lambda b, i: (b, i, 0))
    return pl.pallas_call(
        functools.partial(_mlpre_kernel, chunk=chunk),
        grid=(B, S // ts),
        in_specs=[tok(W),
                  pl.BlockSpec((1, SUBLANE, W), lambda b, i: (b, jnp.maximum(i * nblk - 1, 0), 0)),
                  pl.BlockSpec((1, SUBLANE, W), lambda b, i: (b, jnp.minimum((i + 1) * nblk, last), 0)),
                  tok(LANE), _layer_spec(cw, layer), _layer_spec(cb, layer), _layer_spec(wqk, layer),
                  _layer_spec(gb, layer)],
        out_specs=[tok(W), pl.BlockSpec((1, W, ts), lambda b, i: (b, 0, i)), tok(W), tok(LANE),
                   pl.BlockSpec((1, GATE_ROWS, ts), lambda b, i: (b, 0, i))],
        out_shape=[jax.ShapeDtypeStruct((B, S, W), F32), jax.ShapeDtypeStruct((B, W, S), BF16),
                   jax.ShapeDtypeStruct((B, S, W), BF16), jax.ShapeDtypeStruct((B, S, LANE), F32),
                   jax.ShapeDtypeStruct((B, GATE_ROWS, S), F32)],
        scratch_shapes=[pltpu.VMEM((ts + 2 * SUBLANE, W), F32)],
        compiler_params=_cparams(("parallel", "parallel")),
        name="mlstm_pre",
    )(mx, mx, mx, gate, cw, cb, wqk, gb)


def _mlstm_chunk(blocks, c_ref, n_ref, m_ref):
    nslot = len(blocks)
    bb = nslot // 2
    L = blocks[0][1].shape[0]
    H, dh = ML_HEADS, ML_DIM
    key = lax.broadcasted_iota(jnp.int32, (L, L), 0)
    qry = lax.broadcasted_iota(jnp.int32, (L, L), 1)
    lane_w = lax.broadcasted_iota(jnp.int32, (1, ML_W), 1)
    row_w = lax.broadcasted_iota(jnp.int32, (ML_W, 1), 0)
    hlanes = [(lane_w // dh) == hd for hd in range(H)]
    hrows = [(row_w // dh) == hd for hd in range(H)]
    zero_k = jnp.zeros_like(blocks[0][1])
    zero_v = jnp.zeros_like(blocks[0][2])
    sel8 = (lax.broadcasted_iota(jnp.int32, (SUBLANE, ML_W), 0)
            == lax.broadcasted_iota(jnp.int32, (SUBLANE, ML_W), 1) // dh)
    probs_of = [(si, hd) for si in range(nslot) for hd in range(H)]
    direction = lambda si: si // bb
    gate = lambda kind, si, hd: kind * 2 * H + direction(si) * H + hd
    edge = lambda si: L - 1 if direction(si) == 0 else 0

    QT, K, VT, GI, GT = range(5)
    inter_t, qn_t = [], []
    for si, blk in enumerate(blocks):
        inter_t.append(_dot(c_ref[si].astype(BF16), blk[QT]))
        qn_t.append(_dot(n_ref[si].astype(BF16), blk[QT]))

    def rows_of(kind):
        return jnp.stack([blocks[si][GT][gate(kind, si, hd):gate(kind, si, hd) + 1, :]
                          for si, hd in probs_of])

    m_old = jnp.stack([m_ref[si, hd:hd + 1, 0:1] for si, hd in probs_of])
    bq = rows_of(1)
    imb_r = rows_of(2)
    imb_c = jnp.stack([blocks[si][GI][:, gate(2, si, hd):gate(2, si, hd) + 1]
                       for si, hd in probs_of])
    b_edge = jnp.stack([blocks[si][GT][gate(1, si, hd):gate(1, si, hd) + 1, edge(si):edge(si) + 1]
                        for si, hd in probs_of])
    qn_r = jnp.stack([qn_t[si][hd:hd + 1, :] for si, hd in probs_of])
    qk = jnp.stack([_dot(jnp.where(hlanes[hd], blocks[si][K], zero_k), blocks[si][QT])
                    for si, hd in probs_of])

    raw = imb_c + bq
    half = bb * H
    dmat = jnp.concatenate([jnp.where(key <= qry, raw[:half], -jnp.inf),
                            jnp.where(key >= qry, raw[half:], -jnp.inf)], axis=0)
    m_inter = bq + m_old
    m_t = jnp.maximum(m_inter, jnp.max(dmat, axis=1, keepdims=True))
    w_state = jnp.exp2(m_inter - m_t)
    s = qk * jnp.exp2(dmat - m_t)
    denom = jnp.sum(s, axis=1, keepdims=True) + w_state * qn_r
    scale = 1.0 / jnp.maximum(jnp.abs(denom), jnp.exp2(-m_t))
    wscale = w_state * scale
    probs = s.astype(BF16)

    lw = b_edge + imb_r
    m_new = jnp.maximum(b_edge + m_old, jnp.max(lw, axis=-1, keepdims=True))
    ws = jnp.exp2(lw - m_new)
    dec = jnp.exp2(b_edge + m_old - m_new)

    def per_head_rows(rows, si):
        return jnp.concatenate([jnp.broadcast_to(rows[si * H + hd], (dh, L)) for hd in range(H)], axis=0)

    def per_head_lanes(vals, si):
        e = vals[si * H + H - 1]
        for hd in reversed(range(H - 1)):
            e = jnp.where(hlanes[hd], vals[si * H + hd], e)
        return e

    blockdiag = (row_w // dh) == (lane_w // dh)
    outs = []
    for si, blk in enumerate(blocks):
        k, vt = blk[K], blk[VT]
        v_heads = jnp.concatenate([jnp.where(hrows[hd], vt, zero_v) for hd in range(H)], axis=1)
        intra_t = _dot(v_heads, jnp.concatenate([probs[si * H + hd] for hd in range(H)], axis=0))
        out_t = intra_t * per_head_rows(scale, si) + inter_t[si] * per_head_rows(wscale, si)
        outs.append(out_t.T)

        dec_l = per_head_lanes(dec, si)
        vw = (vt.astype(F32) * per_head_rows(ws, si)).astype(BF16)
        c_new = c_ref[si] * dec_l + _dot(vw, k)
        c_ref[si] = jnp.where(blockdiag, c_new, 0.0)
        ws8 = jnp.concatenate([ws[si * H + hd] for hd in range(H)]
                              + [jnp.zeros((SUBLANE - H, L), F32)], axis=0).astype(BF16)
        n_ref[si] = n_ref[si] * dec_l + jnp.where(sel8, _dot(ws8, k), 0.0)
        m_ref[si, 0:H, :] = jnp.concatenate(
            [jnp.broadcast_to(m_new[si * H + hd], (1, LANE)) for hd in range(H)], axis=0)
    return outs


def _mlscan_kernel(qf_ref, kf_ref, vf_ref, gif_ref, gtf_ref, qb_ref, kb_ref, vb_ref, gib_ref, gtb_ref,
                   hf_ref, hb_ref, c_ref, n_ref, m_ref, *, bb):
    @pl.when(pl.program_id(1) == 0)
    def _():
        c_ref[...] = jnp.zeros_like(c_ref)
        n_ref[...] = jnp.zeros_like(n_ref)
        m_ref[...] = jnp.zeros_like(m_ref)

    fwd = [(qf_ref[b], kf_ref[b], vf_ref[b], gif_ref[b], gtf_ref[b]) for b in range(bb)]
    bwd = [(qb_ref[b], kb_ref[b], vb_ref[b], gib_ref[b], gtb_ref[b]) for b in range(bb)]
    outs = _mlstm_chunk(fwd + bwd, c_ref, n_ref, m_ref)
    for b in range(bb):
        hf_ref[b] = outs[b]
        hb_ref[b] = outs[bb + b]


def _mlstm_scan(qt, k, vt, gi, gt, L, bb):
    B, S, W = k.shape
    nc = S // L
    fw = lambda b, c: c
    bw = lambda b, c: nc - 1 - c
    def specs(cm):
        return [pl.BlockSpec((bb, W, L), lambda b, c: (b, 0, cm(b, c))),
                pl.BlockSpec((bb, L, W), lambda b, c: (b, cm(b, c), 0)),
                pl.BlockSpec((bb, W, L), lambda b, c: (b, 0, cm(b, c))),
                pl.BlockSpec((bb, L, LANE), lambda b, c: (b, cm(b, c), 0)),
                pl.BlockSpec((bb, GATE_ROWS, L), lambda b, c: (b, 0, cm(b, c)))]
    out = jax.ShapeDtypeStruct((B, S, W), F32)
    return pl.pallas_call(
        functools.partial(_mlscan_kernel, bb=bb),
        grid=(B // bb, nc),
        in_specs=specs(fw) + specs(bw),
        out_specs=[pl.BlockSpec((bb, L, W), lambda b, c: (b, c, 0)),
                   pl.BlockSpec((bb, L, W), lambda b, c: (b, nc - 1 - c, 0))],
        out_shape=[out, out],
        scratch_shapes=[pltpu.VMEM((2 * bb, W, W), F32), pltpu.VMEM((2 * bb, SUBLANE, W), F32),
                        pltpu.VMEM((2 * bb, SUBLANE, LANE), F32)],
        compiler_params=_cparams(("parallel", "arbitrary")),
        name="mlstm_scan",
    )(qt, k, vt, gi, gt, qt, k, vt, gi, gt)


def _fnet1_kernel(wr_ref, wi_ref, m1_ref, a_ref):
    n1 = wr_ref.shape[1]
    a = _dot(m1_ref[:, 0:n1], wr_ref[0]) + _dot(m1_ref[:, n1:], wi_ref[0])
    a_ref[0, 0] = a[0:n1].astype(BF16)
    a_ref[0, 1] = a[n1:].astype(BF16)


def _fnet3_kernel(a_ref, tab_ref, w_ref, b_ref, o_ref):
    for j in range(SUBLANE):
        z = jnp.concatenate([a_ref[0, 0, j], a_ref[0, 1, j]], axis=0)
        y = _dot(tab_ref[j], z)
        o_ref[0, :, j, :] = _dot(y.astype(BF16), w_ref[...]) + b_ref[...]


def _fnet(fr, fi, m1, tab3, wbd, bias, layer, tc):
    B, S, W = fr.shape
    n1 = FN_N1
    n2 = S // n1
    cols = n2 * W
    wr = fr.reshape(B, n1, cols)
    wi = fi.reshape(B, n1, cols)
    a = pl.pallas_call(
        _fnet1_kernel,
        grid=(B, cols // tc),
        in_specs=[pl.BlockSpec((1, n1, tc), lambda b, i: (b, 0, i)),
                  pl.BlockSpec((1, n1, tc), lambda b, i: (b, 0, i)),
                  _const_spec(m1.shape)],
        out_specs=pl.BlockSpec((1, 2, n1, tc), lambda b, i: (b, 0, 0, i)),
        out_shape=jax.ShapeDtypeStruct((B, 2, n1, cols), BF16),
        compiler_params=_cparams(("parallel", "parallel")),
        name="fnet_dft1",
    )(wr, wi, m1)
    a = a.reshape(B, 2, n1, n2, W)
    y = pl.pallas_call(
        _fnet3_kernel,
        grid=(B, n1 // SUBLANE),
        in_specs=[pl.BlockSpec((1, 2, SUBLANE, n2, W), lambda b, i: (b, 0, i, 0, 0)),
                  pl.BlockSpec((SUBLANE, n2, 2 * n2), lambda b, i: (i, 0, 0)),
                  _layer_spec(wbd, layer), _layer_spec(bias, layer)],
        out_specs=pl.BlockSpec((1, n2, SUBLANE, W), lambda b, i: (b, 0, i, 0)),
        out_shape=jax.ShapeDtypeStruct((B, n2, n1, W), F32),
        compiler_params=_cparams(("parallel", "parallel")),
        name="fnet_dft2",
    )(a, tab3, wbd, bias)
    return y.reshape(B, S, W)


def _post_kernel(h_ref, ya_ref, hf_ref, hb_ref, xc_ref, mo_ref, yf_ref, p_ref,
                 hn_ref, sk_ref, ones_ref, wo_ref, g2_ref, wg2_ref, wu2_ref, wd_ref,
                 gn_ref, wg_ref, wp_ref, pn_ref, fn_ref, o_ref, acc_ref, *, final, tf):
    tm = h_ref.shape[0]
    a_w = ya_ref.shape[1]
    m_w = a_w + ML_W
    groups = [slice(r, r + tm // ROW_GROUPS) for r in range(0, tm, tm // ROW_GROUPS)]

    def project(r):
        hm = hf_ref[r, :] + hb_ref[r, :]
        sq = hm * hm
        hi = sq.astype(BF16)
        lo = (sq - hi.astype(F32)).astype(BF16)
        ms = (_dot(hi, ones_ref[...]) + _dot(lo, ones_ref[...])) * (1.0 / ML_DIM)
        ym = (hm * lax.rsqrt(ms + EPS) * hn_ref[...] + sk_ref[...] * xc_ref[r, :]) * mo_ref[r, :]
        return (h_ref[r, :] + _dot(ya_ref[r, :], wo_ref[0:a_w, :])
                + _dot(ym.astype(BF16), wo_ref[a_w:m_w, :])
                + _dot(yf_ref[r, :].astype(BF16), wo_ref[m_w:, :]))

    h1 = jnp.concatenate([project(r) for r in groups], axis=0)
    h2 = h1 + _swiglu_half(h1, g2_ref, wg2_ref, wu2_ref, wd_ref, acc_ref, tf)

    for r in groups:
        h2r = h2[r, :]
        e = _rms(_dot(p_ref[r, :].astype(BF16), wp_ref[...]), pn_ref[...])
        gate = jax.nn.sigmoid(_dot(_rms(h2r, gn_ref[...]).astype(BF16), wg_ref[...]))
        h3 = h2r + gate * e
        if final:
            h3 = _rms(h3, fn_ref[...])
        o_ref[r, :] = h3


def _post(h, ya, hf, hb, xc, mo, yf, p, ones_bd, params, final_norm, layer, tm, tf, final):
    T, D = h.shape
    tok = lambda w: pl.BlockSpec((tm, w), lambda i: (i, 0))
    acts = [h, ya, hf, hb, xc, mo, yf]
    hn, sk, wo, g2, wg2, wu2, wd, gn, wg, wp, pn = params
    consts = [hn, sk, ones_bd, wo, g2, wg2, wu2, wd, gn, wg, wp, pn, final_norm]
    shared = (2, len(consts) - 1)
    specs = [_const_spec(c.shape) if j in shared else _layer_spec(c, layer) for j, c in enumerate(consts)]
    return pl.pallas_call(
        functools.partial(_post_kernel, final=final, tf=tf),
        grid=(T // tm,),
        in_specs=([tok(a.shape[1]) for a in acts]
                  + [pl.BlockSpec((None, tm, p.shape[2]), lambda i: (layer, i, 0))] + specs),
        out_specs=tok(D),
        out_shape=jax.ShapeDtypeStruct((T, D), F32),
        scratch_shapes=[pltpu.VMEM((tm, D), F32)],
        compiler_params=_cparams(("parallel",)),
        name="post",
    )(*acts, p, *consts)


def _block_diag(w):
    *lead, G, a, b = w.shape
    eye = jnp.eye(G, dtype=w.dtype)
    return (eye[:, None, :, None] * w[..., :, :, None, :]).reshape(*lead, G * a, G * b)


def _dft_tables(S):
    n1, n2 = FN_N1, S // FN_N1
    c = np.arange(FN_DIM)
    ang = 2.0 * np.pi * np.outer(c, c) / FN_DIM
    eye = np.eye(FN_GROUPS)
    chan = np.concatenate([np.kron(eye, np.cos(ang)), -np.kron(eye, np.sin(ang))], axis=1) / 8.0
    k1 = np.arange(n1)
    a1 = 2.0 * np.pi * np.outer(k1, k1) / n1
    m1 = np.block([[np.cos(a1), np.sin(a1)], [-np.sin(a1), np.cos(a1)]])
    k = k1[:, None, None] + n1 * np.arange(n2)[None, :, None]
    s2 = np.arange(n2)[None, None, :]
    a3 = 2.0 * np.pi * ((k * s2) % S) / S
    tab3 = np.concatenate([np.cos(a3), np.sin(a3)], axis=-1) / np.sqrt(S)
    bf = lambda t: jnp.asarray(t, dtype=F32).astype(BF16)
    return bf(chan), bf(m1), bf(tab3)


def _rope_consts():
    half = MLA_ROPE // 2
    inv = 1.0 / (ROPE_THETA ** (jnp.arange(0, MLA_ROPE, 2, dtype=F32) / MLA_ROPE))
    z = jnp.zeros((half,), F32)
    o = jnp.ones((half,), F32)
    rows = [jnp.concatenate([inv, inv, z, z]), jnp.concatenate([o, o, z, z]),
            jnp.concatenate([z, o, z, z]), jnp.concatenate([-o, z, z, z])]
    rows += [jnp.zeros((LANE,), F32)] * (SUBLANE - len(rows))
    return jnp.stack(rows)


def _tiles(B, S):
    T = B * S
    return dict(tm=min(512, T), tf=256, tmix=min(512, S), tq=min(2048, S), tk=min(1024, S), attn_unroll=2,
                ts=min(512, S), chunk=min(256, S), bb=2 if B % 2 == 0 else 1,
                tc=min(4096, (S // FN_N1) * FN_W),
                trope=min(1024, T))


def kernel(x, p, positions, ffn1_norm, ffn1_w_gate, ffn1_w_up, ffn1_w_down, mix_norm, w_in, mla_q_norm, mla_w_uq, mla_kv_norm, mla_w_ukv, mlstm_conv_w, mlstm_conv_b, mlstm_w_q, mlstm_w_k, mlstm_i_bias, mlstm_f_bias, mlstm_head_norm, mlstm_skip, fnet_w, fnet_b, w_out, ffn2_norm, ffn2_w_gate, ffn2_w_up, ffn2_w_down, ple_gate_norm, ple_w_gate, ple_w_proj, ple_post_norm, final_norm):
    B, S, D = x.shape
    depth = p.shape[0]
    T = B * S
    t = _tiles(B, S)
    Ld = depth
    rows = lambda a: a.reshape(Ld, 1, -1).astype(F32)

    tab = _rope_consts()
    cos_t, sin_a, sin_b = [a.reshape(B, S, LANE) for a in
                           _rope_tables(positions.reshape(T, 1).astype(jnp.int32), tab, t["trope"])]
    dft_chan, dft_m1, dft_tab3 = _dft_tables(S)
    ones_bd = _block_diag(jnp.ones((ML_HEADS, ML_DIM, ML_DIM), BF16))
    qscale = float((MLA_NOPE + MLA_ROPE) ** -0.5 * LOG2E)

    wg1, wu1, wd1 = ffn1_w_gate.astype(BF16), ffn1_w_up.astype(BF16), ffn1_w_down.astype(BF16)
    wg2, wu2, wd2 = ffn2_w_gate.astype(BF16), ffn2_w_up.astype(BF16), ffn2_w_down.astype(BF16)
    o_kr = MLA_Q_LORA + MLA_KV_LORA
    o_mx = o_kr + MLA_ROPE
    o_g = o_mx + 3 * ML_W
    o_f = o_g + GATE_W
    wi = w_in.astype(BF16)
    zpad = lambda n: jnp.zeros((Ld, D, n), BF16)
    win = jnp.concatenate([wi[..., :o_kr], wi[..., o_mx:o_g], wi[..., o_f:], wi[..., o_kr:o_mx],
                           wi[..., o_g:o_f], zpad(LANE - MLA_ROPE - GATE_W)], axis=-1)
    wuq = jnp.pad(mla_w_uq.astype(BF16).reshape(Ld, MLA_Q_LORA, MLA_HEADS, MLA_NOPE + MLA_ROPE),
                  ((0, 0), (0, 0), (0, 0), (0, MLA_QK_PAD - MLA_NOPE - MLA_ROPE)))
    wuq = wuq.reshape(Ld, MLA_Q_LORA, MLA_HEADS * MLA_QK_PAD)
    wukv = mla_w_ukv.astype(BF16).reshape(Ld, MLA_KV_LORA, MLA_HEADS, 2, MLA_NOPE)
    wukv = wukv.transpose(0, 1, 3, 2, 4).reshape(Ld, MLA_KV_LORA, 2 * MLA_HEADS * MLA_NOPE)
    wqk = jnp.concatenate([_block_diag(mlstm_w_q) * (ML_DIM ** -0.5), _block_diag(mlstm_w_k)],
                          axis=-1).astype(BF16)
    gbias = jnp.concatenate([jnp.zeros((Ld, G_OFF), F32), mlstm_i_bias.reshape(Ld, -1),
                             mlstm_f_bias.reshape(Ld, -1), jnp.zeros((Ld, LANE - G_OFF - GATE_W), F32)],
                            axis=-1).reshape(Ld, 1, LANE)
    fw_bd = _block_diag(fnet_w).astype(BF16)
    post_params = [rows(mlstm_head_norm), rows(mlstm_skip), w_out.astype(BF16), rows(ffn2_norm), wg2, wu2, wd2,
                   rows(ple_gate_norm), ple_w_gate.astype(BF16), ple_w_proj.astype(BF16),
                   rows(ple_post_norm)]
    g1, gmix, gq, gkv = rows(ffn1_norm), rows(mix_norm), rows(mla_q_norm), rows(mla_kv_norm)
    conv_w, conv_b, fnet_bias = mlstm_conv_w.astype(F32), rows(mlstm_conv_b), rows(fnet_b)
    p_flat = p.reshape(Ld, T, p.shape[-1])
    fin = final_norm.reshape(1, -1).astype(F32)
    flat = lambda a: a.reshape(T, a.shape[-1])

    h = x.reshape(T, D)
    for i in range(depth):
        h = _ffn(h, g1, wg1, wu1, wd1, i, t["tm"], t["tf"])
        qt, k, vt, mx, mv, mo, gate, fr, fi = _mixin(
            h.reshape(B, S, D), cos_t, sin_a, sin_b, gmix, win, gq, wuq, gkv, wukv, dft_chan, i,
            t["tmix"], qscale)
        y_mla = _attention(qt, k, vt, t["tq"], t["tk"], t["attn_unroll"])
        xc, mqt, mk, gi, gt = _mlstm_pre(mx, gate, conv_w, conv_b, wqk, gbias, i, t["ts"], t["chunk"])
        hf, hb = _mlstm_scan(mqt, mk, mv, gi, gt, t["chunk"], t["bb"])
        y_fnet = _fnet(fr, fi, dft_m1, dft_tab3, fw_bd, fnet_bias, i, t["tc"])
        h = _post(h, flat(y_mla), flat(hf), flat(hb), flat(xc), flat(mo), flat(y_fnet), p_flat, ones_bd,
                  post_params, fin, i, t["tm"], t["tf"], final=(i == depth - 1))
    return h.reshape(B, S, D)
```

```python
import functools

import numpy as np
import jax
import jax.numpy as jnp
from jax import lax
from jax.experimental import pallas as pl
from jax.experimental.pallas import tpu as pltpu

F32 = jnp.float32
BF16 = jnp.bfloat16

EPS = 1e-6
ROPE_THETA = 10000.0
LOG2E = float(np.log2(np.e))
LANE = 128
SUBLANE = 8
MXU_DIM = 256
VMEM_LIMIT = 56 * 1024 * 1024

MLA_HEADS = 4
MLA_Q_LORA = 384
MLA_KV_LORA = 256
MLA_NOPE = 128
MLA_ROPE = 64
MLA_V = 128
MLA_QK_PAD = MXU_DIM
ML_HEADS = 4
ML_DIM = 64
ML_W = ML_HEADS * ML_DIM
ML_CONV = 5
FN_GROUPS = 4
FN_DIM = 64
FN_W = FN_GROUPS * FN_DIM
FN_N1 = 64
GATE_W = 4 * ML_HEADS
GATE_ROWS = 6 * ML_HEADS
ROW_GROUPS = 2

U_CQ = 0
U_CKV = U_CQ + MLA_Q_LORA
U_MX = U_CKV + MLA_KV_LORA
U_MV = U_MX + ML_W
U_MO = U_MV + ML_W
U_FIN = U_MO + ML_W
U_KR = U_FIN + FN_W
G_OFF = MLA_ROPE
U_TOT = U_KR + LANE


def _cparams(sem):
    return pltpu.CompilerParams(dimension_semantics=sem, vmem_limit_bytes=VMEM_LIMIT)


def _const_spec(shape):
    nd = len(shape)
    return pl.BlockSpec(shape, lambda *_: (0,) * nd, pipeline_mode=pl.Buffered(1))


def _layer_spec(a, layer):
    nd = a.ndim - 1
    return pl.BlockSpec((None,) + a.shape[1:], lambda *_: (layer,) + (0,) * nd,
                        pipeline_mode=pl.Buffered(1))


def _rms(x, g):
    ms = jnp.mean(x * x, axis=-1, keepdims=True)
    return x * lax.rsqrt(ms + EPS) * g


def _dot(a, b):
    return jnp.dot(a, b, preferred_element_type=F32)


def _swiglu_half(x, g_ref, wg_ref, wu_ref, wd_ref, acc_ref, tf):
    xn = _rms(x, g_ref[...]).astype(BF16)
    for c in range(wd_ref.shape[0] // tf):
        g = _dot(xn, wg_ref[:, c * tf:(c + 1) * tf])
        u = _dot(xn, wu_ref[:, c * tf:(c + 1) * tf])
        a = (g * jax.nn.sigmoid(g) * u).astype(BF16)
        part = _dot(a, wd_ref[c * tf:(c + 1) * tf, :])
        if c == 0:
            acc_ref[...] = part
        else:
            acc_ref[...] += part
    return 0.5 * acc_ref[...]


def _ffn_kernel(h_ref, g_ref, wg_ref, wu_ref, wd_ref, o_ref, acc_ref, *, tf):
    x = h_ref[...]
    o_ref[...] = x + _swiglu_half(x, g_ref, wg_ref, wu_ref, wd_ref, acc_ref, tf)


def _ffn(h, g, wg, wu, wd, layer, tm, tf):
    T, D = h.shape
    return pl.pallas_call(
        functools.partial(_ffn_kernel, tf=tf),
        grid=(T // tm,),
        in_specs=[pl.BlockSpec((tm, D), lambda i: (i, 0)),
                  _layer_spec(g, layer), _layer_spec(wg, layer), _layer_spec(wu, layer),
                  _layer_spec(wd, layer)],
        out_specs=pl.BlockSpec((tm, D), lambda i: (i, 0)),
        out_shape=jax.ShapeDtypeStruct((T, D), F32),
        scratch_shapes=[pltpu.VMEM((tm, D), F32)],
        compiler_params=_cparams(("parallel",)),
        name="ffn",
    )(h, g, wg, wu, wd)


def _rope_kernel(pos_ref, tab_ref, cos_ref, sina_ref, sinb_ref):
    ang = pos_ref[...].astype(F32) * tab_ref[0:1, :]
    c = jnp.cos(ang)
    s = jnp.sin(ang)
    cos_ref[...] = c * tab_ref[1:2, :]
    sina_ref[...] = s * tab_ref[2:3, :]
    sinb_ref[...] = s * tab_ref[3:4, :]


def _rope_tables(pos, tab, tm):
    T = pos.shape[0]
    out = jax.ShapeDtypeStruct((T, LANE), F32)
    spec = pl.BlockSpec((tm, LANE), lambda i: (i, 0))
    return pl.pallas_call(
        _rope_kernel,
        grid=(T // tm,),
        in_specs=[pl.BlockSpec((tm, 1), lambda i: (i, 0)), _const_spec(tab.shape)],
        out_specs=[spec, spec, spec],
        out_shape=[out, out, out],
        compiler_params=_cparams(("parallel",)),
        name="rope_tables",
    )(pos, tab)


def _rope(blk, cos_t, sin_a, sin_b):
    half = MLA_ROPE // 2
    return (blk * cos_t + pltpu.roll(blk, half, 1) * sin_a
            + pltpu.roll(blk, LANE - half, 1) * sin_b)


def _mixin_kernel(h_ref, hprev_ref, hnext_ref, cos_ref, sina_ref, sinb_ref, g_ref, win_ref, qn_ref, wuq_ref,
                  kvn_ref, wukv_ref, cw_ref, cb_ref, wqk_ref, gb_ref, dft_ref,
                  qt_ref, k_ref, vt_ref, mv_ref, mo_ref, fr_ref, fi_ref, xc_ref, mqt_ref, mk_ref, gi_ref,
                  gt_ref, ext_ref, gate_ref, *, qscale, chunk):
    tm = h_ref.shape[1]
    halo = SUBLANE
    for r0 in range(0, tm, tm // ROW_GROUPS):
        r = slice(r0, r0 + tm // ROW_GROUPS)
        x = h_ref[0, r, :]
        xn = _rms(x, g_ref[...]).astype(BF16)
        u = _dot(xn, win_ref[...])
        cos_t, sin_a, sin_b = cos_ref[0, r, :], sina_ref[0, r, :], sinb_ref[0, r, :]

        cq = _rms(u[:, U_CQ:U_CKV], qn_ref[...]).astype(BF16)
        q = _dot(cq, wuq_ref[...])
        for hd in range(MLA_HEADS):
            o = hd * MLA_QK_PAD
            qt_ref[0, hd, 0:MLA_NOPE, r] = (q[:, o:o + MLA_NOPE] * qscale).T.astype(BF16)
            qr = _rope(q[:, o + MLA_NOPE:o + MLA_QK_PAD], cos_t, sin_a, sin_b)
            qt_ref[0, hd, MLA_NOPE:MLA_QK_PAD, r] = (qr * qscale).T.astype(BF16)

        ckv = _rms(u[:, U_CKV:U_MX], kvn_ref[...]).astype(BF16)
        kv = _dot(ckv, wukv_ref[...])
        kr = _rope(u[:, U_KR:U_KR + LANE], cos_t, sin_a, sin_b).astype(BF16)
        for hd in range(MLA_HEADS):
            o = hd * MLA_QK_PAD
            k_ref[0, r, o:o + MLA_NOPE] = kv[:, hd * MLA_NOPE:(hd + 1) * MLA_NOPE].astype(BF16)
            k_ref[0, r, o + MLA_NOPE:o + MLA_QK_PAD] = kr
            vo = (MLA_HEADS + hd) * MLA_NOPE
            vt_ref[0, hd, :, r] = kv[:, vo:vo + MLA_V].T.astype(BF16)

        ext_ref[halo + r0:halo + r0 + tm // ROW_GROUPS, :] = u[:, U_MX:U_MV]
        mv_ref[0, :, r] = u[:, U_MV:U_MO].T.astype(BF16)
        mo_ref[0, r, :] = jax.nn.sigmoid(u[:, U_MO:U_FIN])
        gate_ref[r, :] = u[:, U_KR:U_TOT]

        f = _dot(u[:, U_FIN:U_KR].astype(BF16), dft_ref[...])
        fr_ref[0, r, :] = f[:, :FN_W].astype(BF16)
        fi_ref[0, r, :] = f[:, FN_W:].astype(BF16)

    i = pl.program_id(1)
    xh = jnp.concatenate([hprev_ref[0], hnext_ref[0]], axis=0)
    mh = _dot(_rms(xh, g_ref[...]).astype(BF16), win_ref[:, U_MX:U_MV])
    ext_ref[0:halo, :] = jnp.where(i > 0, mh[0:halo], 0.0)
    ext_ref[halo + tm:, :] = jnp.where(i < pl.num_programs(1) - 1, mh[halo:], 0.0)
    _mlstm_pre(ext_ref, gate_ref[...], cw_ref, cb_ref, wqk_ref, gb_ref,
               xc_ref, mqt_ref, mk_ref, gi_ref, gt_ref, chunk)


def _mixin(h3, cos_t, sin_a, sin_b, g, win, qn, wuq, kvn, wukv, cw, cb, wqk, gb, dft, layer, tm, qscale, chunk):
    B, S, D = h3.shape
    nblk = tm // SUBLANE
    last = S // SUBLANE - 1
    tok = lambda w: pl.BlockSpec((1, tm, w), lambda b, i: (b, i, 0))
    tcol = lambda w: pl.BlockSpec((1, w, tm), lambda b, i: (b, 0, i))
    outs = [
        (jax.ShapeDtypeStruct((B, MLA_HEADS, MLA_QK_PAD, S), BF16),
         pl.BlockSpec((1, MLA_HEADS, MLA_QK_PAD, tm), lambda b, i: (b, 0, 0, i))),
        (jax.ShapeDtypeStruct((B, S, MLA_HEADS * MLA_QK_PAD), BF16), tok(MLA_HEADS * MLA_QK_PAD)),
        (jax.ShapeDtypeStruct((B, MLA_HEADS, MLA_V, S), BF16),
         pl.BlockSpec((1, MLA_HEADS, MLA_V, tm), lambda b, i: (b, 0, 0, i))),
        (jax.ShapeDtypeStruct((B, ML_W, S), BF16), tcol(ML_W)),
        (jax.ShapeDtypeStruct((B, S, ML_W), F32), tok(ML_W)),
        (jax.ShapeDtypeStruct((B, S, FN_W), BF16), tok(FN_W)),
        (jax.ShapeDtypeStruct((B, S, FN_W), BF16), tok(FN_W)),
        (jax.ShapeDtypeStruct((B, S, ML_W), F32), tok(ML_W)),
        (jax.ShapeDtypeStruct((B, ML_W, S), BF16), tcol(ML_W)),
        (jax.ShapeDtypeStruct((B, S, ML_W), BF16), tok(ML_W)),
        (jax.ShapeDtypeStruct((B, S, LANE), F32), tok(LANE)),
        (jax.ShapeDtypeStruct((B, GATE_ROWS, S), F32), tcol(GATE_ROWS)),
    ]
    consts = [g, win, qn, wuq, kvn, wukv, cw, cb, wqk, gb]
    return pl.pallas_call(
        functools.partial(_mixin_kernel, qscale=qscale, chunk=chunk),
        grid=(B, S // tm),
        in_specs=([tok(D),
                   pl.BlockSpec((1, SUBLANE, D), lambda b, i: (b, jnp.maximum(i * nblk - 1, 0), 0)),
                   pl.BlockSpec((1, SUBLANE, D), lambda b, i: (b, jnp.minimum((i + 1) * nblk, last), 0)),
                   tok(LANE), tok(LANE), tok(LANE)] + [_layer_spec(c, layer) for c in consts]
                  + [_const_spec(dft.shape)]),
        out_specs=[o[1] for o in outs],
        out_shape=[o[0] for o in outs],
        scratch_shapes=[pltpu.VMEM((tm + 2 * SUBLANE, ML_W), F32), pltpu.VMEM((tm, LANE), F32)],
        compiler_params=_cparams(("parallel", "parallel")),
        name="mix_in",
    )(h3, h3, h3, cos_t, sin_a, sin_b, *consts, dft)


def _attn_kernel(qt_ref, k_ref, vt_ref, o_ref, acc_ref, s_ref, *, tk, unroll):
    qt = qt_ref[0, 0]
    S = k_ref.shape[1]
    n = S // tk
    tq = qt.shape[1]
    acc_ref[...] = jnp.zeros_like(acc_ref)

    def produce(j, slot):
        ks = pl.multiple_of(j * tk, tk)
        st = _dot(k_ref[0, pl.ds(ks, tk), :], qt)
        s_ref[slot] = st
        return jnp.max(st, axis=0, keepdims=True)

    def consume(j, slot, cmax, m_old, l_old):
        ks = pl.multiple_of(j * tk, tk)
        m_new = jnp.maximum(m_old, cmax)
        p = jnp.exp2(s_ref[slot] - m_new)
        alpha = jnp.exp2(m_old - m_new)
        l_new = alpha * l_old + jnp.sum(p, axis=0, keepdims=True)
        acc_ref[...] = alpha * acc_ref[...] + _dot(vt_ref[0, 0, :, pl.ds(ks, tk)], p.astype(BF16))
        return m_new, l_new

    def body(i, carry):
        cmax, m, l = carry
        for u in range(unroll):
            j = unroll * i + u
            nxt = produce(j + 1, (u + 1) % 2)
            m, l = consume(j, u % 2, cmax, m, l)
            cmax = nxt
        return cmax, m, l

    trips = (n - 1) // unroll
    carry = (produce(0, 0), jnp.full((1, tq), -jnp.inf, F32), jnp.zeros((1, tq), F32))
    cmax, m, l = lax.fori_loop(0, trips, body, carry)
    for j in range(unroll * trips, n):
        nxt = produce(j + 1, (j + 1) % 2) if j + 1 < n else None
        m, l = consume(j, j % 2, cmax, m, l)
        cmax = nxt
    o_ref[0] = (acc_ref[...] / l).T.astype(o_ref.dtype)


def _attention(qt, k, vt, tq, tk, unroll):
    B, S, _ = k.shape
    assert S % tq == 0 and S % tk == 0 and unroll % 2 == 0, "logit buffers alternate by chunk parity"
    return pl.pallas_call(
        functools.partial(_attn_kernel, tk=tk, unroll=unroll),
        grid=(B, MLA_HEADS, S // tq),
        in_specs=[pl.BlockSpec((1, 1, MLA_QK_PAD, tq), lambda b, h, i: (b, h, 0, i)),
                  pl.BlockSpec((1, S, MLA_QK_PAD), lambda b, h, i: (b, 0, h)),
                  pl.BlockSpec((1, 1, MLA_V, S), lambda b, h, i: (b, h, 0, 0))],
        out_specs=pl.BlockSpec((1, tq, MLA_V), lambda b, h, i: (b, i, h)),
        out_shape=jax.ShapeDtypeStruct((B, S, MLA_HEADS * MLA_V), BF16),
        scratch_shapes=[pltpu.VMEM((MLA_V, tq), F32), pltpu.VMEM((2, tk, tq), F32)],
        compiler_params=_cparams(("parallel", "parallel", "arbitrary")),
        name="mla_attention",
    )(qt, k, vt)


def _mlstm_pre(ext_ref, gate, cw_ref, cb_ref, wqk_ref, gb_ref, xc_ref, qt_ref, k_ref, gi_ref, gt_ref, chunk):
    halo = SUBLANE
    ts = ext_ref.shape[0] - 2 * halo
    pad = ML_CONV // 2
    xc = cb_ref[...] + cw_ref[0:1, :] * ext_ref[halo - pad:halo - pad + ts, :]
    for j in range(1, ML_CONV):
        o = halo - pad + j
        xc = xc + cw_ref[j:j + 1, :] * ext_ref[o:o + ts, :]
    xc = xc * jax.nn.sigmoid(xc)
    xc_ref[0] = xc
    qk = _dot(xc.astype(BF16), wqk_ref[...])
    qt_ref[0] = qk[:, :ML_W].T.astype(BF16)
    k_ref[0] = qk[:, ML_W:].astype(BF16)
    g_t = (gate + gb_ref[...]).T[G_OFF:G_OFF + GATE_W, :]
    lf_t = jnp.minimum(g_t, 0.0) - jnp.log(1.0 + jnp.exp(-jnp.abs(g_t)))
    p1 = lf_t.astype(BF16)
    r1 = lf_t - p1.astype(F32)
    p2 = r1.astype(BF16)
    p3 = (r1 - p2.astype(F32)).astype(BF16)
    r = lax.broadcasted_iota(jnp.int32, (chunk, chunk), 0)
    c = lax.broadcasted_iota(jnp.int32, (chunk, chunk), 1)
    tri_f = (r <= c).astype(BF16)
    tri_b = (r >= c).astype(BF16)
    sub = lax.broadcasted_iota(jnp.int32, (GATE_W, chunk), 0)
    cols = []
    for k in range(ts // chunk):
        sl = slice(k * chunk, (k + 1) * chunk)
        cum_f = _dot(p1[:, sl], tri_f) + _dot(p2[:, sl], tri_f) + _dot(p3[:, sl], tri_f)
        cum_b = _dot(p1[:, sl], tri_b) + _dot(p2[:, sl], tri_b) + _dot(p3[:, sl], tri_b)
        cols.append(jnp.where(sub < 2 * ML_HEADS, g_t[:, sl], jnp.where(sub < 3 * ML_HEADS, cum_f, cum_b)))
    gt = jnp.concatenate(cols, axis=1) * LOG2E
    gt = jnp.concatenate([gt, gt[0:2 * ML_HEADS] - gt[2 * ML_HEADS:]], axis=0)
    gt_ref[0] = gt
    gi_ref[0] = jnp.concatenate([gt, jnp.zeros((LANE - GATE_ROWS, ts), F32)], axis=0).T


def _mlstm_chunk(blocks, c_ref, n_ref, m_ref):
    nslot = len(blocks)
    bb = nslot // 2
    L = blocks[0][1].shape[0]
    H, dh = ML_HEADS, ML_DIM
    key = lax.broadcasted_iota(jnp.int32, (L, L), 0)
    qry = lax.broadcasted_iota(jnp.int32, (L, L), 1)
    lane_w = lax.broadcasted_iota(jnp.int32, (1, ML_W), 1)
    row_w = lax.broadcasted_iota(jnp.int32, (ML_W, 1), 0)
    hlanes = [(lane_w // dh) == hd for hd in range(H)]
    hrows = [(row_w // dh) == hd for hd in range(H)]
    zero_k = jnp.zeros_like(blocks[0][1])
    zero_v = jnp.zeros_like(blocks[0][2])
    sel8 = (lax.broadcasted_iota(jnp.int32, (SUBLANE, ML_W), 0)
            == lax.broadcasted_iota(jnp.int32, (SUBLANE, ML_W), 1) // dh)
    probs_of = [(si, hd) for si in range(nslot) for hd in range(H)]
    direction = lambda si: si // bb
    gate = lambda kind, si, hd: kind * 2 * H + direction(si) * H + hd
    edge = lambda si: L - 1 if direction(si) == 0 else 0

    QT, K, VT, GI, GT = range(5)
    inter_t, qn_t = [], []
    for si, blk in enumerate(blocks):
        inter_t.append(_dot(c_ref[si].astype(BF16), blk[QT]))
        qn_t.append(_dot(n_ref[si].astype(BF16), blk[QT]))

    def rows_of(kind):
        return jnp.stack([blocks[si][GT][gate(kind, si, hd):gate(kind, si, hd) + 1, :]
                          for si, hd in probs_of])

    m_old = jnp.stack([m_ref[si, hd:hd + 1, 0:1] for si, hd in probs_of])
    bq = rows_of(1)
    imb_r = rows_of(2)
    imb_c = jnp.stack([blocks[si][GI][:, gate(2, si, hd):gate(2, si, hd) + 1]
                       for si, hd in probs_of])
    b_edge = jnp.stack([blocks[si][GT][gate(1, si, hd):gate(1, si, hd) + 1, edge(si):edge(si) + 1]
                        for si, hd in probs_of])
    qn_r = jnp.stack([qn_t[si][hd:hd + 1, :] for si, hd in probs_of])
    qk = jnp.stack([_dot(jnp.where(hlanes[hd], blocks[si][K], zero_k), blocks[si][QT])
                    for si, hd in probs_of])

    raw = imb_c + bq
    half = bb * H
    dmat = jnp.concatenate([jnp.where(key <= qry, raw[:half], -jnp.inf),
                            jnp.where(key >= qry, raw[half:], -jnp.inf)], axis=0)
    m_inter = bq + m_old
    m_t = jnp.maximum(m_inter, jnp.max(dmat, axis=1, keepdims=True))
    w_state = jnp.exp2(m_inter - m_t)
    s = qk * jnp.exp2(dmat - m_t)
    denom = jnp.sum(s, axis=1, keepdims=True) + w_state * qn_r
    scale = 1.0 / jnp.maximum(jnp.abs(denom), jnp.exp2(-m_t))
    wscale = w_state * scale
    probs = s.astype(BF16)

    lw = b_edge + imb_r
    m_new = jnp.maximum(b_edge + m_old, jnp.max(lw, axis=-1, keepdims=True))
    ws = jnp.exp2(lw - m_new)
    dec = jnp.exp2(b_edge + m_old - m_new)

    def per_head_rows(rows, si):
        return jnp.concatenate([jnp.broadcast_to(rows[si * H + hd], (dh, L)) for hd in range(H)], axis=0)

    def per_head_lanes(vals, si):
        e = vals[si * H + H - 1]
        for hd in reversed(range(H - 1)):
            e = jnp.where(hlanes[hd], vals[si * H + hd], e)
        return e

    blockdiag = (row_w // dh) == (lane_w // dh)
    outs = []
    for si, blk in enumerate(blocks):
        k, vt = blk[K], blk[VT]
        v_heads = jnp.concatenate([jnp.where(hrows[hd], vt, zero_v) for hd in range(H)], axis=1)
        intra_t = _dot(v_heads, jnp.concatenate([probs[si * H + hd] for hd in range(H)], axis=0))
        out_t = intra_t * per_head_rows(scale, si) + inter_t[si] * per_head_rows(wscale, si)
        outs.append(out_t.T)

        dec_l = per_head_lanes(dec, si)
        vw = (vt.astype(F32) * per_head_rows(ws, si)).astype(BF16)
        c_new = c_ref[si] * dec_l + _dot(vw, k)
        c_ref[si] = jnp.where(blockdiag, c_new, 0.0)
        ws8 = jnp.concatenate([ws[si * H + hd] for hd in range(H)]
                              + [jnp.zeros((SUBLANE - H, L), F32)], axis=0).astype(BF16)
        n_ref[si] = n_ref[si] * dec_l + jnp.where(sel8, _dot(ws8, k), 0.0)
        m_ref[si, 0:H, :] = jnp.concatenate(
            [jnp.broadcast_to(m_new[si * H + hd], (1, LANE)) for hd in range(H)], axis=0)
    return outs


def _mlscan_kernel(qf_ref, kf_ref, vf_ref, gif_ref, gtf_ref, qb_ref, kb_ref, vb_ref, gib_ref, gtb_ref,
                   hf_ref, hb_ref, c_ref, n_ref, m_ref, *, bb):
    @pl.when(pl.program_id(1) == 0)
    def _():
        c_ref[...] = jnp.zeros_like(c_ref)
        n_ref[...] = jnp.zeros_like(n_ref)
        m_ref[...] = jnp.zeros_like(m_ref)

    fwd = [(qf_ref[b], kf_ref[b], vf_ref[b], gif_ref[b], gtf_ref[b]) for b in range(bb)]
    bwd = [(qb_ref[b], kb_ref[b], vb_ref[b], gib_ref[b], gtb_ref[b]) for b in range(bb)]
    outs = _mlstm_chunk(fwd + bwd, c_ref, n_ref, m_ref)
    for b in range(bb):
        hf_ref[b] = outs[b]
        hb_ref[b] = outs[bb + b]


def _mlstm_scan(qt, k, vt, gi, gt, L, bb):
    B, S, W = k.shape
    nc = S // L
    fw = lambda b, c: c
    bw = lambda b, c: nc - 1 - c
    def specs(cm):
        return [pl.BlockSpec((bb, W, L), lambda b, c: (b, 0, cm(b, c))),
                pl.BlockSpec((bb, L, W), lambda b, c: (b, cm(b, c), 0)),
                pl.BlockSpec((bb, W, L), lambda b, c: (b, 0, cm(b, c))),
                pl.BlockSpec((bb, L, LANE), lambda b, c: (b, cm(b, c), 0)),
                pl.BlockSpec((bb, GATE_ROWS, L), lambda b, c: (b, 0, cm(b, c)))]
    out = jax.ShapeDtypeStruct((B, S, W), F32)
    return pl.pallas_call(
        functools.partial(_mlscan_kernel, bb=bb),
        grid=(B // bb, nc),
        in_specs=specs(fw) + specs(bw),
        out_specs=[pl.BlockSpec((bb, L, W), lambda b, c: (b, c, 0)),
                   pl.BlockSpec((bb, L, W), lambda b, c: (b, nc - 1 - c, 0))],
        out_shape=[out, out],
        scratch_shapes=[pltpu.VMEM((2 * bb, W, W), F32), pltpu.VMEM((2 * bb, SUBLANE, W), F32),
                        pltpu.VMEM((2 * bb, SUBLANE, LANE), F32)],
        compiler_params=_cparams(("parallel", "arbitrary")),
        name="mlstm_scan",
    )(qt, k, vt, gi, gt, qt, k, vt, gi, gt)


def _fnet1_kernel(wr_ref, wi_ref, m1_ref, a_ref):
    n1 = wr_ref.shape[1]
    a = _dot(m1_ref[:, 0:n1], wr_ref[0]) + _dot(m1_ref[:, n1:], wi_ref[0])
    a_ref[0, 0] = a[0:n1].astype(BF16)
    a_ref[0, 1] = a[n1:].astype(BF16)


def _fnet3_kernel(a_ref, tab_ref, w_ref, b_ref, o_ref):
    for j in range(SUBLANE):
        z = jnp.concatenate([a_ref[0, 0, j], a_ref[0, 1, j]], axis=0)
        y = _dot(tab_ref[j], z)
        o_ref[0, :, j, :] = _dot(y.astype(BF16), w_ref[...]) + b_ref[...]


def _fnet(fr, fi, m1, tab3, wbd, bias, layer, tc):
    B, S, W = fr.shape
    n1 = FN_N1
    n2 = S // n1
    cols = n2 * W
    wr = fr.reshape(B, n1, cols)
    wi = fi.reshape(B, n1, cols)
    a = pl.pallas_call(
        _fnet1_kernel,
        grid=(B, cols // tc),
        in_specs=[pl.BlockSpec((1, n1, tc), lambda b, i: (b, 0, i)),
                  pl.BlockSpec((1, n1, tc), lambda b, i: (b, 0, i)),
                  _const_spec(m1.shape)],
        out_specs=pl.BlockSpec((1, 2, n1, tc), lambda b, i: (b, 0, 0, i)),
        out_shape=jax.ShapeDtypeStruct((B, 2, n1, cols), BF16),
        compiler_params=_cparams(("parallel", "parallel")),
        name="fnet_dft1",
    )(wr, wi, m1)
    a = a.reshape(B, 2, n1, n2, W)
    y = pl.pallas_call(
        _fnet3_kernel,
        grid=(B, n1 // SUBLANE),
        in_specs=[pl.BlockSpec((1, 2, SUBLANE, n2, W), lambda b, i: (b, 0, i, 0, 0)),
                  pl.BlockSpec((SUBLANE, n2, 2 * n2), lambda b, i: (i, 0, 0)),
                  _layer_spec(wbd, layer), _layer_spec(bias, layer)],
        out_specs=pl.BlockSpec((1, n2, SUBLANE, W), lambda b, i: (b, 0, i, 0)),
        out_shape=jax.ShapeDtypeStruct((B, n2, n1, W), F32),
        compiler_params=_cparams(("parallel", "parallel")),
        name="fnet_dft2",
    )(a, tab3, wbd, bias)
    return y.reshape(B, S, W)


def _post_kernel(h_ref, ya_ref, hf_ref, hb_ref, xc_ref, mo_ref, yf_ref, p_ref,
                 hn_ref, sk_ref, ones_ref, wo_ref, g2_ref, wg2_ref, wu2_ref, wd_ref,
                 gn_ref, wg_ref, wp_ref, pn_ref, fn_ref, o_ref, acc_ref, *, final, tf):
    tm = h_ref.shape[0]
    a_w = ya_ref.shape[1]
    m_w = a_w + ML_W
    groups = [slice(r, r + tm // ROW_GROUPS) for r in range(0, tm, tm // ROW_GROUPS)]

    def project(r):
        hm = hf_ref[r, :] + hb_ref[r, :]
        sq = hm * hm
        hi = sq.astype(BF16)
        lo = (sq - hi.astype(F32)).astype(BF16)
        ms = (_dot(hi, ones_ref[...]) + _dot(lo, ones_ref[...])) * (1.0 / ML_DIM)
        ym = (hm * lax.rsqrt(ms + EPS) * hn_ref[...] + sk_ref[...] * xc_ref[r, :]) * mo_ref[r, :]
        return (h_ref[r, :] + _dot(ya_ref[r, :], wo_ref[0:a_w, :])
                + _dot(ym.astype(BF16), wo_ref[a_w:m_w, :])
                + _dot(yf_ref[r, :].astype(BF16), wo_ref[m_w:, :]))

    h1 = jnp.concatenate([project(r) for r in groups], axis=0)
    h2 = h1 + _swiglu_half(h1, g2_ref, wg2_ref, wu2_ref, wd_ref, acc_ref, tf)

    for r in groups:
        h2r = h2[r, :]
        e = _rms(_dot(p_ref[r, :].astype(BF16), wp_ref[...]), pn_ref[...])
        gate = jax.nn.sigmoid(_dot(_rms(h2r, gn_ref[...]).astype(BF16), wg_ref[...]))
        h3 = h2r + gate * e
        if final:
            h3 = _rms(h3, fn_ref[...])
        o_ref[r, :] = h3


def _post(h, ya, hf, hb, xc, mo, yf, p, ones_bd, params, final_norm, layer, tm, tf, final):
    T, D = h.shape
    tok = lambda w: pl.BlockSpec((tm, w), lambda i: (i, 0))
    acts = [h, ya, hf, hb, xc, mo, yf]
    hn, sk, wo, g2, wg2, wu2, wd, gn, wg, wp, pn = params
    consts = [hn, sk, ones_bd, wo, g2, wg2, wu2, wd, gn, wg, wp, pn, final_norm]
    shared = (2, len(consts) - 1)
    specs = [_const_spec(c.shape) if j in shared else _layer_spec(c, layer) for j, c in enumerate(consts)]
    return pl.pallas_call(
        functools.partial(_post_kernel, final=final, tf=tf),
        grid=(T // tm,),
        in_specs=([tok(a.shape[1]) for a in acts]
                  + [pl.BlockSpec((None, tm, p.shape[2]), lambda i: (layer, i, 0))] + specs),
        out_specs=tok(D),
        out_shape=jax.ShapeDtypeStruct((T, D), F32),
        scratch_shapes=[pltpu.VMEM((tm, D), F32)],
        compiler_params=_cparams(("parallel",)),
        name="post",
    )(*acts, p, *consts)


def _block_diag(w):
    *lead, G, a, b = w.shape
    eye = jnp.eye(G, dtype=w.dtype)
    return (eye[:, None, :, None] * w[..., :, :, None, :]).reshape(*lead, G * a, G * b)


def _dft_tables(S):
    n1, n2 = FN_N1, S // FN_N1
    c = np.arange(FN_DIM)
    ang = 2.0 * np.pi * np.outer(c, c) / FN_DIM
    eye = np.eye(FN_GROUPS)
    chan = np.concatenate([np.kron(eye, np.cos(ang)), -np.kron(eye, np.sin(ang))], axis=1) / np.sqrt(FN_DIM)
    k1 = np.arange(n1)
    a1 = 2.0 * np.pi * np.outer(k1, k1) / n1
    m1 = np.block([[np.cos(a1), np.sin(a1)], [-np.sin(a1), np.cos(a1)]])
    k = k1[:, None, None] + n1 * np.arange(n2)[None, :, None]
    s2 = np.arange(n2)[None, None, :]
    a3 = 2.0 * np.pi * ((k * s2) % S) / S
    tab3 = np.concatenate([np.cos(a3), np.sin(a3)], axis=-1) / np.sqrt(S)
    bf = lambda t: jnp.asarray(t, dtype=F32).astype(BF16)
    return bf(chan), bf(m1), bf(tab3)


def _rope_consts():
    half = MLA_ROPE // 2
    inv = 1.0 / (ROPE_THETA ** (jnp.arange(0, MLA_ROPE, 2, dtype=F32) / MLA_ROPE))
    z = jnp.zeros((half,), F32)
    o = jnp.ones((half,), F32)
    rows = [jnp.concatenate([inv, inv, z, z]), jnp.concatenate([o, o, z, z]),
            jnp.concatenate([z, o, z, z]), jnp.concatenate([-o, z, z, z])]
    rows += [jnp.zeros((LANE,), F32)] * (SUBLANE - len(rows))
    return jnp.stack(rows)


def _tiles(B, S):
    T = B * S
    return dict(tm=min(512, T), tf=MXU_DIM, tmix=min(512, S), tq=min(2048, S), tk=min(1024, S), attn_unroll=2,
                chunk=min(256, S), bb=2 if B % 2 == 0 else 1,
                tc=min(4096, (S // FN_N1) * FN_W),
                trope=min(1024, T))


def kernel(x, p, positions, ffn1_norm, ffn1_w_gate, ffn1_w_up, ffn1_w_down, mix_norm, w_in, mla_q_norm, mla_w_uq, mla_kv_norm, mla_w_ukv, mlstm_conv_w, mlstm_conv_b, mlstm_w_q, mlstm_w_k, mlstm_i_bias, mlstm_f_bias, mlstm_head_norm, mlstm_skip, fnet_w, fnet_b, w_out, ffn2_norm, ffn2_w_gate, ffn2_w_up, ffn2_w_down, ple_gate_norm, ple_w_gate, ple_w_proj, ple_post_norm, final_norm):
    B, S, D = x.shape
    depth = p.shape[0]
    T = B * S
    t = _tiles(B, S)
    Ld = depth
    rows = lambda a: a.reshape(Ld, 1, -1).astype(F32)

    tab = _rope_consts()
    cos_t, sin_a, sin_b = [a.reshape(B, S, LANE) for a in
                           _rope_tables(positions.reshape(T, 1).astype(jnp.int32), tab, t["trope"])]
    dft_chan, dft_m1, dft_tab3 = _dft_tables(S)
    ones_bd = _block_diag(jnp.ones((ML_HEADS, ML_DIM, ML_DIM), BF16))
    qscale = float((MLA_NOPE + MLA_ROPE) ** -0.5 * LOG2E)

    wg1, wu1, wd1 = ffn1_w_gate.astype(BF16), ffn1_w_up.astype(BF16), ffn1_w_down.astype(BF16)
    wg2, wu2, wd2 = ffn2_w_gate.astype(BF16), ffn2_w_up.astype(BF16), ffn2_w_down.astype(BF16)
    o_kr = MLA_Q_LORA + MLA_KV_LORA
    o_mx = o_kr + MLA_ROPE
    o_g = o_mx + 3 * ML_W
    o_f = o_g + GATE_W
    wi = w_in.astype(BF16)
    zpad = lambda n: jnp.zeros((Ld, D, n), BF16)
    win = jnp.concatenate([wi[..., :o_kr], wi[..., o_mx:o_g], wi[..., o_f:], wi[..., o_kr:o_mx],
                           wi[..., o_g:o_f], zpad(LANE - MLA_ROPE - GATE_W)], axis=-1)
    wuq = jnp.pad(mla_w_uq.astype(BF16).reshape(Ld, MLA_Q_LORA, MLA_HEADS, MLA_NOPE + MLA_ROPE),
                  ((0, 0), (0, 0), (0, 0), (0, MLA_QK_PAD - MLA_NOPE - MLA_ROPE)))
    wuq = wuq.reshape(Ld, MLA_Q_LORA, MLA_HEADS * MLA_QK_PAD)
    wukv = mla_w_ukv.astype(BF16).reshape(Ld, MLA_KV_LORA, MLA_HEADS, 2, MLA_NOPE)
    wukv = wukv.transpose(0, 1, 3, 2, 4).reshape(Ld, MLA_KV_LORA, 2 * MLA_HEADS * MLA_NOPE)
    wqk = jnp.concatenate([_block_diag(mlstm_w_q) * (ML_DIM ** -0.5), _block_diag(mlstm_w_k)],
                          axis=-1).astype(BF16)
    gbias = jnp.concatenate([jnp.zeros((Ld, G_OFF), F32), mlstm_i_bias.reshape(Ld, -1),
                             mlstm_f_bias.reshape(Ld, -1), jnp.zeros((Ld, LANE - G_OFF - GATE_W), F32)],
                            axis=-1).reshape(Ld, 1, LANE)
    fw_bd = _block_diag(fnet_w).astype(BF16)
    post_params = [rows(mlstm_head_norm), rows(mlstm_skip), w_out.astype(BF16), rows(ffn2_norm), wg2, wu2, wd2,
                   rows(ple_gate_norm), ple_w_gate.astype(BF16), ple_w_proj.astype(BF16),
                   rows(ple_post_norm)]
    g1, gmix, gq, gkv = rows(ffn1_norm), rows(mix_norm), rows(mla_q_norm), rows(mla_kv_norm)
    conv_w, conv_b, fnet_bias = mlstm_conv_w.astype(F32), rows(mlstm_conv_b), rows(fnet_b)
    p_flat = p.reshape(Ld, T, p.shape[-1])
    fin = final_norm.reshape(1, -1).astype(F32)
    flat = lambda a: a.reshape(T, a.shape[-1])

    h = x.reshape(T, D)
    for i in range(depth):
        h = _ffn(h, g1, wg1, wu1, wd1, i, t["tm"], t["tf"])
        qt, k, vt, mv, mo, fr, fi, xc, mqt, mk, gi, gt = _mixin(
            h.reshape(B, S, D), cos_t, sin_a, sin_b, gmix, win, gq, wuq, gkv, wukv, conv_w, conv_b, wqk,
            gbias, dft_chan, i, t["tmix"], qscale, t["chunk"])
        y_mla = _attention(qt, k, vt, t["tq"], t["tk"], t["attn_unroll"])
        hf, hb = _mlstm_scan(mqt, mk, mv, gi, gt, t["chunk"], t["bb"])
        y_fnet = _fnet(fr, fi, dft_m1, dft_tab3, fw_bd, fnet_bias, i, t["tc"])
        h = _post(h, flat(y_mla), flat(hf), flat(hb), flat(xc), flat(mo), flat(y_fnet), p_flat, ones_bd,
                  post_params, fin, i, t["tm"], t["tf"], final=(i == depth - 1))
    return h.reshape(B, S, D)
```

```python
import functools

import numpy as np
import jax
import jax.numpy as jnp
from jax import lax
from jax.experimental import pallas as pl
from jax.experimental.pallas import tpu as pltpu

F32 = jnp.float32
BF16 = jnp.bfloat16

EPS = 1e-6
ROPE_THETA = 10000.0
LOG2E = float(np.log2(np.e))
LANE = 128
SUBLANE = 8
MXU_DIM = 256
VMEM_LIMIT = 56 * 1024 * 1024

MLA_HEADS = 4
MLA_Q_LORA = 384
MLA_KV_LORA = 256
MLA_NOPE = 128
MLA_ROPE = 64
MLA_V = 128
MLA_QK_PAD = MXU_DIM
ML_HEADS = 4
ML_DIM = 64
ML_W = ML_HEADS * ML_DIM
ML_CONV = 5
FN_GROUPS = 4
FN_DIM = 64
FN_W = FN_GROUPS * FN_DIM
FN_N1 = 64
GATE_W = 4 * ML_HEADS
GATE_ROWS = 6 * ML_HEADS
ROW_GROUPS = 2

U_CQ = 0
U_CKV = U_CQ + MLA_Q_LORA
U_MX = U_CKV + MLA_KV_LORA
U_MV = U_MX + ML_W
U_MO = U_MV + ML_W
U_FIN = U_MO + ML_W
U_KR = U_FIN + FN_W
G_OFF = MLA_ROPE
U_TOT = U_KR + LANE


def _cparams(sem):
    return pltpu.CompilerParams(dimension_semantics=sem, vmem_limit_bytes=VMEM_LIMIT)


def _const_spec(shape):
    nd = len(shape)
    return pl.BlockSpec(shape, lambda *_: (0,) * nd, pipeline_mode=pl.Buffered(1))


def _layer_spec(a, layer):
    nd = a.ndim - 1
    return pl.BlockSpec((None,) + a.shape[1:], lambda *_: (layer,) + (0,) * nd,
                        pipeline_mode=pl.Buffered(1))


def _rms(x, g):
    ms = jnp.mean(x * x, axis=-1, keepdims=True)
    return x * lax.rsqrt(ms + EPS) * g


def _dot(a, b):
    return jnp.dot(a, b, preferred_element_type=F32)


def _swiglu_half(x, g_ref, wg_ref, wu_ref, wd_ref, acc_ref, tf):
    xn = _rms(x, g_ref[...]).astype(BF16)
    for c in range(wd_ref.shape[0] // tf):
        g = _dot(xn, wg_ref[:, c * tf:(c + 1) * tf])
        u = _dot(xn, wu_ref[:, c * tf:(c + 1) * tf])
        a = (g * jax.nn.sigmoid(g) * u).astype(BF16)
        part = _dot(a, wd_ref[c * tf:(c + 1) * tf, :])
        if c == 0:
            acc_ref[...] = part
        else:
            acc_ref[...] += part
    return 0.5 * acc_ref[...]


def _ffn_kernel(h_ref, g_ref, wg_ref, wu_ref, wd_ref, o_ref, acc_ref, *, tf):
    x = h_ref[...]
    o_ref[...] = x + _swiglu_half(x, g_ref, wg_ref, wu_ref, wd_ref, acc_ref, tf)


def _ffn(h, g, wg, wu, wd, layer, tm, tf):
    T, D = h.shape
    return pl.pallas_call(
        functools.partial(_ffn_kernel, tf=tf),
        grid=(T // tm,),
        in_specs=[pl.BlockSpec((tm, D), lambda i: (i, 0)),
                  _layer_spec(g, layer), _layer_spec(wg, layer), _layer_spec(wu, layer),
                  _layer_spec(wd, layer)],
        out_specs=pl.BlockSpec((tm, D), lambda i: (i, 0)),
        out_shape=jax.ShapeDtypeStruct((T, D), F32),
        scratch_shapes=[pltpu.VMEM((tm, D), F32)],
        compiler_params=_cparams(("parallel",)),
        name="ffn",
    )(h, g, wg, wu, wd)


def _rope_kernel(pos_ref, tab_ref, cos_ref, sina_ref, sinb_ref):
    ang = pos_ref[...].astype(F32) * tab_ref[0:1, :]
    c = jnp.cos(ang)
    s = jnp.sin(ang)
    cos_ref[...] = c * tab_ref[1:2, :]
    sina_ref[...] = s * tab_ref[2:3, :]
    sinb_ref[...] = s * tab_ref[3:4, :]


def _rope_tables(pos, tab, tm):
    T = pos.shape[0]
    out = jax.ShapeDtypeStruct((T, LANE), F32)
    spec = pl.BlockSpec((tm, LANE), lambda i: (i, 0))
    return pl.pallas_call(
        _rope_kernel,
        grid=(T // tm,),
        in_specs=[pl.BlockSpec((tm, 1), lambda i: (i, 0)), _const_spec(tab.shape)],
        out_specs=[spec, spec, spec],
        out_shape=[out, out, out],
        compiler_params=_cparams(("parallel",)),
        name="rope_tables",
    )(pos, tab)


def _rope(blk, cos_t, sin_a, sin_b):
    half = MLA_ROPE // 2
    return (blk * cos_t + pltpu.roll(blk, half, 1) * sin_a
            + pltpu.roll(blk, LANE - half, 1) * sin_b)


def _mixin_kernel(h_ref, hprev_ref, hnext_ref, cos_ref, sina_ref, sinb_ref, g_ref, win_ref, qn_ref, wuq_ref,
                  kvn_ref, wukv_ref, cw_ref, cb_ref, wqk_ref, gb_ref, dft_ref,
                  qt_ref, k_ref, vt_ref, mv_ref, mo_ref, fr_ref, fi_ref, xc_ref, mqt_ref, mk_ref, gi_ref,
                  gt_ref, ext_ref, gate_ref, *, qscale, chunk):
    tm = h_ref.shape[1]
    halo = SUBLANE
    for r0 in range(0, tm, tm // ROW_GROUPS):
        r = slice(r0, r0 + tm // ROW_GROUPS)
        x = h_ref[0, r, :]
        xn = _rms(x, g_ref[...]).astype(BF16)
        u = _dot(xn, win_ref[...])
        cos_t, sin_a, sin_b = cos_ref[0, r, :], sina_ref[0, r, :], sinb_ref[0, r, :]

        cq = _rms(u[:, U_CQ:U_CKV], qn_ref[...]).astype(BF16)
        q = _dot(cq, wuq_ref[...])
        for hd in range(MLA_HEADS):
            o = hd * MLA_QK_PAD
            qt_ref[0, hd, 0:MLA_NOPE, r] = (q[:, o:o + MLA_NOPE] * qscale).T.astype(BF16)
            qr = _rope(q[:, o + MLA_NOPE:o + MLA_QK_PAD], cos_t, sin_a, sin_b)
            qt_ref[0, hd, MLA_NOPE:MLA_QK_PAD, r] = (qr * qscale).T.astype(BF16)

        ckv = _rms(u[:, U_CKV:U_MX], kvn_ref[...]).astype(BF16)
        kv = _dot(ckv, wukv_ref[...])
        kr = _rope(u[:, U_KR:U_KR + LANE], cos_t, sin_a, sin_b).astype(BF16)
        for hd in range(MLA_HEADS):
            o = hd * MLA_QK_PAD
            k_ref[0, r, o:o + MLA_NOPE] = kv[:, hd * MLA_NOPE:(hd + 1) * MLA_NOPE].astype(BF16)
            k_ref[0, r, o + MLA_NOPE:o + MLA_QK_PAD] = kr
            vo = (MLA_HEADS + hd) * MLA_NOPE
            vt_ref[0, hd, :, r] = kv[:, vo:vo + MLA_V].T.astype(BF16)

        ext_ref[halo + r0:halo + r0 + tm // ROW_GROUPS, :] = u[:, U_MX:U_MV]
        mv_ref[0, :, r] = u[:, U_MV:U_MO].T.astype(BF16)
        mo_ref[0, r, :] = jax.nn.sigmoid(u[:, U_MO:U_FIN])
        gate_ref[r, :] = u[:, U_KR:U_TOT]

        f = _dot(u[:, U_FIN:U_KR].astype(BF16), dft_ref[...])
        fr_ref[0, r, :] = f[:, :FN_W].astype(BF16)
        fi_ref[0, r, :] = f[:, FN_W:].astype(BF16)

    i = pl.program_id(1)
    xh = jnp.concatenate([hprev_ref[0], hnext_ref[0]], axis=0)
    mh = _dot(_rms(xh, g_ref[...]).astype(BF16), win_ref[:, U_MX:U_MV])
    ext_ref[0:halo, :] = jnp.where(i > 0, mh[0:halo], 0.0)
    ext_ref[halo + tm:, :] = jnp.where(i < pl.num_programs(1) - 1, mh[halo:], 0.0)
    _mlstm_pre(ext_ref, gate_ref[...], cw_ref, cb_ref, wqk_ref, gb_ref,
               xc_ref, mqt_ref, mk_ref, gi_ref, gt_ref, chunk)


def _mixin(h3, cos_t, sin_a, sin_b, g, win, qn, wuq, kvn, wukv, cw, cb, wqk, gb, dft, layer, tm, qscale, chunk):
    B, S, D = h3.shape
    nblk = tm // SUBLANE
    last = S // SUBLANE - 1
    tok = lambda w: pl.BlockSpec((1, tm, w), lambda b, i: (b, i, 0))
    tcol = lambda w: pl.BlockSpec((1, w, tm), lambda b, i: (b, 0, i))
    outs = [
        (jax.ShapeDtypeStruct((B, MLA_HEADS, MLA_QK_PAD, S), BF16),
         pl.BlockSpec((1, MLA_HEADS, MLA_QK_PAD, tm), lambda b, i: (b, 0, 0, i))),
        (jax.ShapeDtypeStruct((B, S, MLA_HEADS * MLA_QK_PAD), BF16), tok(MLA_HEADS * MLA_QK_PAD)),
        (jax.ShapeDtypeStruct((B, MLA_HEADS, MLA_V, S), BF16),
         pl.BlockSpec((1, MLA_HEADS, MLA_V, tm), lambda b, i: (b, 0, 0, i))),
        (jax.ShapeDtypeStruct((B, ML_W, S), BF16), tcol(ML_W)),
        (jax.ShapeDtypeStruct((B, S, ML_W), F32), tok(ML_W)),
        (jax.ShapeDtypeStruct((B, S, FN_W), BF16), tok(FN_W)),
        (jax.ShapeDtypeStruct((B, S, FN_W), BF16), tok(FN_W)),
        (jax.ShapeDtypeStruct((B, S, ML_W), F32), tok(ML_W)),
        (jax.ShapeDtypeStruct((B, ML_W, S), BF16), tcol(ML_W)),
        (jax.ShapeDtypeStruct((B, S, ML_W), BF16), tok(ML_W)),
        (jax.ShapeDtypeStruct((B, S, LANE), F32), tok(LANE)),
        (jax.ShapeDtypeStruct((B, GATE_ROWS, S), F32), tcol(GATE_ROWS)),
    ]
    consts = [g, win, qn, wuq, kvn, wukv, cw, cb, wqk, gb]
    return pl.pallas_call(
        functools.partial(_mixin_kernel, qscale=qscale, chunk=chunk),
        grid=(B, S // tm),
        in_specs=([tok(D),
                   pl.BlockSpec((1, SUBLANE, D), lambda b, i: (b, jnp.maximum(i * nblk - 1, 0), 0)),
                   pl.BlockSpec((1, SUBLANE, D), lambda b, i: (b, jnp.minimum((i + 1) * nblk, last), 0)),
                   tok(LANE), tok(LANE), tok(LANE)] + [_layer_spec(c, layer) for c in consts]
                  + [_const_spec(dft.shape)]),
        out_specs=[o[1] for o in outs],
        out_shape=[o[0] for o in outs],
        scratch_shapes=[pltpu.VMEM((tm + 2 * SUBLANE, ML_W), F32), pltpu.VMEM((tm, LANE), F32)],
        compiler_params=_cparams(("parallel", "parallel")),
        name="mix_in",
    )(h3, h3, h3, cos_t, sin_a, sin_b, *consts, dft)


def _attn_kernel(qt_ref, k_ref, vt_ref, o_ref, acc_ref, s_ref, *, tk, unroll):
    qt = qt_ref[0, 0]
    S = k_ref.shape[1]
    n = S // tk
    tq = qt.shape[1]
    acc_ref[...] = jnp.zeros_like(acc_ref)

    def produce(j, slot):
        ks = pl.multiple_of(j * tk, tk)
        st = _dot(k_ref[0, pl.ds(ks, tk), :], qt)
        s_ref[slot] = st
        return jnp.max(st, axis=0, keepdims=True)

    def consume(j, slot, cmax, m_old, l_old):
        ks = pl.multiple_of(j * tk, tk)
        m_new = jnp.maximum(m_old, cmax)
        p = jnp.exp2(s_ref[slot] - m_new)
        alpha = jnp.exp2(m_old - m_new)
        l_new = alpha * l_old + jnp.sum(p, axis=0, keepdims=True)
        acc_ref[...] = alpha * acc_ref[...] + _dot(vt_ref[0, 0, :, pl.ds(ks, tk)], p.astype(BF16))
        return m_new, l_new

    def body(i, carry):
        cmax, m, l = carry
        for u in range(unroll):
            j = unroll * i + u
            nxt = produce(j + 1, (u + 1) % 2)
            m, l = consume(j, u % 2, cmax, m, l)
            cmax = nxt
        return cmax, m, l

    trips = (n - 1) // unroll
    carry = (produce(0, 0), jnp.full((1, tq), -jnp.inf, F32), jnp.zeros((1, tq), F32))
    cmax, m, l = lax.fori_loop(0, trips, body, carry)
    for j in range(unroll * trips, n):
        nxt = produce(j + 1, (j + 1) % 2) if j + 1 < n else None
        m, l = consume(j, j % 2, cmax, m, l)
        cmax = nxt
    o_ref[0] = (acc_ref[...] / l).T.astype(o_ref.dtype)


def _attention(qt, k, vt, tq, tk, unroll):
    B, S, _ = k.shape
    assert S % tq == 0 and S % tk == 0 and unroll % 2 == 0, "logit buffers alternate by chunk parity"
    return pl.pallas_call(
        functools.partial(_attn_kernel, tk=tk, unroll=unroll),
        grid=(B, MLA_HEADS, S // tq),
        in_specs=[pl.BlockSpec((1, 1, MLA_QK_PAD, tq), lambda b, h, i: (b, h, 0, i)),
                  pl.BlockSpec((1, S, MLA_QK_PAD), lambda b, h, i: (b, 0, h)),
                  pl.BlockSpec((1, 1, MLA_V, S), lambda b, h, i: (b, h, 0, 0))],
        out_specs=pl.BlockSpec((1, tq, MLA_V), lambda b, h, i: (b, i, h)),
        out_shape=jax.ShapeDtypeStruct((B, S, MLA_HEADS * MLA_V), BF16),
        scratch_shapes=[pltpu.VMEM((MLA_V, tq), F32), pltpu.VMEM((2, tk, tq), F32)],
        compiler_params=_cparams(("parallel", "parallel", "arbitrary")),
        name="mla_attention",
    )(qt, k, vt)


def _mlstm_pre(ext_ref, gate, cw_ref, cb_ref, wqk_ref, gb_ref, xc_ref, qt_ref, k_ref, gi_ref, gt_ref, chunk):
    halo = SUBLANE
    ts = ext_ref.shape[0] - 2 * halo
    pad = ML_CONV // 2
    xc = cb_ref[...] + cw_ref[0:1, :] * ext_ref[halo - pad:halo - pad + ts, :]
    for j in range(1, ML_CONV):
        o = halo - pad + j
        xc = xc + cw_ref[j:j + 1, :] * ext_ref[o:o + ts, :]
    xc = xc * jax.nn.sigmoid(xc)
    xc_ref[0] = xc
    qk = _dot(xc.astype(BF16), wqk_ref[...])
    qt_ref[0] = qk[:, :ML_W].T.astype(BF16)
    k_ref[0] = qk[:, ML_W:].astype(BF16)
    g_t = (gate + gb_ref[...]).T[G_OFF:G_OFF + GATE_W, :]
    lf_t = jnp.minimum(g_t, 0.0) - jnp.log(1.0 + jnp.exp(-jnp.abs(g_t)))
    p1 = lf_t.astype(BF16)
    r1 = lf_t - p1.astype(F32)
    p2 = r1.astype(BF16)
    p3 = (r1 - p2.astype(F32)).astype(BF16)
    r = lax.broadcasted_iota(jnp.int32, (chunk, chunk), 0)
    c = lax.broadcasted_iota(jnp.int32, (chunk, chunk), 1)
    tri_f = (r <= c).astype(BF16)
    tri_b = (r >= c).astype(BF16)
    sub = lax.broadcasted_iota(jnp.int32, (GATE_W, chunk), 0)
    cols = []
    for k in range(ts // chunk):
        sl = slice(k * chunk, (k + 1) * chunk)
        cum_f = _dot(p1[:, sl], tri_f) + _dot(p2[:, sl], tri_f) + _dot(p3[:, sl], tri_f)
        cum_b = _dot(p1[:, sl], tri_b) + _dot(p2[:, sl], tri_b) + _dot(p3[:, sl], tri_b)
        cols.append(jnp.where(sub < 2 * ML_HEADS, g_t[:, sl], jnp.where(sub < 3 * ML_HEADS, cum_f, cum_b)))
    gt = jnp.concatenate(cols, axis=1) * LOG2E
    gt = jnp.concatenate([gt, gt[0:2 * ML_HEADS] - gt[2 * ML_HEADS:]], axis=0)
    gt_ref[0] = gt
    gi_ref[0] = jnp.concatenate([gt, jnp.zeros((LANE - GATE_ROWS, ts), F32)], axis=0).T


def _mlstm_chunk(blocks, c_ref, n_ref, m_ref):
    nslot = len(blocks)
    bb = nslot // 2
    L = blocks[0][1].shape[0]
    H, dh = ML_HEADS, ML_DIM
    key = lax.broadcasted_iota(jnp.int32, (L, L), 0)
    qry = lax.broadcasted_iota(jnp.int32, (L, L), 1)
    lane_w = lax.broadcasted_iota(jnp.int32, (1, ML_W), 1)
    row_w = lax.broadcasted_iota(jnp.int32, (ML_W, 1), 0)
    hlanes = [(lane_w // dh) == hd for hd in range(H)]
    hrows = [(row_w // dh) == hd for hd in range(H)]
    zero_k = jnp.zeros_like(blocks[0][1])
    zero_v = jnp.zeros_like(blocks[0][2])
    sel8 = (lax.broadcasted_iota(jnp.int32, (SUBLANE, ML_W), 0)
            == lax.broadcasted_iota(jnp.int32, (SUBLANE, ML_W), 1) // dh)
    probs_of = [(si, hd) for si in range(nslot) for hd in range(H)]
    direction = lambda si: si // bb
    gate = lambda kind, si, hd: kind * 2 * H + direction(si) * H + hd
    edge = lambda si: L - 1 if direction(si) == 0 else 0

    QT, K, VT, GI, GT = range(5)
    inter_t, qn_t = [], []
    for si, blk in enumerate(blocks):
        inter_t.append(_dot(c_ref[si].astype(BF16), blk[QT]))
        qn_t.append(_dot(n_ref[si].astype(BF16), blk[QT]))

    def rows_of(kind):
        return jnp.stack([blocks[si][GT][gate(kind, si, hd):gate(kind, si, hd) + 1, :]
                          for si, hd in probs_of])

    m_old = jnp.stack([m_ref[si, hd:hd + 1, 0:1] for si, hd in probs_of])
    bq = rows_of(1)
    imb_r = rows_of(2)
    imb_c = jnp.stack([blocks[si][GI][:, gate(2, si, hd):gate(2, si, hd) + 1]
                       for si, hd in probs_of])
    b_edge = jnp.stack([blocks[si][GT][gate(1, si, hd):gate(1, si, hd) + 1, edge(si):edge(si) + 1]
                        for si, hd in probs_of])
    qn_r = jnp.stack([qn_t[si][hd:hd + 1, :] for si, hd in probs_of])
    qk = jnp.stack([_dot(jnp.where(hlanes[hd], blocks[si][K], zero_k), blocks[si][QT])
                    for si, hd in probs_of])

    raw = imb_c + bq
    half = bb * H
    dmat = jnp.concatenate([jnp.where(key <= qry, raw[:half], -jnp.inf),
                            jnp.where(key >= qry, raw[half:], -jnp.inf)], axis=0)
    m_inter = bq + m_old
    m_t = jnp.maximum(m_inter, jnp.max(dmat, axis=1, keepdims=True))
    w_state = jnp.exp2(m_inter - m_t)
    s = qk * jnp.exp2(dmat - m_t)
    denom = jnp.sum(s, axis=1, keepdims=True) + w_state * qn_r
    scale = 1.0 / jnp.maximum(jnp.abs(denom), jnp.exp2(-m_t))
    wscale = w_state * scale
    probs = s.astype(BF16)

    lw = b_edge + imb_r
    m_new = jnp.maximum(b_edge + m_old, jnp.max(lw, axis=-1, keepdims=True))
    ws = jnp.exp2(lw - m_new)
    dec = jnp.exp2(b_edge + m_old - m_new)

    def per_head_rows(rows, si):
        return jnp.concatenate([jnp.broadcast_to(rows[si * H + hd], (dh, L)) for hd in range(H)], axis=0)

    def per_head_lanes(vals, si):
        e = vals[si * H + H - 1]
        for hd in reversed(range(H - 1)):
            e = jnp.where(hlanes[hd], vals[si * H + hd], e)
        return e

    blockdiag = (row_w // dh) == (lane_w // dh)
    outs = []
    for si, blk in enumerate(blocks):
        k, vt = blk[K], blk[VT]
        v_heads = jnp.concatenate([jnp.where(hrows[hd], vt, zero_v) for hd in range(H)], axis=1)
        intra_t = _dot(v_heads, jnp.concatenate([probs[si * H + hd] for hd in range(H)], axis=0))
        out_t = intra_t * per_head_rows(scale, si) + inter_t[si] * per_head_rows(wscale, si)
        outs.append(out_t.T)

        dec_l = per_head_lanes(dec, si)
        vw = (vt.astype(F32) * per_head_rows(ws, si)).astype(BF16)
        c_new = c_ref[si] * dec_l + _dot(vw, k)
        c_ref[si] = jnp.where(blockdiag, c_new, 0.0)
        ws8 = jnp.concatenate([ws[si * H + hd] for hd in range(H)]
                              + [jnp.zeros((SUBLANE - H, L), F32)], axis=0).astype(BF16)
        n_ref[si] = n_ref[si] * dec_l + jnp.where(sel8, _dot(ws8, k), 0.0)
        m_ref[si, 0:H, :] = jnp.concatenate(
            [jnp.broadcast_to(m_new[si * H + hd], (1, LANE)) for hd in range(H)], axis=0)
    return outs


def _mlscan_kernel(qf_ref, kf_ref, vf_ref, gif_ref, gtf_ref, qb_ref, kb_ref, vb_ref, gib_ref, gtb_ref,
                   hf_ref, hb_ref, c_ref, n_ref, m_ref, *, bb):
    @pl.when(pl.program_id(1) == 0)
    def _():
        c_ref[...] = jnp.zeros_like(c_ref)
        n_ref[...] = jnp.zeros_like(n_ref)
        m_ref[...] = jnp.zeros_like(m_ref)

    fwd = [(qf_ref[b], kf_ref[b], vf_ref[b], gif_ref[b], gtf_ref[b]) for b in range(bb)]
    bwd = [(qb_ref[b], kb_ref[b], vb_ref[b], gib_ref[b], gtb_ref[b]) for b in range(bb)]
    outs = _mlstm_chunk(fwd + bwd, c_ref, n_ref, m_ref)
    for b in range(bb):
        hf_ref[b] = outs[b]
        hb_ref[b] = outs[bb + b]


def _mlstm_scan(qt, k, vt, gi, gt, L, bb):
    B, S, W = k.shape
    nc = S // L
    fw = lambda b, c: c
    bw = lambda b, c: nc - 1 - c
    def specs(cm):
        return [pl.BlockSpec((bb, W, L), lambda b, c: (b, 0, cm(b, c))),
                pl.BlockSpec((bb, L, W), lambda b, c: (b, cm(b, c), 0)),
                pl.BlockSpec((bb, W, L), lambda b, c: (b, 0, cm(b, c))),
                pl.BlockSpec((bb, L, LANE), lambda b, c: (b, cm(b, c), 0)),
                pl.BlockSpec((bb, GATE_ROWS, L), lambda b, c: (b, 0, cm(b, c)))]
    out = jax.ShapeDtypeStruct((B, S, W), F32)
    return pl.pallas_call(
        functools.partial(_mlscan_kernel, bb=bb),
        grid=(B // bb, nc),
        in_specs=specs(fw) + specs(bw),
        out_specs=[pl.BlockSpec((bb, L, W), lambda b, c: (b, c, 0)),
                   pl.BlockSpec((bb, L, W), lambda b, c: (b, nc - 1 - c, 0))],
        out_shape=[out, out],
        scratch_shapes=[pltpu.VMEM((2 * bb, W, W), F32), pltpu.VMEM((2 * bb, SUBLANE, W), F32),
                        pltpu.VMEM((2 * bb, SUBLANE, LANE), F32)],
        compiler_params=_cparams(("parallel", "arbitrary")),
        name="mlstm_scan",
    )(qt, k, vt, gi, gt, qt, k, vt, gi, gt)


def _fnet1_kernel(wr_ref, wi_ref, m1_ref, a_ref):
    n1 = wr_ref.shape[1]
    a = _dot(m1_ref[:, 0:n1], wr_ref[0]) + _dot(m1_ref[:, n1:], wi_ref[0])
    a_ref[0, 0] = a[0:n1].astype(BF16)
    a_ref[0, 1] = a[n1:].astype(BF16)


def _fnet3_kernel(a_ref, tab_ref, w_ref, b_ref, o_ref):
    for j in range(SUBLANE):
        z = jnp.concatenate([a_ref[0, 0, j], a_ref[0, 1, j]], axis=0)
        y = _dot(tab_ref[j], z)
        o_ref[0, :, j, :] = _dot(y.astype(BF16), w_ref[...]) + b_ref[...]


def _fnet(fr, fi, m1, tab3, wbd, bias, layer, tc):
    B, S, W = fr.shape
    n1 = FN_N1
    n2 = S // n1
    cols = n2 * W
    wr = fr.reshape(B, n1, cols)
    wi = fi.reshape(B, n1, cols)
    a = pl.pallas_call(
        _fnet1_kernel,
        grid=(B, cols // tc),
        in_specs=[pl.BlockSpec((1, n1, tc), lambda b, i: (b, 0, i)),
                  pl.BlockSpec((1, n1, tc), lambda b, i: (b, 0, i)),
                  _const_spec(m1.shape)],
        out_specs=pl.BlockSpec((1, 2, n1, tc), lambda b, i: (b, 0, 0, i)),
        out_shape=jax.ShapeDtypeStruct((B, 2, n1, cols), BF16),
        compiler_params=_cparams(("parallel", "parallel")),
        name="fnet_dft1",
    )(wr, wi, m1)
    a = a.reshape(B, 2, n1, n2, W)
    y = pl.pallas_call(
        _fnet3_kernel,
        grid=(B, n1 // SUBLANE),
        in_specs=[pl.BlockSpec((1, 2, SUBLANE, n2, W), lambda b, i: (b, 0, i, 0, 0)),
                  pl.BlockSpec((SUBLANE, n2, 2 * n2), lambda b, i: (i, 0, 0)),
                  _layer_spec(wbd, layer), _layer_spec(bias, layer)],
        out_specs=pl.BlockSpec((1, n2, SUBLANE, W), lambda b, i: (b, 0, i, 0)),
        out_shape=jax.ShapeDtypeStruct((B, n2, n1, W), F32),
        compiler_params=_cparams(("parallel", "parallel")),
        name="fnet_dft2",
    )(a, tab3, wbd, bias)
    return y.reshape(B, S, W)


def _post_kernel(h_ref, ya_ref, hf_ref, hb_ref, xc_ref, mo_ref, yf_ref, p_ref,
                 hn_ref, sk_ref, ones_ref, wo_ref, g2_ref, wg2_ref, wu2_ref, wd_ref,
                 gn_ref, wg_ref, wp_ref, pn_ref, fn_ref, o_ref, acc_ref, *, final, tf):
    tm = h_ref.shape[0]
    a_w = ya_ref.shape[1]
    m_w = a_w + ML_W
    groups = [slice(r, r + tm // ROW_GROUPS) for r in range(0, tm, tm // ROW_GROUPS)]

    def project(r):
        hm = hf_ref[r, :] + hb_ref[r, :]
        sq = hm * hm
        hi = sq.astype(BF16)
        lo = (sq - hi.astype(F32)).astype(BF16)
        ms = (_dot(hi, ones_ref[...]) + _dot(lo, ones_ref[...])) * (1.0 / ML_DIM)
        ym = (hm * lax.rsqrt(ms + EPS) * hn_ref[...] + sk_ref[...] * xc_ref[r, :]) * mo_ref[r, :]
        return (h_ref[r, :] + _dot(ya_ref[r, :], wo_ref[0:a_w, :])
                + _dot(ym.astype(BF16), wo_ref[a_w:m_w, :])
                + _dot(yf_ref[r, :].astype(BF16), wo_ref[m_w:, :]))

    h1 = jnp.concatenate([project(r) for r in groups], axis=0)
    h2 = h1 + _swiglu_half(h1, g2_ref, wg2_ref, wu2_ref, wd_ref, acc_ref, tf)

    for r in groups:
        h2r = h2[r, :]
        e = _rms(_dot(p_ref[r, :].astype(BF16), wp_ref[...]), pn_ref[...])
        gate = jax.nn.sigmoid(_dot(_rms(h2r, gn_ref[...]).astype(BF16), wg_ref[...]))
        h3 = h2r + gate * e
        if final:
            h3 = _rms(h3, fn_ref[...])
        o_ref[r, :] = h3


def _post(h, ya, hf, hb, xc, mo, yf, p, ones_bd, params, final_norm, layer, tm, tf, final):
    T, D = h.shape
    tok = lambda w: pl.BlockSpec((tm, w), lambda i: (i, 0))
    acts = [h, ya, hf, hb, xc, mo, yf]
    hn, sk, wo, g2, wg2, wu2, wd, gn, wg, wp, pn = params
    consts = [hn, sk, ones_bd, wo, g2, wg2, wu2, wd, gn, wg, wp, pn, final_norm]
    shared = (2, len(consts) - 1)
    specs = [_const_spec(c.shape) if j in shared else _layer_spec(c, layer) for j, c in enumerate(consts)]
    return pl.pallas_call(
        functools.partial(_post_kernel, final=final, tf=tf),
        grid=(T // tm,),
        in_specs=([tok(a.shape[1]) for a in acts]
                  + [pl.BlockSpec((None, tm, p.shape[2]), lambda i: (layer, i, 0))] + specs),
        out_specs=tok(D),
        out_shape=jax.ShapeDtypeStruct((T, D), F32),
        scratch_shapes=[pltpu.VMEM((tm, D), F32)],
        compiler_params=_cparams(("parallel",)),
        name="post",
    )(*acts, p, *consts)


def _block_diag(w):
    *lead, G, a, b = w.shape
    eye = jnp.eye(G, dtype=w.dtype)
    return (eye[:, None, :, None] * w[..., :, :, None, :]).reshape(*lead, G * a, G * b)


def _dft_tables(S):
    n1, n2 = FN_N1, S // FN_N1
    c = np.arange(FN_DIM)
    ang = 2.0 * np.pi * np.outer(c, c) / FN_DIM
    eye = np.eye(FN_GROUPS)
    chan = np.concatenate([np.kron(eye, np.cos(ang)), -np.kron(eye, np.sin(ang))], axis=1) / np.sqrt(FN_DIM)
    k1 = np.arange(n1)
    a1 = 2.0 * np.pi * np.outer(k1, k1) / n1
    m1 = np.block([[np.cos(a1), np.sin(a1)], [-np.sin(a1), np.cos(a1)]])
    k = k1[:, None, None] + n1 * np.arange(n2)[None, :, None]
    s2 = np.arange(n2)[None, None, :]
    a3 = 2.0 * np.pi * ((k * s2) % S) / S
    tab3 = np.concatenate([np.cos(a3), np.sin(a3)], axis=-1) / np.sqrt(S)
    bf = lambda t: jnp.asarray(t, dtype=F32).astype(BF16)
    return bf(chan), bf(m1), bf(tab3)


def _rope_consts():
    half = MLA_ROPE // 2
    inv = 1.0 / (ROPE_THETA ** (jnp.arange(0, MLA_ROPE, 2, dtype=F32) / MLA_ROPE))
    z = jnp.zeros((half,), F32)
    o = jnp.ones((half,), F32)
    rows = [jnp.concatenate([inv, inv, z, z]), jnp.concatenate([o, o, z, z]),
            jnp.concatenate([z, o, z, z]), jnp.concatenate([-o, z, z, z])]
    rows += [jnp.zeros((LANE,), F32)] * (SUBLANE - len(rows))
    return jnp.stack(rows)


def _tiles(B, S):
    T = B * S
    return dict(tm=min(512, T), tf=MXU_DIM, tmix=min(1024, S), tq=min(2048, S), tk=min(1024, S), attn_unroll=2,
                chunk=min(256, S), bb=2 if B % 2 == 0 else 1,
                tc=min(4096, (S // FN_N1) * FN_W),
                trope=min(1024, T))


def kernel(x, p, positions, ffn1_norm, ffn1_w_gate, ffn1_w_up, ffn1_w_down, mix_norm, w_in, mla_q_norm, mla_w_uq, mla_kv_norm, mla_w_ukv, mlstm_conv_w, mlstm_conv_b, mlstm_w_q, mlstm_w_k, mlstm_i_bias, mlstm_f_bias, mlstm_head_norm, mlstm_skip, fnet_w, fnet_b, w_out, ffn2_norm, ffn2_w_gate, ffn2_w_up, ffn2_w_down, ple_gate_norm, ple_w_gate, ple_w_proj, ple_post_norm, final_norm):
    B, S, D = x.shape
    depth = p.shape[0]
    T = B * S
    t = _tiles(B, S)
    Ld = depth
    rows = lambda a: a.reshape(Ld, 1, -1).astype(F32)

    tab = _rope_consts()
    cos_t, sin_a, sin_b = [a.reshape(B, S, LANE) for a in
                           _rope_tables(positions.reshape(T, 1).astype(jnp.int32), tab, t["trope"])]
    dft_chan, dft_m1, dft_tab3 = _dft_tables(S)
    ones_bd = _block_diag(jnp.ones((ML_HEADS, ML_DIM, ML_DIM), BF16))
    qscale = float((MLA_NOPE + MLA_ROPE) ** -0.5 * LOG2E)

    wg1, wu1, wd1 = ffn1_w_gate.astype(BF16), ffn1_w_up.astype(BF16), ffn1_w_down.astype(BF16)
    wg2, wu2, wd2 = ffn2_w_gate.astype(BF16), ffn2_w_up.astype(BF16), ffn2_w_down.astype(BF16)
    o_kr = MLA_Q_LORA + MLA_KV_LORA
    o_mx = o_kr + MLA_ROPE
    o_g = o_mx + 3 * ML_W
    o_f = o_g + GATE_W
    wi = w_in.astype(BF16)
    zpad = lambda n: jnp.zeros((Ld, D, n), BF16)
    win = jnp.concatenate([wi[..., :o_kr], wi[..., o_mx:o_g], wi[..., o_f:], wi[..., o_kr:o_mx],
                           wi[..., o_g:o_f], zpad(LANE - MLA_ROPE - GATE_W)], axis=-1)
    wuq = jnp.pad(mla_w_uq.astype(BF16).reshape(Ld, MLA_Q_LORA, MLA_HEADS, MLA_NOPE + MLA_ROPE),
                  ((0, 0), (0, 0), (0, 0), (0, MLA_QK_PAD - MLA_NOPE - MLA_ROPE)))
    wuq = wuq.reshape(Ld, MLA_Q_LORA, MLA_HEADS * MLA_QK_PAD)
    wukv = mla_w_ukv.astype(BF16).reshape(Ld, MLA_KV_LORA, MLA_HEADS, 2, MLA_NOPE)
    wukv = wukv.transpose(0, 1, 3, 2, 4).reshape(Ld, MLA_KV_LORA, 2 * MLA_HEADS * MLA_NOPE)
    wqk = jnp.concatenate([_block_diag(mlstm_w_q) * (ML_DIM ** -0.5), _block_diag(mlstm_w_k)],
                          axis=-1).astype(BF16)
    gbias = jnp.concatenate([jnp.zeros((Ld, G_OFF), F32), mlstm_i_bias.reshape(Ld, -1),
                             mlstm_f_bias.reshape(Ld, -1), jnp.zeros((Ld, LANE - G_OFF - GATE_W), F32)],
                            axis=-1).reshape(Ld, 1, LANE)
    fw_bd = _block_diag(fnet_w).astype(BF16)
    post_params = [rows(mlstm_head_norm), rows(mlstm_skip), w_out.astype(BF16), rows(ffn2_norm), wg2, wu2, wd2,
                   rows(ple_gate_norm), ple_w_gate.astype(BF16), ple_w_proj.astype(BF16),
                   rows(ple_post_norm)]
    g1, gmix, gq, gkv = rows(ffn1_norm), rows(mix_norm), rows(mla_q_norm), rows(mla_kv_norm)
    conv_w, conv_b, fnet_bias = mlstm_conv_w.astype(F32), rows(mlstm_conv_b), rows(fnet_b)
    p_flat = p.reshape(Ld, T, p.shape[-1])
    fin = final_norm.reshape(1, -1).astype(F32)
    flat = lambda a: a.reshape(T, a.shape[-1])

    h = x.reshape(T, D)
    for i in range(depth):
        h = _ffn(h, g1, wg1, wu1, wd1, i, 2 * t["tm"], t["tf"])
        qt, k, vt, mv, mo, fr, fi, xc, mqt, mk, gi, gt = _mixin(
            h.reshape(B, S, D), cos_t, sin_a, sin_b, gmix, win, gq, wuq, gkv, wukv, conv_w, conv_b, wqk,
            gbias, dft_chan, i, t["tmix"], qscale, t["chunk"])
        y_mla = _attention(qt, k, vt, t["tq"], t["tk"], t["attn_unroll"])
        hf, hb = _mlstm_scan(mqt, mk, mv, gi, gt, t["chunk"], t["bb"])
        y_fnet = _fnet(fr, fi, dft_m1, dft_tab3, fw_bd, fnet_bias, i, t["tc"])
        h = _post(h, flat(y_mla), flat(hf), flat(hb), flat(xc), flat(mo), flat(y_fnet), p_flat, ones_bd,
                  post_params, fin, i, t["tm"], t["tf"], final=(i == depth - 1))
    return h.reshape(B, S, D)
```

```python
import functools

import numpy as np
import jax
import jax.numpy as jnp
from jax import lax
from jax.experimental import pallas as pl
from jax.experimental.pallas import tpu as pltpu

F32 = jnp.float32
BF16 = jnp.bfloat16

EPS = 1e-6
ROPE_THETA = 10000.0
LOG2E = float(np.log2(np.e))
LANE = 128
SUBLANE = 8
MXU_DIM = 256
VMEM_LIMIT = 56 * 1024 * 1024

MLA_HEADS = 4
MLA_Q_LORA = 384
MLA_KV_LORA = 256
MLA_NOPE = 128
MLA_ROPE = 64
MLA_V = 128
MLA_QK_PAD = MXU_DIM
ML_HEADS = 4
ML_DIM = 64
ML_W = ML_HEADS * ML_DIM
ML_CONV = 5
FN_GROUPS = 4
FN_DIM = 64
FN_W = FN_GROUPS * FN_DIM
FN_N1 = 64
GATE_W = 4 * ML_HEADS
GATE_ROWS = 6 * ML_HEADS
ROW_GROUPS = 2

U_CQ = 0
U_CKV = U_CQ + MLA_Q_LORA
U_MX = U_CKV + MLA_KV_LORA
U_MV = U_MX + ML_W
U_MO = U_MV + ML_W
U_FIN = U_MO + ML_W
U_KR = U_FIN + FN_W
G_OFF = MLA_ROPE
U_TOT = U_KR + LANE


def _cparams(sem):
    return pltpu.CompilerParams(dimension_semantics=sem, vmem_limit_bytes=VMEM_LIMIT)


def _const_spec(shape):
    nd = len(shape)
    return pl.BlockSpec(shape, lambda *_: (0,) * nd, pipeline_mode=pl.Buffered(1))


def _layer_spec(a, layer):
    nd = a.ndim - 1
    return pl.BlockSpec((None,) + a.shape[1:], lambda *_: (layer,) + (0,) * nd,
                        pipeline_mode=pl.Buffered(1))


def _rms(x, g):
    ms = jnp.mean(x * x, axis=-1, keepdims=True)
    return x * lax.rsqrt(ms + EPS) * g


def _dot(a, b):
    return jnp.dot(a, b, preferred_element_type=F32)


def _swiglu_half(x, g_ref, wg_ref, wu_ref, wd_ref, acc_ref, tf):
    xn = _rms(x, g_ref[...]).astype(BF16)
    for c in range(wd_ref.shape[0] // tf):
        g = _dot(xn, wg_ref[:, c * tf:(c + 1) * tf])
        u = _dot(xn, wu_ref[:, c * tf:(c + 1) * tf])
        a = (g * jax.nn.sigmoid(g) * u).astype(BF16)
        part = _dot(a, wd_ref[c * tf:(c + 1) * tf, :])
        if c == 0:
            acc_ref[...] = part
        else:
            acc_ref[...] += part
    return 0.5 * acc_ref[...]


def _ffn_kernel(h_ref, g_ref, wg_ref, wu_ref, wd_ref, o_ref, acc_ref, *, tf):
    x = h_ref[...]
    o_ref[...] = x + _swiglu_half(x, g_ref, wg_ref, wu_ref, wd_ref, acc_ref, tf)


def _ffn(h, g, wg, wu, wd, layer, tm, tf):
    T, D = h.shape
    return pl.pallas_call(
        functools.partial(_ffn_kernel, tf=tf),
        grid=(T // tm,),
        in_specs=[pl.BlockSpec((tm, D), lambda i: (i, 0)),
                  _layer_spec(g, layer), _layer_spec(wg, layer), _layer_spec(wu, layer),
                  _layer_spec(wd, layer)],
        out_specs=pl.BlockSpec((tm, D), lambda i: (i, 0)),
        out_shape=jax.ShapeDtypeStruct((T, D), F32),
        scratch_shapes=[pltpu.VMEM((tm, D), F32)],
        compiler_params=_cparams(("parallel",)),
        name="ffn",
    )(h, g, wg, wu, wd)


def _rope_kernel(pos_ref, tab_ref, cos_ref, sina_ref, sinb_ref):
    ang = pos_ref[...].astype(F32) * tab_ref[0:1, :]
    c = jnp.cos(ang)
    s = jnp.sin(ang)
    cos_ref[...] = c * tab_ref[1:2, :]
    sina_ref[...] = s * tab_ref[2:3, :]
    sinb_ref[...] = s * tab_ref[3:4, :]


def _rope_tables(pos, tab, tm):
    T = pos.shape[0]
    out = jax.ShapeDtypeStruct((T, LANE), F32)
    spec = pl.BlockSpec((tm, LANE), lambda i: (i, 0))
    return pl.pallas_call(
        _rope_kernel,
        grid=(T // tm,),
        in_specs=[pl.BlockSpec((tm, 1), lambda i: (i, 0)), _const_spec(tab.shape)],
        out_specs=[spec, spec, spec],
        out_shape=[out, out, out],
        compiler_params=_cparams(("parallel",)),
        name="rope_tables",
    )(pos, tab)


def _rope(blk, cos_t, sin_a, sin_b):
    half = MLA_ROPE // 2
    return (blk * cos_t + pltpu.roll(blk, half, 1) * sin_a
            + pltpu.roll(blk, LANE - half, 1) * sin_b)


def _mixin_kernel(h_ref, hprev_ref, hnext_ref, cos_ref, sina_ref, sinb_ref, g_ref, win_ref, qn_ref, wuq_ref,
                  kvn_ref, wukv_ref, cw_ref, cb_ref, wqk_ref, gb_ref, dft_ref,
                  qt_ref, k_ref, vt_ref, mv_ref, mo_ref, fr_ref, fi_ref, xc_ref, mqt_ref, mk_ref, gi_ref,
                  gt_ref, ext_ref, gate_ref, *, qscale, chunk):
    tm = h_ref.shape[1]
    halo = SUBLANE
    for r0 in range(0, tm, tm // ROW_GROUPS):
        r = slice(r0, r0 + tm // ROW_GROUPS)
        x = h_ref[0, r, :]
        xn = _rms(x, g_ref[...]).astype(BF16)
        u = _dot(xn, win_ref[...])
        cos_t, sin_a, sin_b = cos_ref[0, r, :], sina_ref[0, r, :], sinb_ref[0, r, :]

        cq = _rms(u[:, U_CQ:U_CKV], qn_ref[...]).astype(BF16)
        q = _dot(cq, wuq_ref[...])
        for hd in range(MLA_HEADS):
            o = hd * MLA_QK_PAD
            qt_ref[0, hd, 0:MLA_NOPE, r] = (q[:, o:o + MLA_NOPE] * qscale).T.astype(BF16)
            qr = _rope(q[:, o + MLA_NOPE:o + MLA_QK_PAD], cos_t, sin_a, sin_b)
            qt_ref[0, hd, MLA_NOPE:MLA_QK_PAD, r] = (qr * qscale).T.astype(BF16)

        ckv = _rms(u[:, U_CKV:U_MX], kvn_ref[...]).astype(BF16)
        kv = _dot(ckv, wukv_ref[...])
        kr = _rope(u[:, U_KR:U_KR + LANE], cos_t, sin_a, sin_b).astype(BF16)
        for hd in range(MLA_HEADS):
            o = hd * MLA_QK_PAD
            k_ref[0, r, o:o + MLA_NOPE] = kv[:, hd * MLA_NOPE:(hd + 1) * MLA_NOPE].astype(BF16)
            k_ref[0, r, o + MLA_NOPE:o + MLA_QK_PAD] = kr
            vo = (MLA_HEADS + hd) * MLA_NOPE
            vt_ref[0, hd, :, r] = kv[:, vo:vo + MLA_V].T.astype(BF16)

        ext_ref[halo + r0:halo + r0 + tm // ROW_GROUPS, :] = u[:, U_MX:U_MV]
        mv_ref[0, :, r] = u[:, U_MV:U_MO].T.astype(BF16)
        mo_ref[0, r, :] = jax.nn.sigmoid(u[:, U_MO:U_FIN])
        gate_ref[r, :] = u[:, U_KR:U_TOT]

        f = _dot(u[:, U_FIN:U_KR].astype(BF16), dft_ref[...])
        fr_ref[0, r, :] = f[:, :FN_W].astype(BF16)
        fi_ref[0, r, :] = f[:, FN_W:].astype(BF16)

    i = pl.program_id(1)
    xh = jnp.concatenate([hprev_ref[0], hnext_ref[0]], axis=0)
    mh = _dot(_rms(xh, g_ref[...]).astype(BF16), win_ref[:, U_MX:U_MV])
    ext_ref[0:halo, :] = jnp.where(i > 0, mh[0:halo], 0.0)
    ext_ref[halo + tm:, :] = jnp.where(i < pl.num_programs(1) - 1, mh[halo:], 0.0)
    _mlstm_pre(ext_ref, gate_ref[...], cw_ref, cb_ref, wqk_ref, gb_ref,
               xc_ref, mqt_ref, mk_ref, gi_ref, gt_ref, chunk)


def _mixin(h3, cos_t, sin_a, sin_b, g, win, qn, wuq, kvn, wukv, cw, cb, wqk, gb, dft, layer, tm, qscale, chunk):
    B, S, D = h3.shape
    nblk = tm // SUBLANE
    last = S // SUBLANE - 1
    tok = lambda w: pl.BlockSpec((1, tm, w), lambda b, i: (b, i, 0))
    tcol = lambda w: pl.BlockSpec((1, w, tm), lambda b, i: (b, 0, i))
    outs = [
        (jax.ShapeDtypeStruct((B, MLA_HEADS, MLA_QK_PAD, S), BF16),
         pl.BlockSpec((1, MLA_HEADS, MLA_QK_PAD, tm), lambda b, i: (b, 0, 0, i))),
        (jax.ShapeDtypeStruct((B, S, MLA_HEADS * MLA_QK_PAD), BF16), tok(MLA_HEADS * MLA_QK_PAD)),
        (jax.ShapeDtypeStruct((B, MLA_HEADS, MLA_V, S), BF16),
         pl.BlockSpec((1, MLA_HEADS, MLA_V, tm), lambda b, i: (b, 0, 0, i))),
        (jax.ShapeDtypeStruct((B, ML_W, S), BF16), tcol(ML_W)),
        (jax.ShapeDtypeStruct((B, S, ML_W), F32), tok(ML_W)),
        (jax.ShapeDtypeStruct((B, S, FN_W), BF16), tok(FN_W)),
        (jax.ShapeDtypeStruct((B, S, FN_W), BF16), tok(FN_W)),
        (jax.ShapeDtypeStruct((B, S, ML_W), F32), tok(ML_W)),
        (jax.ShapeDtypeStruct((B, ML_W, S), BF16), tcol(ML_W)),
        (jax.ShapeDtypeStruct((B, S, ML_W), BF16), tok(ML_W)),
        (jax.ShapeDtypeStruct((B, S, LANE), F32), tok(LANE)),
        (jax.ShapeDtypeStruct((B, GATE_ROWS, S), F32), tcol(GATE_ROWS)),
    ]
    consts = [g, win, qn, wuq, kvn, wukv, cw, cb, wqk, gb]
    return pl.pallas_call(
        functools.partial(_mixin_kernel, qscale=qscale, chunk=chunk),
        grid=(B, S // tm),
        in_specs=([tok(D),
                   pl.BlockSpec((1, SUBLANE, D), lambda b, i: (b, jnp.maximum(i * nblk - 1, 0), 0)),
                   pl.BlockSpec((1, SUBLANE, D), lambda b, i: (b, jnp.minimum((i + 1) * nblk, last), 0)),
                   tok(LANE), tok(LANE), tok(LANE)] + [_layer_spec(c, layer) for c in consts]
                  + [_const_spec(dft.shape)]),
        out_specs=[o[1] for o in outs],
        out_shape=[o[0] for o in outs],
        scratch_shapes=[pltpu.VMEM((tm + 2 * SUBLANE, ML_W), F32), pltpu.VMEM((tm, LANE), F32)],
        compiler_params=_cparams(("parallel", "parallel")),
        name="mix_in",
    )(h3, h3, h3, cos_t, sin_a, sin_b, *consts, dft)


def _attn_kernel(qt_ref, k_ref, vt_ref, o_ref, acc_ref, s_ref, *, tk, unroll):
    qt = qt_ref[0, 0]
    S = k_ref.shape[1]
    n = S // tk
    tq = qt.shape[1]
    acc_ref[...] = jnp.zeros_like(acc_ref)

    def produce(j, slot):
        ks = pl.multiple_of(j * tk, tk)
        st = _dot(k_ref[0, pl.ds(ks, tk), :], qt)
        s_ref[slot] = st
        return jnp.max(st, axis=0, keepdims=True)

    def consume(j, slot, cmax, m_old, l_old):
        ks = pl.multiple_of(j * tk, tk)
        m_new = jnp.maximum(m_old, cmax)
        p = jnp.exp2(s_ref[slot] - m_new)
        alpha = jnp.exp2(m_old - m_new)
        l_new = alpha * l_old + jnp.sum(p, axis=0, keepdims=True)
        acc_ref[...] = alpha * acc_ref[...] + _dot(vt_ref[0, 0, :, pl.ds(ks, tk)], p.astype(BF16))
        return m_new, l_new

    def body(i, carry):
        cmax, m, l = carry
        for u in range(unroll):
            j = unroll * i + u
            nxt = produce(j + 1, (u + 1) % 2)
            m, l = consume(j, u % 2, cmax, m, l)
            cmax = nxt
        return cmax, m, l

    trips = (n - 1) // unroll
    carry = (produce(0, 0), jnp.full((1, tq), -jnp.inf, F32), jnp.zeros((1, tq), F32))
    cmax, m, l = lax.fori_loop(0, trips, body, carry)
    for j in range(unroll * trips, n):
        nxt = produce(j + 1, (j + 1) % 2) if j + 1 < n else None
        m, l = consume(j, j % 2, cmax, m, l)
        cmax = nxt
    o_ref[0] = (acc_ref[...] / l).T.astype(o_ref.dtype)


def _attention(qt, k, vt, tq, tk, unroll):
    B, S, _ = k.shape
    assert S % tq == 0 and S % tk == 0 and unroll % 2 == 0, "logit buffers alternate by chunk parity"
    return pl.pallas_call(
        functools.partial(_attn_kernel, tk=tk, unroll=unroll),
        grid=(B, MLA_HEADS, S // tq),
        in_specs=[pl.BlockSpec((1, 1, MLA_QK_PAD, tq), lambda b, h, i: (b, h, 0, i)),
                  pl.BlockSpec((1, S, MLA_QK_PAD), lambda b, h, i: (b, 0, h)),
                  pl.BlockSpec((1, 1, MLA_V, S), lambda b, h, i: (b, h, 0, 0))],
        out_specs=pl.BlockSpec((1, tq, MLA_V), lambda b, h, i: (b, i, h)),
        out_shape=jax.ShapeDtypeStruct((B, S, MLA_HEADS * MLA_V), BF16),
        scratch_shapes=[pltpu.VMEM((MLA_V, tq), F32), pltpu.VMEM((2, tk, tq), F32)],
        compiler_params=_cparams(("parallel", "parallel", "arbitrary")),
        name="mla_attention",
    )(qt, k, vt)


def _mlstm_pre(ext_ref, gate, cw_ref, cb_ref, wqk_ref, gb_ref, xc_ref, qt_ref, k_ref, gi_ref, gt_ref, chunk):
    halo = SUBLANE
    ts = ext_ref.shape[0] - 2 * halo
    pad = ML_CONV // 2
    xc = cb_ref[...] + cw_ref[0:1, :] * ext_ref[halo - pad:halo - pad + ts, :]
    for j in range(1, ML_CONV):
        o = halo - pad + j
        xc = xc + cw_ref[j:j + 1, :] * ext_ref[o:o + ts, :]
    xc = xc * jax.nn.sigmoid(xc)
    xc_ref[0] = xc
    qk = _dot(xc.astype(BF16), wqk_ref[...])
    qt_ref[0] = qk[:, :ML_W].T.astype(BF16)
    k_ref[0] = qk[:, ML_W:].astype(BF16)
    g_t = (gate + gb_ref[...]).T[G_OFF:G_OFF + GATE_W, :]
    lf_t = jnp.minimum(g_t, 0.0) - jnp.log(1.0 + jnp.exp(-jnp.abs(g_t)))
    p1 = lf_t.astype(BF16)
    r1 = lf_t - p1.astype(F32)
    p2 = r1.astype(BF16)
    p3 = (r1 - p2.astype(F32)).astype(BF16)
    r = lax.broadcasted_iota(jnp.int32, (chunk, chunk), 0)
    c = lax.broadcasted_iota(jnp.int32, (chunk, chunk), 1)
    tri_f = (r <= c).astype(BF16)
    tri_b = (r >= c).astype(BF16)
    sub = lax.broadcasted_iota(jnp.int32, (GATE_W, chunk), 0)
    cols = []
    for k in range(ts // chunk):
        sl = slice(k * chunk, (k + 1) * chunk)
        cum_f = _dot(p1[:, sl], tri_f) + _dot(p2[:, sl], tri_f) + _dot(p3[:, sl], tri_f)
        cum_b = _dot(p1[:, sl], tri_b) + _dot(p2[:, sl], tri_b) + _dot(p3[:, sl], tri_b)
        cols.append(jnp.where(sub < 2 * ML_HEADS, g_t[:, sl], jnp.where(sub < 3 * ML_HEADS, cum_f, cum_b)))
    gt = jnp.concatenate(cols, axis=1) * LOG2E
    gt = jnp.concatenate([gt, gt[0:2 * ML_HEADS] - gt[2 * ML_HEADS:]], axis=0)
    gt_ref[0] = gt
    gi_ref[0] = jnp.concatenate([gt, jnp.zeros((LANE - GATE_ROWS, ts), F32)], axis=0).T


def _mlstm_chunk(blocks, c_ref, n_ref, m_ref):
    nslot = len(blocks)
    bb = nslot // 2
    L = blocks[0][1].shape[0]
    H, dh = ML_HEADS, ML_DIM
    key = lax.broadcasted_iota(jnp.int32, (L, L), 0)
    qry = lax.broadcasted_iota(jnp.int32, (L, L), 1)
    lane_w = lax.broadcasted_iota(jnp.int32, (1, ML_W), 1)
    row_w = lax.broadcasted_iota(jnp.int32, (ML_W, 1), 0)
    hlanes = [(lane_w // dh) == hd for hd in range(H)]
    hrows = [(row_w // dh) == hd for hd in range(H)]
    zero_k = jnp.zeros_like(blocks[0][1])
    zero_v = jnp.zeros_like(blocks[0][2])
    sel8 = (lax.broadcasted_iota(jnp.int32, (SUBLANE, ML_W), 0)
            == lax.broadcasted_iota(jnp.int32, (SUBLANE, ML_W), 1) // dh)
    probs_of = [(si, hd) for si in range(nslot) for hd in range(H)]
    direction = lambda si: si // bb
    gate = lambda kind, si, hd: kind * 2 * H + direction(si) * H + hd
    edge = lambda si: L - 1 if direction(si) == 0 else 0

    QT, K, VT, GI, GT = range(5)
    inter_t, qn_t = [], []
    for si, blk in enumerate(blocks):
        inter_t.append(_dot(c_ref[si].astype(BF16), blk[QT]))
        qn_t.append(_dot(n_ref[si].astype(BF16), blk[QT]))

    def rows_of(kind):
        return jnp.stack([blocks[si][GT][gate(kind, si, hd):gate(kind, si, hd) + 1, :]
                          for si, hd in probs_of])

    m_old = jnp.stack([m_ref[si, hd:hd + 1, 0:1] for si, hd in probs_of])
    bq = rows_of(1)
    imb_r = rows_of(2)
    imb_c = jnp.stack([blocks[si][GI][:, gate(2, si, hd):gate(2, si, hd) + 1]
                       for si, hd in probs_of])
    b_edge = jnp.stack([blocks[si][GT][gate(1, si, hd):gate(1, si, hd) + 1, edge(si):edge(si) + 1]
                        for si, hd in probs_of])
    qn_r = jnp.stack([qn_t[si][hd:hd + 1, :] for si, hd in probs_of])
    qk = jnp.stack([_dot(jnp.where(hlanes[hd], blocks[si][K], zero_k), blocks[si][QT])
                    for si, hd in probs_of])

    raw = imb_c + bq
    half = bb * H
    dmat = jnp.concatenate([jnp.where(key <= qry, raw[:half], -jnp.inf),
                            jnp.where(key >= qry, raw[half:], -jnp.inf)], axis=0)
    m_inter = bq + m_old
    m_t = jnp.maximum(m_inter, jnp.max(dmat, axis=1, keepdims=True))
    w_state = jnp.exp2(m_inter - m_t)
    s = qk * jnp.exp2(dmat - m_t)
    denom = jnp.sum(s, axis=1, keepdims=True) + w_state * qn_r
    scale = 1.0 / jnp.maximum(jnp.abs(denom), jnp.exp2(-m_t))
    wscale = w_state * scale
    probs = s.astype(BF16)

    lw = b_edge + imb_r
    m_new = jnp.maximum(b_edge + m_old, jnp.max(lw, axis=-1, keepdims=True))
    ws = jnp.exp2(lw - m_new)
    dec = jnp.exp2(b_edge + m_old - m_new)

    def per_head_rows(rows, si):
        return jnp.concatenate([jnp.broadcast_to(rows[si * H + hd], (dh, L)) for hd in range(H)], axis=0)

    def per_head_lanes(vals, si):
        e = vals[si * H + H - 1]
        for hd in reversed(range(H - 1)):
            e = jnp.where(hlanes[hd], vals[si * H + hd], e)
        return e

    blockdiag = (row_w // dh) == (lane_w // dh)
    outs = []
    for si, blk in enumerate(blocks):
        k, vt = blk[K], blk[VT]
        v_heads = jnp.concatenate([jnp.where(hrows[hd], vt, zero_v) for hd in range(H)], axis=1)
        intra_t = _dot(v_heads, jnp.concatenate([probs[si * H + hd] for hd in range(H)], axis=0))
        out_t = intra_t * per_head_rows(scale, si) + inter_t[si] * per_head_rows(wscale, si)
        outs.append(out_t.T)

        dec_l = per_head_lanes(dec, si)
        vw = (vt.astype(F32) * per_head_rows(ws, si)).astype(BF16)
        c_new = c_ref[si] * dec_l + _dot(vw, k)
        c_ref[si] = jnp.where(blockdiag, c_new, 0.0)
        ws8 = jnp.concatenate([ws[si * H + hd] for hd in range(H)]
                              + [jnp.zeros((SUBLANE - H, L), F32)], axis=0).astype(BF16)
        n_ref[si] = n_ref[si] * dec_l + jnp.where(sel8, _dot(ws8, k), 0.0)
        m_ref[si, 0:H, :] = jnp.concatenate(
            [jnp.broadcast_to(m_new[si * H + hd], (1, LANE)) for hd in range(H)], axis=0)
    return outs


def _mlscan_kernel(qf_ref, kf_ref, vf_ref, gif_ref, gtf_ref, qb_ref, kb_ref, vb_ref, gib_ref, gtb_ref,
                   hf_ref, hb_ref, c_ref, n_ref, m_ref, *, bb):
    @pl.when(pl.program_id(1) == 0)
    def _():
        c_ref[...] = jnp.zeros_like(c_ref)
        n_ref[...] = jnp.zeros_like(n_ref)
        m_ref[...] = jnp.zeros_like(m_ref)

    fwd = [(qf_ref[b], kf_ref[b], vf_ref[b], gif_ref[b], gtf_ref[b]) for b in range(bb)]
    bwd = [(qb_ref[b], kb_ref[b], vb_ref[b], gib_ref[b], gtb_ref[b]) for b in range(bb)]
    outs = _mlstm_chunk(fwd + bwd, c_ref, n_ref, m_ref)
    for b in range(bb):
        hf_ref[b] = outs[b]
        hb_ref[b] = outs[bb + b]


def _mlstm_scan(qt, k, vt, gi, gt, L, bb):
    B, S, W = k.shape
    nc = S // L
    fw = lambda b, c: c
    bw = lambda b, c: nc - 1 - c
    def specs(cm):
        return [pl.BlockSpec((bb, W, L), lambda b, c: (b, 0, cm(b, c))),
                pl.BlockSpec((bb, L, W), lambda b, c: (b, cm(b, c), 0)),
                pl.BlockSpec((bb, W, L), lambda b, c: (b, 0, cm(b, c))),
                pl.BlockSpec((bb, L, LANE), lambda b, c: (b, cm(b, c), 0)),
                pl.BlockSpec((bb, GATE_ROWS, L), lambda b, c: (b, 0, cm(b, c)))]
    out = jax.ShapeDtypeStruct((B, S, W), F32)
    return pl.pallas_call(
        functools.partial(_mlscan_kernel, bb=bb),
        grid=(B // bb, nc),
        in_specs=specs(fw) + specs(bw),
        out_specs=[pl.BlockSpec((bb, L, W), lambda b, c: (b, c, 0)),
                   pl.BlockSpec((bb, L, W), lambda b, c: (b, nc - 1 - c, 0))],
        out_shape=[out, out],
        scratch_shapes=[pltpu.VMEM((2 * bb, W, W), F32), pltpu.VMEM((2 * bb, SUBLANE, W), F32),
                        pltpu.VMEM((2 * bb, SUBLANE, LANE), F32)],
        compiler_params=_cparams(("parallel", "arbitrary")),
        name="mlstm_scan",
    )(qt, k, vt, gi, gt, qt, k, vt, gi, gt)


def _fnet1_kernel(wr_ref, wi_ref, m1_ref, a_ref):
    n1 = wr_ref.shape[1]
    a = _dot(m1_ref[:, 0:n1], wr_ref[0]) + _dot(m1_ref[:, n1:], wi_ref[0])
    a_ref[0, 0] = a[0:n1].astype(BF16)
    a_ref[0, 1] = a[n1:].astype(BF16)


def _fnet3_kernel(a_ref, tab_ref, w_ref, b_ref, o_ref):
    for j in range(a_ref.shape[2]):
        z = jnp.concatenate([a_ref[0, 0, j], a_ref[0, 1, j]], axis=0)
        y = _dot(tab_ref[j], z)
        o_ref[0, :, j, :] = _dot(y.astype(BF16), w_ref[...]) + b_ref[...]


def _fnet(fr, fi, m1, tab3, wbd, bias, layer, tc, kb):
    B, S, W = fr.shape
    n1 = FN_N1
    n2 = S // n1
    cols = n2 * W
    wr = fr.reshape(B, n1, cols)
    wi = fi.reshape(B, n1, cols)
    a = pl.pallas_call(
        _fnet1_kernel,
        grid=(B, cols // tc),
        in_specs=[pl.BlockSpec((1, n1, tc), lambda b, i: (b, 0, i)),
                  pl.BlockSpec((1, n1, tc), lambda b, i: (b, 0, i)),
                  _const_spec(m1.shape)],
        out_specs=pl.BlockSpec((1, 2, n1, tc), lambda b, i: (b, 0, 0, i)),
        out_shape=jax.ShapeDtypeStruct((B, 2, n1, cols), BF16),
        compiler_params=_cparams(("parallel", "parallel")),
        name="fnet_dft1",
    )(wr, wi, m1)
    a = a.reshape(B, 2, n1, n2, W)
    y = pl.pallas_call(
        _fnet3_kernel,
        grid=(B, n1 // kb),
        in_specs=[pl.BlockSpec((1, 2, kb, n2, W), lambda b, i: (b, 0, i, 0, 0)),
                  pl.BlockSpec((kb, n2, 2 * n2), lambda b, i: (i, 0, 0)),
                  _layer_spec(wbd, layer), _layer_spec(bias, layer)],
        out_specs=pl.BlockSpec((1, n2, kb, W), lambda b, i: (b, 0, i, 0)),
        out_shape=jax.ShapeDtypeStruct((B, n2, n1, W), F32),
        compiler_params=_cparams(("parallel", "parallel")),
        name="fnet_dft2",
    )(a, tab3, wbd, bias)
    return y.reshape(B, S, W)


def _post_kernel(h_ref, ya_ref, hf_ref, hb_ref, xc_ref, mo_ref, yf_ref, p_ref,
                 hn_ref, sk_ref, ones_ref, wo_ref, g2_ref, wg2_ref, wu2_ref, wd_ref,
                 gn_ref, wg_ref, wp_ref, pn_ref, fn_ref, o_ref, acc_ref, *, final, tf):
    tm = h_ref.shape[0]
    a_w = ya_ref.shape[1]
    m_w = a_w + ML_W
    groups = [slice(r, r + tm // ROW_GROUPS) for r in range(0, tm, tm // ROW_GROUPS)]

    def project(r):
        hm = hf_ref[r, :] + hb_ref[r, :]
        sq = hm * hm
        hi = sq.astype(BF16)
        lo = (sq - hi.astype(F32)).astype(BF16)
        ms = (_dot(hi, ones_ref[...]) + _dot(lo, ones_ref[...])) * (1.0 / ML_DIM)
        ym = (hm * lax.rsqrt(ms + EPS) * hn_ref[...] + sk_ref[...] * xc_ref[r, :]) * mo_ref[r, :]
        return (h_ref[r, :] + _dot(ya_ref[r, :], wo_ref[0:a_w, :])
                + _dot(ym.astype(BF16), wo_ref[a_w:m_w, :])
                + _dot(yf_ref[r, :].astype(BF16), wo_ref[m_w:, :]))

    h1 = jnp.concatenate([project(r) for r in groups], axis=0)
    h2 = h1 + _swiglu_half(h1, g2_ref, wg2_ref, wu2_ref, wd_ref, acc_ref, tf)

    for r in groups:
        h2r = h2[r, :]
        e = _rms(_dot(p_ref[r, :].astype(BF16), wp_ref[...]), pn_ref[...])
        gate = jax.nn.sigmoid(_dot(_rms(h2r, gn_ref[...]).astype(BF16), wg_ref[...]))
        h3 = h2r + gate * e
        if final:
            h3 = _rms(h3, fn_ref[...])
        o_ref[r, :] = h3


def _post(h, ya, hf, hb, xc, mo, yf, p, ones_bd, params, final_norm, layer, tm, tf, final):
    T, D = h.shape
    tok = lambda w: pl.BlockSpec((tm, w), lambda i: (i, 0))
    acts = [h, ya, hf, hb, xc, mo, yf]
    hn, sk, wo, g2, wg2, wu2, wd, gn, wg, wp, pn = params
    consts = [hn, sk, ones_bd, wo, g2, wg2, wu2, wd, gn, wg, wp, pn, final_norm]
    shared = (2, len(consts) - 1)
    specs = [_const_spec(c.shape) if j in shared else _layer_spec(c, layer) for j, c in enumerate(consts)]
    return pl.pallas_call(
        functools.partial(_post_kernel, final=final, tf=tf),
        grid=(T // tm,),
        in_specs=([tok(a.shape[1]) for a in acts]
                  + [pl.BlockSpec((None, tm, p.shape[2]), lambda i: (layer, i, 0))] + specs),
        out_specs=tok(D),
        out_shape=jax.ShapeDtypeStruct((T, D), F32),
        scratch_shapes=[pltpu.VMEM((tm, D), F32)],
        compiler_params=_cparams(("parallel",)),
        name="post",
    )(*acts, p, *consts)


def _block_diag(w):
    *lead, G, a, b = w.shape
    eye = jnp.eye(G, dtype=w.dtype)
    return (eye[:, None, :, None] * w[..., :, :, None, :]).reshape(*lead, G * a, G * b)


def _dft_tables(S):
    n1, n2 = FN_N1, S // FN_N1
    c = np.arange(FN_DIM)
    ang = 2.0 * np.pi * np.outer(c, c) / FN_DIM
    eye = np.eye(FN_GROUPS)
    chan = np.concatenate([np.kron(eye, np.cos(ang)), -np.kron(eye, np.sin(ang))], axis=1) / np.sqrt(FN_DIM)
    k1 = np.arange(n1)
    a1 = 2.0 * np.pi * np.outer(k1, k1) / n1
    m1 = np.block([[np.cos(a1), np.sin(a1)], [-np.sin(a1), np.cos(a1)]])
    k = k1[:, None, None] + n1 * np.arange(n2)[None, :, None]
    s2 = np.arange(n2)[None, None, :]
    a3 = 2.0 * np.pi * ((k * s2) % S) / S
    tab3 = np.concatenate([np.cos(a3), np.sin(a3)], axis=-1) / np.sqrt(S)
    bf = lambda t: jnp.asarray(t, dtype=F32).astype(BF16)
    return bf(chan), bf(m1), bf(tab3)


def _rope_consts():
    half = MLA_ROPE // 2
    inv = 1.0 / (ROPE_THETA ** (jnp.arange(0, MLA_ROPE, 2, dtype=F32) / MLA_ROPE))
    z = jnp.zeros((half,), F32)
    o = jnp.ones((half,), F32)
    rows = [jnp.concatenate([inv, inv, z, z]), jnp.concatenate([o, o, z, z]),
            jnp.concatenate([z, o, z, z]), jnp.concatenate([-o, z, z, z])]
    rows += [jnp.zeros((LANE,), F32)] * (SUBLANE - len(rows))
    return jnp.stack(rows)


def _tiles(B, S):
    T = B * S
    return dict(tm=min(512, T), tf=MXU_DIM, tmix=min(1024, S), tq=min(2048, S), tk=min(1024, S), attn_unroll=2,
                chunk=min(256, S), bb=2 if B % 2 == 0 else 1,
                tc=min(8192, (S // FN_N1) * FN_W), kb=2 * SUBLANE,
                trope=min(1024, T))


def kernel(x, p, positions, ffn1_norm, ffn1_w_gate, ffn1_w_up, ffn1_w_down, mix_norm, w_in, mla_q_norm, mla_w_uq, mla_kv_norm, mla_w_ukv, mlstm_conv_w, mlstm_conv_b, mlstm_w_q, mlstm_w_k, mlstm_i_bias, mlstm_f_bias, mlstm_head_norm, mlstm_skip, fnet_w, fnet_b, w_out, ffn2_norm, ffn2_w_gate, ffn2_w_up, ffn2_w_down, ple_gate_norm, ple_w_gate, ple_w_proj, ple_post_norm, final_norm):
    B, S, D = x.shape
    depth = p.shape[0]
    T = B * S
    t = _tiles(B, S)
    Ld = depth
    rows = lambda a: a.reshape(Ld, 1, -1).astype(F32)

    tab = _rope_consts()
    cos_t, sin_a, sin_b = [a.reshape(B, S, LANE) for a in
                           _rope_tables(positions.reshape(T, 1).astype(jnp.int32), tab, t["trope"])]
    dft_chan, dft_m1, dft_tab3 = _dft_tables(S)
    ones_bd = _block_diag(jnp.ones((ML_HEADS, ML_DIM, ML_DIM), BF16))
    qscale = float((MLA_NOPE + MLA_ROPE) ** -0.5 * LOG2E)

    wg1, wu1, wd1 = ffn1_w_gate.astype(BF16), ffn1_w_up.astype(BF16), ffn1_w_down.astype(BF16)
    wg2, wu2, wd2 = ffn2_w_gate.astype(BF16), ffn2_w_up.astype(BF16), ffn2_w_down.astype(BF16)
    o_kr = MLA_Q_LORA + MLA_KV_LORA
    o_mx = o_kr + MLA_ROPE
    o_g = o_mx + 3 * ML_W
    o_f = o_g + GATE_W
    wi = w_in.astype(BF16)
    zpad = lambda n: jnp.zeros((Ld, D, n), BF16)
    win = jnp.concatenate([wi[..., :o_kr], wi[..., o_mx:o_g], wi[..., o_f:], wi[..., o_kr:o_mx],
                           wi[..., o_g:o_f], zpad(LANE - MLA_ROPE - GATE_W)], axis=-1)
    wuq = jnp.pad(mla_w_uq.astype(BF16).reshape(Ld, MLA_Q_LORA, MLA_HEADS, MLA_NOPE + MLA_ROPE),
                  ((0, 0), (0, 0), (0, 0), (0, MLA_QK_PAD - MLA_NOPE - MLA_ROPE)))
    wuq = wuq.reshape(Ld, MLA_Q_LORA, MLA_HEADS * MLA_QK_PAD)
    wukv = mla_w_ukv.astype(BF16).reshape(Ld, MLA_KV_LORA, MLA_HEADS, 2, MLA_NOPE)
    wukv = wukv.transpose(0, 1, 3, 2, 4).reshape(Ld, MLA_KV_LORA, 2 * MLA_HEADS * MLA_NOPE)
    wqk = jnp.concatenate([_block_diag(mlstm_w_q) * (ML_DIM ** -0.5), _block_diag(mlstm_w_k)],
                          axis=-1).astype(BF16)
    gbias = jnp.concatenate([jnp.zeros((Ld, G_OFF), F32), mlstm_i_bias.reshape(Ld, -1),
                             mlstm_f_bias.reshape(Ld, -1), jnp.zeros((Ld, LANE - G_OFF - GATE_W), F32)],
                            axis=-1).reshape(Ld, 1, LANE)
    fw_bd = _block_diag(fnet_w).astype(BF16)
    post_params = [rows(mlstm_head_norm), rows(mlstm_skip), w_out.astype(BF16), rows(ffn2_norm), wg2, wu2, wd2,
                   rows(ple_gate_norm), ple_w_gate.astype(BF16), ple_w_proj.astype(BF16),
                   rows(ple_post_norm)]
    g1, gmix, gq, gkv = rows(ffn1_norm), rows(mix_norm), rows(mla_q_norm), rows(mla_kv_norm)
    conv_w, conv_b, fnet_bias = mlstm_conv_w.astype(F32), rows(mlstm_conv_b), rows(fnet_b)
    p_flat = p.reshape(Ld, T, p.shape[-1])
    fin = final_norm.reshape(1, -1).astype(F32)
    flat = lambda a: a.reshape(T, a.shape[-1])

    h = x.reshape(T, D)
    for i in range(depth):
        h = _ffn(h, g1, wg1, wu1, wd1, i, 2 * t["tm"], t["tf"])
        qt, k, vt, mv, mo, fr, fi, xc, mqt, mk, gi, gt = _mixin(
            h.reshape(B, S, D), cos_t, sin_a, sin_b, gmix, win, gq, wuq, gkv, wukv, conv_w, conv_b, wqk,
            gbias, dft_chan, i, t["tmix"], qscale, t["chunk"])
        y_mla = _attention(qt, k, vt, t["tq"], t["tk"], t["attn_unroll"])
        hf, hb = _mlstm_scan(mqt, mk, mv, gi, gt, t["chunk"], t["bb"])
        y_fnet = _fnet(fr, fi, dft_m1, dft_tab3, fw_bd, fnet_bias, i, t["tc"], t["kb"])
        h = _post(h, flat(y_mla), flat(hf), flat(hb), flat(xc), flat(mo), flat(y_fnet), p_flat, ones_bd,
                  post_params, fin, i, t["tm"], t["tf"], final=(i == depth - 1))
    return h.reshape(B, S, D)
```

```python
import functools

import numpy as np
import jax
import jax.numpy as jnp
from jax import lax
from jax.experimental import pallas as pl
from jax.experimental.pallas import tpu as pltpu

F32 = jnp.float32
BF16 = jnp.bfloat16

EPS = 1e-6
ROPE_THETA = 10000.0
LOG2E = float(np.log2(np.e))
LANE = 128
SUBLANE = 8
MXU_DIM = 256
VMEM_LIMIT = 56 * 1024 * 1024

MLA_HEADS = 4
MLA_Q_LORA = 384
MLA_KV_LORA = 256
MLA_NOPE = 128
MLA_ROPE = 64
MLA_V = 128
MLA_QK_PAD = MXU_DIM
ML_HEADS = 4
ML_DIM = 64
ML_W = ML_HEADS * ML_DIM
ML_CONV = 5
FN_GROUPS = 4
FN_DIM = 64
FN_W = FN_GROUPS * FN_DIM
FN_N1 = 64
GATE_W = 4 * ML_HEADS
GATE_ROWS = 6 * ML_HEADS
ROW_GROUPS = 2

U_CQ = 0
U_CKV = U_CQ + MLA_Q_LORA
U_MX = U_CKV + MLA_KV_LORA
U_MV = U_MX + ML_W
U_MO = U_MV + ML_W
U_FIN = U_MO + ML_W
U_KR = U_FIN + FN_W
G_OFF = MLA_ROPE
U_TOT = U_KR + LANE


def _cparams(sem):
    return pltpu.CompilerParams(dimension_semantics=sem, vmem_limit_bytes=VMEM_LIMIT)


def _const_spec(shape):
    nd = len(shape)
    return pl.BlockSpec(shape, lambda *_: (0,) * nd, pipeline_mode=pl.Buffered(1))


def _layer_spec(a, layer):
    nd = a.ndim - 1
    return pl.BlockSpec((None,) + a.shape[1:], lambda *_: (layer,) + (0,) * nd,
                        pipeline_mode=pl.Buffered(1))


def _rms(x, g):
    ms = jnp.mean(x * x, axis=-1, keepdims=True)
    return x * lax.rsqrt(ms + EPS) * g


def _dot(a, b):
    return jnp.dot(a, b, preferred_element_type=F32)


def _swiglu_half(x, g_ref, wg_ref, wu_ref, wd_ref, acc_ref, tf):
    xn = _rms(x, g_ref[...]).astype(BF16)
    for c in range(wd_ref.shape[0] // tf):
        g = _dot(xn, wg_ref[:, c * tf:(c + 1) * tf])
        u = _dot(xn, wu_ref[:, c * tf:(c + 1) * tf])
        a = (g * jax.nn.sigmoid(g) * u).astype(BF16)
        part = _dot(a, wd_ref[c * tf:(c + 1) * tf, :])
        if c == 0:
            acc_ref[...] = part
        else:
            acc_ref[...] += part
    return 0.5 * acc_ref[...]


def _ffn_kernel(h_ref, g_ref, wg_ref, wu_ref, wd_ref, o_ref, acc_ref, *, tf):
    x = h_ref[...]
    o_ref[...] = x + _swiglu_half(x, g_ref, wg_ref, wu_ref, wd_ref, acc_ref, tf)


def _ffn(h, g, wg, wu, wd, layer, tm, tf):
    T, D = h.shape
    return pl.pallas_call(
        functools.partial(_ffn_kernel, tf=tf),
        grid=(T // tm,),
        in_specs=[pl.BlockSpec((tm, D), lambda i: (i, 0)),
                  _layer_spec(g, layer), _layer_spec(wg, layer), _layer_spec(wu, layer),
                  _layer_spec(wd, layer)],
        out_specs=pl.BlockSpec((tm, D), lambda i: (i, 0)),
        out_shape=jax.ShapeDtypeStruct((T, D), F32),
        scratch_shapes=[pltpu.VMEM((tm, D), F32)],
        compiler_params=_cparams(("parallel",)),
        name="ffn",
    )(h, g, wg, wu, wd)


def _rope_kernel(pos_ref, tab_ref, cos_ref, sina_ref, sinb_ref):
    ang = pos_ref[...].astype(F32) * tab_ref[0:1, :]
    c = jnp.cos(ang)
    s = jnp.sin(ang)
    cos_ref[...] = c * tab_ref[1:2, :]
    sina_ref[...] = s * tab_ref[2:3, :]
    sinb_ref[...] = s * tab_ref[3:4, :]


def _rope_tables(pos, tab, tm):
    T = pos.shape[0]
    out = jax.ShapeDtypeStruct((T, LANE), F32)
    spec = pl.BlockSpec((tm, LANE), lambda i: (i, 0))
    return pl.pallas_call(
        _rope_kernel,
        grid=(T // tm,),
        in_specs=[pl.BlockSpec((tm, 1), lambda i: (i, 0)), _const_spec(tab.shape)],
        out_specs=[spec, spec, spec],
        out_shape=[out, out, out],
        compiler_params=_cparams(("parallel",)),
        name="rope_tables",
    )(pos, tab)


def _rope(blk, cos_t, sin_a, sin_b):
    half = MLA_ROPE // 2
    return (blk * cos_t + pltpu.roll(blk, half, 1) * sin_a
            + pltpu.roll(blk, LANE - half, 1) * sin_b)


def _mixin_kernel(h_ref, hprev_ref, hnext_ref, cos_ref, sina_ref, sinb_ref, g_ref, win_ref, qn_ref, wuq_ref,
                  kvn_ref, wukv_ref, cw_ref, cb_ref, wqk_ref, gb_ref, dft_ref,
                  qt_ref, k_ref, vt_ref, mv_ref, mo_ref, fr_ref, fi_ref, xc_ref, mqt_ref, mk_ref, gi_ref,
                  gt_ref, ext_ref, gate_ref, *, qscale, chunk):
    tm = h_ref.shape[1]
    halo = SUBLANE
    for r0 in range(0, tm, tm // ROW_GROUPS):
        r = slice(r0, r0 + tm // ROW_GROUPS)
        x = h_ref[0, r, :]
        xn = _rms(x, g_ref[...]).astype(BF16)
        u = _dot(xn, win_ref[...])
        cos_t, sin_a, sin_b = cos_ref[0, r, :], sina_ref[0, r, :], sinb_ref[0, r, :]

        cq = _rms(u[:, U_CQ:U_CKV], qn_ref[...]).astype(BF16)
        q = _dot(cq, wuq_ref[...])
        for hd in range(MLA_HEADS):
            o = hd * MLA_QK_PAD
            qt_ref[0, hd, 0:MLA_NOPE, r] = (q[:, o:o + MLA_NOPE] * qscale).T.astype(BF16)
            qr = _rope(q[:, o + MLA_NOPE:o + MLA_QK_PAD], cos_t, sin_a, sin_b)
            qt_ref[0, hd, MLA_NOPE:MLA_QK_PAD, r] = (qr * qscale).T.astype(BF16)

        ckv = _rms(u[:, U_CKV:U_MX], kvn_ref[...]).astype(BF16)
        kv = _dot(ckv, wukv_ref[...])
        kr = _rope(u[:, U_KR:U_KR + LANE], cos_t, sin_a, sin_b).astype(BF16)
        for hd in range(MLA_HEADS):
            o = hd * MLA_QK_PAD
            k_ref[0, r, o:o + MLA_NOPE] = kv[:, hd * MLA_NOPE:(hd + 1) * MLA_NOPE].astype(BF16)
            k_ref[0, r, o + MLA_NOPE:o + MLA_QK_PAD] = kr
            vo = (MLA_HEADS + hd) * MLA_NOPE
            vt_ref[0, hd, :, r] = kv[:, vo:vo + MLA_V].T.astype(BF16)

        ext_ref[halo + r0:halo + r0 + tm // ROW_GROUPS, :] = u[:, U_MX:U_MV]
        mv_ref[0, :, r] = u[:, U_MV:U_MO].T.astype(BF16)
        mo_ref[0, r, :] = jax.nn.sigmoid(u[:, U_MO:U_FIN])
        gate_ref[r, :] = u[:, U_KR:U_TOT]

        f = _dot(u[:, U_FIN:U_KR].astype(BF16), dft_ref[...])
        fr_ref[0, r, :] = f[:, :FN_W].astype(BF16)
        fi_ref[0, r, :] = f[:, FN_W:].astype(BF16)

    i = pl.program_id(1)
    xh = jnp.concatenate([hprev_ref[0], hnext_ref[0]], axis=0)
    mh = _dot(_rms(xh, g_ref[...]).astype(BF16), win_ref[:, U_MX:U_MV])
    ext_ref[0:halo, :] = jnp.where(i > 0, mh[0:halo], 0.0)
    ext_ref[halo + tm:, :] = jnp.where(i < pl.num_programs(1) - 1, mh[halo:], 0.0)
    _mlstm_pre(ext_ref, gate_ref[...], cw_ref, cb_ref, wqk_ref, gb_ref,
               xc_ref, mqt_ref, mk_ref, gi_ref, gt_ref, chunk)


def _mixin(h3, cos_t, sin_a, sin_b, g, win, qn, wuq, kvn, wukv, cw, cb, wqk, gb, dft, layer, tm, qscale, chunk):
    B, S, D = h3.shape
    nblk = tm // SUBLANE
    last = S // SUBLANE - 1
    tok = lambda w: pl.BlockSpec((1, tm, w), lambda b, i: (b, i, 0))
    tcol = lambda w: pl.BlockSpec((1, w, tm), lambda b, i: (b, 0, i))
    outs = [
        (jax.ShapeDtypeStruct((B, MLA_HEADS, MLA_QK_PAD, S), BF16),
         pl.BlockSpec((1, MLA_HEADS, MLA_QK_PAD, tm), lambda b, i: (b, 0, 0, i))),
        (jax.ShapeDtypeStruct((B, S, MLA_HEADS * MLA_QK_PAD), BF16), tok(MLA_HEADS * MLA_QK_PAD)),
        (jax.ShapeDtypeStruct((B, MLA_HEADS, MLA_V, S), BF16),
         pl.BlockSpec((1, MLA_HEADS, MLA_V, tm), lambda b, i: (b, 0, 0, i))),
        (jax.ShapeDtypeStruct((B, ML_W, S), BF16), tcol(ML_W)),
        (jax.ShapeDtypeStruct((B, S, ML_W), F32), tok(ML_W)),
        (jax.ShapeDtypeStruct((B, S, FN_W), BF16), tok(FN_W)),
        (jax.ShapeDtypeStruct((B, S, FN_W), BF16), tok(FN_W)),
        (jax.ShapeDtypeStruct((B, S, ML_W), F32), tok(ML_W)),
        (jax.ShapeDtypeStruct((B, ML_W, S), BF16), tcol(ML_W)),
        (jax.ShapeDtypeStruct((B, S, ML_W), BF16), tok(ML_W)),
        (jax.ShapeDtypeStruct((B, S, LANE), F32), tok(LANE)),
        (jax.ShapeDtypeStruct((B, GATE_ROWS, S), F32), tcol(GATE_ROWS)),
    ]
    consts = [g, win, qn, wuq, kvn, wukv, cw, cb, wqk, gb]
    return pl.pallas_call(
        functools.partial(_mixin_kernel, qscale=qscale, chunk=chunk),
        grid=(B, S // tm),
        in_specs=([tok(D),
                   pl.BlockSpec((1, SUBLANE, D), lambda b, i: (b, jnp.maximum(i * nblk - 1, 0), 0)),
                   pl.BlockSpec((1, SUBLANE, D), lambda b, i: (b, jnp.minimum((i + 1) * nblk, last), 0)),
                   tok(LANE), tok(LANE), tok(LANE)] + [_layer_spec(c, layer) for c in consts]
                  + [_const_spec(dft.shape)]),
        out_specs=[o[1] for o in outs],
        out_shape=[o[0] for o in outs],
        scratch_shapes=[pltpu.VMEM((tm + 2 * SUBLANE, ML_W), F32), pltpu.VMEM((tm, LANE), F32)],
        compiler_params=_cparams(("parallel", "parallel")),
        name="mix_in",
    )(h3, h3, h3, cos_t, sin_a, sin_b, *consts, dft)


def _attn_kernel(qt_ref, k_ref, vt_ref, o_ref, acc_ref, s_ref, *, tk, unroll):
    qt = qt_ref[0, 0]
    S = k_ref.shape[1]
    n = S // tk
    tq = qt.shape[1]
    acc_ref[...] = jnp.zeros_like(acc_ref)

    def produce(j, slot):
        ks = pl.multiple_of(j * tk, tk)
        st = _dot(k_ref[0, pl.ds(ks, tk), :], qt)
        s_ref[slot] = st
        return jnp.max(st, axis=0, keepdims=True)

    def consume(j, slot, cmax, m_old, l_old):
        ks = pl.multiple_of(j * tk, tk)
        m_new = jnp.maximum(m_old, cmax)
        p = jnp.exp2(s_ref[slot] - m_new)
        alpha = jnp.exp2(m_old - m_new)
        l_new = alpha * l_old + jnp.sum(p, axis=0, keepdims=True)
        acc_ref[...] = alpha * acc_ref[...] + _dot(vt_ref[0, 0, :, pl.ds(ks, tk)], p.astype(BF16))
        return m_new, l_new

    def body(i, carry):
        cmax, m, l = carry
        for u in range(unroll):
            j = unroll * i + u
            nxt = produce(j + 1, (u + 1) % 2)
            m, l = consume(j, u % 2, cmax, m, l)
            cmax = nxt
        return cmax, m, l

    trips = (n - 1) // unroll
    carry = (produce(0, 0), jnp.full((1, tq), -jnp.inf, F32), jnp.zeros((1, tq), F32))
    cmax, m, l = lax.fori_loop(0, trips, body, carry)
    for j in range(unroll * trips, n):
        nxt = produce(j + 1, (j + 1) % 2) if j + 1 < n else None
        m, l = consume(j, j % 2, cmax, m, l)
        cmax = nxt
    o_ref[0] = (acc_ref[...] / l).T.astype(o_ref.dtype)


def _attention(qt, k, vt, tq, tk, unroll):
    B, S, _ = k.shape
    assert S % tq == 0 and S % tk == 0 and unroll % 2 == 0, "logit buffers alternate by chunk parity"
    return pl.pallas_call(
        functools.partial(_attn_kernel, tk=tk, unroll=unroll),
        grid=(B, MLA_HEADS, S // tq),
        in_specs=[pl.BlockSpec((1, 1, MLA_QK_PAD, tq), lambda b, h, i: (b, h, 0, i)),
                  pl.BlockSpec((1, S, MLA_QK_PAD), lambda b, h, i: (b, 0, h)),
                  pl.BlockSpec((1, 1, MLA_V, S), lambda b, h, i: (b, h, 0, 0))],
        out_specs=pl.BlockSpec((1, tq, MLA_V), lambda b, h, i: (b, i, h)),
        out_shape=jax.ShapeDtypeStruct((B, S, MLA_HEADS * MLA_V), BF16),
        scratch_shapes=[pltpu.VMEM((MLA_V, tq), F32), pltpu.VMEM((2, tk, tq), F32)],
        compiler_params=_cparams(("parallel", "parallel", "arbitrary")),
        name="mla_attention",
    )(qt, k, vt)


def _mlstm_pre(ext_ref, gate, cw_ref, cb_ref, wqk_ref, gb_ref, xc_ref, qt_ref, k_ref, gi_ref, gt_ref, chunk):
    halo = SUBLANE
    ts = ext_ref.shape[0] - 2 * halo
    pad = ML_CONV // 2
    xc = cb_ref[...] + cw_ref[0:1, :] * ext_ref[halo - pad:halo - pad + ts, :]
    for j in range(1, ML_CONV):
        o = halo - pad + j
        xc = xc + cw_ref[j:j + 1, :] * ext_ref[o:o + ts, :]
    xc = xc * jax.nn.sigmoid(xc)
    xc_ref[0] = xc
    qk = _dot(xc.astype(BF16), wqk_ref[...])
    qt_ref[0] = qk[:, :ML_W].T.astype(BF16)
    k_ref[0] = qk[:, ML_W:].astype(BF16)
    g_t = (gate + gb_ref[...]).T[G_OFF:G_OFF + GATE_W, :]
    lf_t = jnp.minimum(g_t, 0.0) - jnp.log(1.0 + jnp.exp(-jnp.abs(g_t)))
    p1 = lf_t.astype(BF16)
    r1 = lf_t - p1.astype(F32)
    p2 = r1.astype(BF16)
    p3 = (r1 - p2.astype(F32)).astype(BF16)
    r = lax.broadcasted_iota(jnp.int32, (chunk, chunk), 0)
    c = lax.broadcasted_iota(jnp.int32, (chunk, chunk), 1)
    tri_f = (r <= c).astype(BF16)
    tri_b = (r >= c).astype(BF16)
    sub = lax.broadcasted_iota(jnp.int32, (GATE_W, chunk), 0)
    cols = []
    for k in range(ts // chunk):
        sl = slice(k * chunk, (k + 1) * chunk)
        cum_f = _dot(p1[:, sl], tri_f) + _dot(p2[:, sl], tri_f) + _dot(p3[:, sl], tri_f)
        cum_b = _dot(p1[:, sl], tri_b) + _dot(p2[:, sl], tri_b) + _dot(p3[:, sl], tri_b)
        cols.append(jnp.where(sub < 2 * ML_HEADS, g_t[:, sl], jnp.where(sub < 3 * ML_HEADS, cum_f, cum_b)))
    gt = jnp.concatenate(cols, axis=1) * LOG2E
    gt = jnp.concatenate([gt, gt[0:2 * ML_HEADS] - gt[2 * ML_HEADS:]], axis=0)
    gt_ref[0] = gt
    gi_ref[0] = jnp.concatenate([gt, jnp.zeros((LANE - GATE_ROWS, ts), F32)], axis=0).T


def _mlstm_chunk(blocks, c_ref, n_ref, m_ref):
    nslot = len(blocks)
    bb = nslot // 2
    L = blocks[0][1].shape[0]
    H, dh = ML_HEADS, ML_DIM
    key = lax.broadcasted_iota(jnp.int32, (L, L), 0)
    qry = lax.broadcasted_iota(jnp.int32, (L, L), 1)
    lane_w = lax.broadcasted_iota(jnp.int32, (1, ML_W), 1)
    row_w = lax.broadcasted_iota(jnp.int32, (ML_W, 1), 0)
    hlanes = [(lane_w // dh) == hd for hd in range(H)]
    hrows = [(row_w // dh) == hd for hd in range(H)]
    zero_k = jnp.zeros_like(blocks[0][1])
    zero_v = jnp.zeros_like(blocks[0][2])
    sel8 = (lax.broadcasted_iota(jnp.int32, (SUBLANE, ML_W), 0)
            == lax.broadcasted_iota(jnp.int32, (SUBLANE, ML_W), 1) // dh)
    probs_of = [(si, hd) for si in range(nslot) for hd in range(H)]
    direction = lambda si: si // bb
    gate = lambda kind, si, hd: kind * 2 * H + direction(si) * H + hd
    edge = lambda si: L - 1 if direction(si) == 0 else 0

    QT, K, VT, GI, GT = range(5)
    inter_t, qn_t = [], []
    for si, blk in enumerate(blocks):
        inter_t.append(_dot(c_ref[si].astype(BF16), blk[QT]))
        qn_t.append(_dot(n_ref[si].astype(BF16), blk[QT]))

    def rows_of(kind):
        return jnp.stack([blocks[si][GT][gate(kind, si, hd):gate(kind, si, hd) + 1, :]
                          for si, hd in probs_of])

    m_old = jnp.stack([m_ref[si, hd:hd + 1, 0:1] for si, hd in probs_of])
    bq = rows_of(1)
    imb_r = rows_of(2)
    imb_c = jnp.stack([blocks[si][GI][:, gate(2, si, hd):gate(2, si, hd) + 1]
                       for si, hd in probs_of])
    b_edge = jnp.stack([blocks[si][GT][gate(1, si, hd):gate(1, si, hd) + 1, edge(si):edge(si) + 1]
                        for si, hd in probs_of])
    qn_r = jnp.stack([qn_t[si][hd:hd + 1, :] for si, hd in probs_of])
    qk = jnp.stack([_dot(jnp.where(hlanes[hd], blocks[si][K], zero_k), blocks[si][QT])
                    for si, hd in probs_of])

    raw = imb_c + bq
    half = bb * H
    dmat = jnp.concatenate([jnp.where(key <= qry, raw[:half], -jnp.inf),
                            jnp.where(key >= qry, raw[half:], -jnp.inf)], axis=0)
    m_inter = bq + m_old
    m_t = jnp.maximum(m_inter, jnp.max(dmat, axis=1, keepdims=True))
    w_state = jnp.exp2(m_inter - m_t)
    s = qk * jnp.exp2(dmat - m_t)
    denom = jnp.sum(s, axis=1, keepdims=True) + w_state * qn_r
    scale = 1.0 / jnp.maximum(jnp.abs(denom), jnp.exp2(-m_t))
    wscale = w_state * scale
    probs = s.astype(BF16)

    lw = b_edge + imb_r
    m_new = jnp.maximum(b_edge + m_old, jnp.max(lw, axis=-1, keepdims=True))
    ws = jnp.exp2(lw - m_new)
    dec = jnp.exp2(b_edge + m_old - m_new)

    def per_head_rows(rows, si):
        return jnp.concatenate([jnp.broadcast_to(rows[si * H + hd], (dh, L)) for hd in range(H)], axis=0)

    def per_head_lanes(vals, si):
        e = vals[si * H + H - 1]
        for hd in reversed(range(H - 1)):
            e = jnp.where(hlanes[hd], vals[si * H + hd], e)
        return e

    blockdiag = (row_w // dh) == (lane_w // dh)
    outs = []
    for si, blk in enumerate(blocks):
        k, vt = blk[K], blk[VT]
        v_heads = jnp.concatenate([jnp.where(hrows[hd], vt, zero_v) for hd in range(H)], axis=1)
        intra_t = _dot(v_heads, jnp.concatenate([probs[si * H + hd] for hd in range(H)], axis=0))
        out_t = intra_t * per_head_rows(scale, si) + inter_t[si] * per_head_rows(wscale, si)
        outs.append(out_t.T)

        dec_l = per_head_lanes(dec, si)
        vw = (vt.astype(F32) * per_head_rows(ws, si)).astype(BF16)
        c_new = c_ref[si] * dec_l + _dot(vw, k)
        c_ref[si] = jnp.where(blockdiag, c_new, 0.0)
        ws8 = jnp.concatenate([ws[si * H + hd] for hd in range(H)]
                              + [jnp.zeros((SUBLANE - H, L), F32)], axis=0).astype(BF16)
        n_ref[si] = n_ref[si] * dec_l + jnp.where(sel8, _dot(ws8, k), 0.0)
        m_ref[si, 0:H, :] = jnp.concatenate(
            [jnp.broadcast_to(m_new[si * H + hd], (1, LANE)) for hd in range(H)], axis=0)
    return outs


def _mlscan_kernel(qf_ref, kf_ref, vf_ref, gif_ref, gtf_ref, qb_ref, kb_ref, vb_ref, gib_ref, gtb_ref,
                   hf_ref, hb_ref, c_ref, n_ref, m_ref, *, bb):
    @pl.when(pl.program_id(1) == 0)
    def _():
        c_ref[...] = jnp.zeros_like(c_ref)
        n_ref[...] = jnp.zeros_like(n_ref)
        m_ref[...] = jnp.zeros_like(m_ref)

    fwd = [(qf_ref[b], kf_ref[b], vf_ref[b], gif_ref[b], gtf_ref[b]) for b in range(bb)]
    bwd = [(qb_ref[b], kb_ref[b], vb_ref[b], gib_ref[b], gtb_ref[b]) for b in range(bb)]
    outs = _mlstm_chunk(fwd + bwd, c_ref, n_ref, m_ref)
    for b in range(bb):
        hf_ref[b] = outs[b]
        hb_ref[b] = outs[bb + b]


def _mlstm_scan(qt, k, vt, gi, gt, L, bb):
    B, S, W = k.shape
    nc = S // L
    fw = lambda b, c: c
    bw = lambda b, c: nc - 1 - c
    def specs(cm):
        return [pl.BlockSpec((bb, W, L), lambda b, c: (b, 0, cm(b, c))),
                pl.BlockSpec((bb, L, W), lambda b, c: (b, cm(b, c), 0)),
                pl.BlockSpec((bb, W, L), lambda b, c: (b, 0, cm(b, c))),
                pl.BlockSpec((bb, L, LANE), lambda b, c: (b, cm(b, c), 0)),
                pl.BlockSpec((bb, GATE_ROWS, L), lambda b, c: (b, 0, cm(b, c)))]
    out = jax.ShapeDtypeStruct((B, S, W), F32)
    return pl.pallas_call(
        functools.partial(_mlscan_kernel, bb=bb),
        grid=(B // bb, nc),
        in_specs=specs(fw) + specs(bw),
        out_specs=[pl.BlockSpec((bb, L, W), lambda b, c: (b, c, 0)),
                   pl.BlockSpec((bb, L, W), lambda b, c: (b, nc - 1 - c, 0))],
        out_shape=[out, out],
        scratch_shapes=[pltpu.VMEM((2 * bb, W, W), F32), pltpu.VMEM((2 * bb, SUBLANE, W), F32),
                        pltpu.VMEM((2 * bb, SUBLANE, LANE), F32)],
        compiler_params=_cparams(("parallel", "arbitrary")),
        name="mlstm_scan",
    )(qt, k, vt, gi, gt, qt, k, vt, gi, gt)


def _fnet1_kernel(wr_ref, wi_ref, m1_ref, a_ref):
    n1 = wr_ref.shape[1]
    tc = wr_ref.shape[2] * wr_ref.shape[3]
    a = (_dot(m1_ref[:, 0:n1], wr_ref[0].reshape(n1, tc))
         + _dot(m1_ref[:, n1:], wi_ref[0].reshape(n1, tc)))
    a_ref[0, 0] = a[0:n1].astype(BF16)
    a_ref[0, 1] = a[n1:].astype(BF16)


def _fnet3_kernel(a_ref, tab_ref, w_ref, b_ref, o_ref):
    for j in range(a_ref.shape[2]):
        z = jnp.concatenate([a_ref[0, 0, j], a_ref[0, 1, j]], axis=0)
        y = _dot(tab_ref[j], z)
        o_ref[0, :, j, :] = _dot(y.astype(BF16), w_ref[...]) + b_ref[...]


def _fnet(fr, fi, m1, tab3, wbd, bias, layer, tc, kb):
    B, S, W = fr.shape
    n1 = FN_N1
    n2 = S // n1
    cols = n2 * W
    wr = fr.reshape(B, n1, n2, W)
    wi = fi.reshape(B, n1, n2, W)
    a = pl.pallas_call(
        _fnet1_kernel,
        grid=(B, cols // tc),
        in_specs=[pl.BlockSpec((1, n1, tc // W, W), lambda b, i: (b, 0, i, 0)),
                  pl.BlockSpec((1, n1, tc // W, W), lambda b, i: (b, 0, i, 0)),
                  _const_spec(m1.shape)],
        out_specs=pl.BlockSpec((1, 2, n1, tc), lambda b, i: (b, 0, 0, i)),
        out_shape=jax.ShapeDtypeStruct((B, 2, n1, cols), BF16),
        compiler_params=_cparams(("parallel", "parallel")),
        name="fnet_dft1",
    )(wr, wi, m1)
    a = a.reshape(B, 2, n1, n2, W)
    y = pl.pallas_call(
        _fnet3_kernel,
        grid=(B, n1 // kb),
        in_specs=[pl.BlockSpec((1, 2, kb, n2, W), lambda b, i: (b, 0, i, 0, 0)),
                  pl.BlockSpec((kb, n2, 2 * n2), lambda b, i: (i, 0, 0)),
                  _layer_spec(wbd, layer), _layer_spec(bias, layer)],
        out_specs=pl.BlockSpec((1, n2, kb, W), lambda b, i: (b, 0, i, 0)),
        out_shape=jax.ShapeDtypeStruct((B, n2, n1, W), F32),
        compiler_params=_cparams(("parallel", "parallel")),
        name="fnet_dft2",
    )(a, tab3, wbd, bias)
    return y.reshape(B, S, W)


def _post_kernel(h_ref, ya_ref, hf_ref, hb_ref, xc_ref, mo_ref, yf_ref, p_ref,
                 hn_ref, sk_ref, ones_ref, wo_ref, g2_ref, wg2_ref, wu2_ref, wd_ref,
                 gn_ref, wg_ref, wp_ref, pn_ref, fn_ref, o_ref, acc_ref, *, final, tf):
    tm = h_ref.shape[0]
    a_w = ya_ref.shape[1]
    m_w = a_w + ML_W
    groups = [slice(r, r + tm // ROW_GROUPS) for r in range(0, tm, tm // ROW_GROUPS)]

    def project(r):
        hm = hf_ref[r, :] + hb_ref[r, :]
        sq = hm * hm
        hi = sq.astype(BF16)
        lo = (sq - hi.astype(F32)).astype(BF16)
        ms = (_dot(hi, ones_ref[...]) + _dot(lo, ones_ref[...])) * (1.0 / ML_DIM)
        ym = (hm * lax.rsqrt(ms + EPS) * hn_ref[...] + sk_ref[...] * xc_ref[r, :]) * mo_ref[r, :]
        return (h_ref[r, :] + _dot(ya_ref[r, :], wo_ref[0:a_w, :])
                + _dot(ym.astype(BF16), wo_ref[a_w:m_w, :])
                + _dot(yf_ref[r, :].astype(BF16), wo_ref[m_w:, :]))

    h1 = jnp.concatenate([project(r) for r in groups], axis=0)
    h2 = h1 + _swiglu_half(h1, g2_ref, wg2_ref, wu2_ref, wd_ref, acc_ref, tf)

    for r in groups:
        h2r = h2[r, :]
        e = _rms(_dot(p_ref[r, :].astype(BF16), wp_ref[...]), pn_ref[...])
        gate = jax.nn.sigmoid(_dot(_rms(h2r, gn_ref[...]).astype(BF16), wg_ref[...]))
        h3 = h2r + gate * e
        if final:
            h3 = _rms(h3, fn_ref[...])
        o_ref[r, :] = h3


def _post(h, ya, hf, hb, xc, mo, yf, p, ones_bd, params, final_norm, layer, tm, tf, final):
    T, D = h.shape
    tok = lambda w: pl.BlockSpec((tm, w), lambda i: (i, 0))
    acts = [h, ya, hf, hb, xc, mo, yf]
    hn, sk, wo, g2, wg2, wu2, wd, gn, wg, wp, pn = params
    consts = [hn, sk, ones_bd, wo, g2, wg2, wu2, wd, gn, wg, wp, pn, final_norm]
    shared = (2, len(consts) - 1)
    specs = [_const_spec(c.shape) if j in shared else _layer_spec(c, layer) for j, c in enumerate(consts)]
    return pl.pallas_call(
        functools.partial(_post_kernel, final=final, tf=tf),
        grid=(T // tm,),
        in_specs=([tok(a.shape[1]) for a in acts]
                  + [pl.BlockSpec((None, tm, p.shape[2]), lambda i: (layer, i, 0))] + specs),
        out_specs=tok(D),
        out_shape=jax.ShapeDtypeStruct((T, D), F32),
        scratch_shapes=[pltpu.VMEM((tm, D), F32)],
        compiler_params=_cparams(("parallel",)),
        name="post",
    )(*acts, p, *consts)


def _block_diag(w):
    *lead, G, a, b = w.shape
    eye = jnp.eye(G, dtype=w.dtype)
    return (eye[:, None, :, None] * w[..., :, :, None, :]).reshape(*lead, G * a, G * b)


def _dft_tables(S):
    n1, n2 = FN_N1, S // FN_N1
    c = np.arange(FN_DIM)
    ang = 2.0 * np.pi * np.outer(c, c) / FN_DIM
    eye = np.eye(FN_GROUPS)
    chan = np.concatenate([np.kron(eye, np.cos(ang)), -np.kron(eye, np.sin(ang))], axis=1) / np.sqrt(FN_DIM)
    k1 = np.arange(n1)
    a1 = 2.0 * np.pi * np.outer(k1, k1) / n1
    m1 = np.block([[np.cos(a1), np.sin(a1)], [-np.sin(a1), np.cos(a1)]])
    k = k1[:, None, None] + n1 * np.arange(n2)[None, :, None]
    s2 = np.arange(n2)[None, None, :]
    a3 = 2.0 * np.pi * ((k * s2) % S) / S
    tab3 = np.concatenate([np.cos(a3), np.sin(a3)], axis=-1) / np.sqrt(S)
    bf = lambda t: jnp.asarray(t, dtype=F32).astype(BF16)
    return bf(chan), bf(m1), bf(tab3)


def _rope_consts():
    half = MLA_ROPE // 2
    inv = 1.0 / (ROPE_THETA ** (jnp.arange(0, MLA_ROPE, 2, dtype=F32) / MLA_ROPE))
    z = jnp.zeros((half,), F32)
    o = jnp.ones((half,), F32)
    rows = [jnp.concatenate([inv, inv, z, z]), jnp.concatenate([o, o, z, z]),
            jnp.concatenate([z, o, z, z]), jnp.concatenate([-o, z, z, z])]
    rows += [jnp.zeros((LANE,), F32)] * (SUBLANE - len(rows))
    return jnp.stack(rows)


def _tiles(B, S):
    T = B * S
    return dict(tm=min(512, T), tf=MXU_DIM, tmix=min(1024, S), tq=min(2048, S), tk=min(1024, S), attn_unroll=2,
                chunk=min(256, S), bb=2 if B % 2 == 0 else 1,
                tc=min(8192, (S // FN_N1) * FN_W), kb=2 * SUBLANE,
                trope=min(1024, T))


def kernel(x, p, positions, ffn1_norm, ffn1_w_gate, ffn1_w_up, ffn1_w_down, mix_norm, w_in, mla_q_norm, mla_w_uq, mla_kv_norm, mla_w_ukv, mlstm_conv_w, mlstm_conv_b, mlstm_w_q, mlstm_w_k, mlstm_i_bias, mlstm_f_bias, mlstm_head_norm, mlstm_skip, fnet_w, fnet_b, w_out, ffn2_norm, ffn2_w_gate, ffn2_w_up, ffn2_w_down, ple_gate_norm, ple_w_gate, ple_w_proj, ple_post_norm, final_norm):
    B, S, D = x.shape
    depth = p.shape[0]
    T = B * S
    t = _tiles(B, S)
    Ld = depth
    rows = lambda a: a.reshape(Ld, 1, -1).astype(F32)

    tab = _rope_consts()
    cos_t, sin_a, sin_b = [a.reshape(B, S, LANE) for a in
                           _rope_tables(positions.reshape(T, 1).astype(jnp.int32), tab, t["trope"])]
    dft_chan, dft_m1, dft_tab3 = _dft_tables(S)
    ones_bd = _block_diag(jnp.ones((ML_HEADS, ML_DIM, ML_DIM), BF16))
    qscale = float((MLA_NOPE + MLA_ROPE) ** -0.5 * LOG2E)

    wg1, wu1, wd1 = ffn1_w_gate.astype(BF16), ffn1_w_up.astype(BF16), ffn1_w_down.astype(BF16)
    wg2, wu2, wd2 = ffn2_w_gate.astype(BF16), ffn2_w_up.astype(BF16), ffn2_w_down.astype(BF16)
    o_kr = MLA_Q_LORA + MLA_KV_LORA
    o_mx = o_kr + MLA_ROPE
    o_g = o_mx + 3 * ML_W
    o_f = o_g + GATE_W
    wi = w_in.astype(BF16)
    zpad = lambda n: jnp.zeros((Ld, D, n), BF16)
    win = jnp.concatenate([wi[..., :o_kr], wi[..., o_mx:o_g], wi[..., o_f:], wi[..., o_kr:o_mx],
                           wi[..., o_g:o_f], zpad(LANE - MLA_ROPE - GATE_W)], axis=-1)
    wuq = jnp.pad(mla_w_uq.astype(BF16).reshape(Ld, MLA_Q_LORA, MLA_HEADS, MLA_NOPE + MLA_ROPE),
                  ((0, 0), (0, 0), (0, 0), (0, MLA_QK_PAD - MLA_NOPE - MLA_ROPE)))
    wuq = wuq.reshape(Ld, MLA_Q_LORA, MLA_HEADS * MLA_QK_PAD)
    wukv = mla_w_ukv.astype(BF16).reshape(Ld, MLA_KV_LORA, MLA_HEADS, 2, MLA_NOPE)
    wukv = wukv.transpose(0, 1, 3, 2, 4).reshape(Ld, MLA_KV_LORA, 2 * MLA_HEADS * MLA_NOPE)
    wqk = jnp.concatenate([_block_diag(mlstm_w_q) * (ML_DIM ** -0.5), _block_diag(mlstm_w_k)],
                          axis=-1).astype(BF16)
    gbias = jnp.concatenate([jnp.zeros((Ld, G_OFF), F32), mlstm_i_bias.reshape(Ld, -1),
                             mlstm_f_bias.reshape(Ld, -1), jnp.zeros((Ld, LANE - G_OFF - GATE_W), F32)],
                            axis=-1).reshape(Ld, 1, LANE)
    fw_bd = _block_diag(fnet_w).astype(BF16)
    post_params = [rows(mlstm_head_norm), rows(mlstm_skip), w_out.astype(BF16), rows(ffn2_norm), wg2, wu2, wd2,
                   rows(ple_gate_norm), ple_w_gate.astype(BF16), ple_w_proj.astype(BF16),
                   rows(ple_post_norm)]
    g1, gmix, gq, gkv = rows(ffn1_norm), rows(mix_norm), rows(mla_q_norm), rows(mla_kv_norm)
    conv_w, conv_b, fnet_bias = mlstm_conv_w.astype(F32), rows(mlstm_conv_b), rows(fnet_b)
    p_flat = p.reshape(Ld, T, p.shape[-1])
    fin = final_norm.reshape(1, -1).astype(F32)
    flat = lambda a: a.reshape(T, a.shape[-1])

    h = x.reshape(T, D)
    for i in range(depth):
        h = _ffn(h, g1, wg1, wu1, wd1, i, 2 * t["tm"], t["tf"])
        qt, k, vt, mv, mo, fr, fi, xc, mqt, mk, gi, gt = _mixin(
            h.reshape(B, S, D), cos_t, sin_a, sin_b, gmix, win, gq, wuq, gkv, wukv, conv_w, conv_b, wqk,
            gbias, dft_chan, i, t["tmix"], qscale, t["chunk"])
        y_mla = _attention(qt, k, vt, t["tq"], t["tk"], t["attn_unroll"])
        hf, hb = _mlstm_scan(mqt, mk, mv, gi, gt, t["chunk"], t["bb"])
        y_fnet = _fnet(fr, fi, dft_m1, dft_tab3, fw_bd, fnet_bias, i, t["tc"], t["kb"])
        h = _post(h, flat(y_mla), flat(hf), flat(hb), flat(xc), flat(mo), flat(y_fnet), p_flat, ones_bd,
                  post_params, fin, i, t["tm"], t["tf"], final=(i == depth - 1))
    return h.reshape(B, S, D)
```

```python
import functools

import numpy as np
import jax
import jax.numpy as jnp
from jax import lax
from jax.experimental import pallas as pl
from jax.experimental.pallas import tpu as pltpu

F32 = jnp.float32
BF16 = jnp.bfloat16

EPS = 1e-6
ROPE_THETA = 10000.0
LOG2E = float(np.log2(np.e))
LANE = 128
SUBLANE = 8
MXU_DIM = 256
VMEM_LIMIT = 56 * 1024 * 1024

MLA_HEADS = 4
MLA_Q_LORA = 384
MLA_KV_LORA = 256
MLA_NOPE = 128
MLA_ROPE = 64
MLA_V = 128
MLA_QK_PAD = MXU_DIM
ML_HEADS = 4
ML_DIM = 64
ML_W = ML_HEADS * ML_DIM
ML_CONV = 5
FN_GROUPS = 4
FN_DIM = 64
FN_W = FN_GROUPS * FN_DIM
FN_N1 = 64
GATE_W = 4 * ML_HEADS
GATE_ROWS = 6 * ML_HEADS
ROW_GROUPS = 2

U_CQ = 0
U_CKV = U_CQ + MLA_Q_LORA
U_MX = U_CKV + MLA_KV_LORA
U_MV = U_MX + ML_W
U_MO = U_MV + ML_W
U_FIN = U_MO + ML_W
U_KR = U_FIN + FN_W
G_OFF = MLA_ROPE
U_TOT = U_KR + LANE


def _cparams(sem):
    return pltpu.CompilerParams(dimension_semantics=sem, vmem_limit_bytes=VMEM_LIMIT)


def _const_spec(shape):
    nd = len(shape)
    return pl.BlockSpec(shape, lambda *_: (0,) * nd, pipeline_mode=pl.Buffered(1))


def _layer_spec(a, layer):
    nd = a.ndim - 1
    return pl.BlockSpec((None,) + a.shape[1:], lambda *_: (layer,) + (0,) * nd,
                        pipeline_mode=pl.Buffered(1))


def _rms(x, g):
    ms = jnp.mean(x * x, axis=-1, keepdims=True)
    return x * lax.rsqrt(ms + EPS) * g


def _dot(a, b):
    return jnp.dot(a, b, preferred_element_type=F32)


def _swiglu_half(x, g_ref, wg_ref, wu_ref, wd_ref, acc_ref, tf):
    xn = _rms(x, g_ref[...]).astype(BF16)
    for c in range(wd_ref.shape[0] // tf):
        g = _dot(xn, wg_ref[:, c * tf:(c + 1) * tf])
        u = _dot(xn, wu_ref[:, c * tf:(c + 1) * tf])
        a = (g * jax.nn.sigmoid(g) * u).astype(BF16)
        part = _dot(a, wd_ref[c * tf:(c + 1) * tf, :])
        if c == 0:
            acc_ref[...] = part
        else:
            acc_ref[...] += part
    return 0.5 * acc_ref[...]


def _ffn_kernel(h_ref, g_ref, wg_ref, wu_ref, wd_ref, o_ref, acc_ref, *, tf):
    x = h_ref[...]
    o_ref[...] = x + _swiglu_half(x, g_ref, wg_ref, wu_ref, wd_ref, acc_ref, tf)


def _ffn(h, g, wg, wu, wd, layer, tm, tf):
    T, D = h.shape
    return pl.pallas_call(
        functools.partial(_ffn_kernel, tf=tf),
        grid=(T // tm,),
        in_specs=[pl.BlockSpec((tm, D), lambda i: (i, 0)),
                  _layer_spec(g, layer), _layer_spec(wg, layer), _layer_spec(wu, layer),
                  _layer_spec(wd, layer)],
        out_specs=pl.BlockSpec((tm, D), lambda i: (i, 0)),
        out_shape=jax.ShapeDtypeStruct((T, D), F32),
        scratch_shapes=[pltpu.VMEM((tm, D), F32)],
        compiler_params=_cparams(("parallel",)),
        name="ffn",
    )(h, g, wg, wu, wd)


def _rope_kernel(pos_ref, tab_ref, cos_ref, sina_ref, sinb_ref):
    ang = pos_ref[...].astype(F32) * tab_ref[0:1, :]
    c = jnp.cos(ang)
    s = jnp.sin(ang)
    cos_ref[...] = c * tab_ref[1:2, :]
    sina_ref[...] = s * tab_ref[2:3, :]
    sinb_ref[...] = s * tab_ref[3:4, :]


def _rope_tables(pos, tab, tm):
    T = pos.shape[0]
    out = jax.ShapeDtypeStruct((T, LANE), F32)
    spec = pl.BlockSpec((tm, LANE), lambda i: (i, 0))
    return pl.pallas_call(
        _rope_kernel,
        grid=(T // tm,),
        in_specs=[pl.BlockSpec((tm, 1), lambda i: (i, 0)), _const_spec(tab.shape)],
        out_specs=[spec, spec, spec],
        out_shape=[out, out, out],
        compiler_params=_cparams(("parallel",)),
        name="rope_tables",
    )(pos, tab)


def _rope(blk, cos_t, sin_a, sin_b):
    half = MLA_ROPE // 2
    return (blk * cos_t + pltpu.roll(blk, half, 1) * sin_a
            + pltpu.roll(blk, LANE - half, 1) * sin_b)


def _mixin_kernel(h_ref, hprev_ref, hnext_ref, cos_ref, sina_ref, sinb_ref, g_ref, win_ref, qn_ref, wuq_ref,
                  kvn_ref, wukv_ref, cw_ref, cb_ref, wqk_ref, gb_ref, dft_ref,
                  qt_ref, k_ref, vt_ref, mv_ref, mo_ref, fr_ref, fi_ref, xc_ref, mqt_ref, mk_ref, gi_ref,
                  gt_ref, ext_ref, gate_ref, *, qscale, chunk):
    tm = h_ref.shape[1]
    halo = SUBLANE
    for r0 in range(0, tm, tm // ROW_GROUPS):
        r = slice(r0, r0 + tm // ROW_GROUPS)
        x = h_ref[0, r, :]
        xn = _rms(x, g_ref[...]).astype(BF16)
        u = _dot(xn, win_ref[...])
        cos_t, sin_a, sin_b = cos_ref[0, r, :], sina_ref[0, r, :], sinb_ref[0, r, :]

        cq = _rms(u[:, U_CQ:U_CKV], qn_ref[...]).astype(BF16)
        q = _dot(cq, wuq_ref[...])
        for hd in range(MLA_HEADS):
            o = hd * MLA_QK_PAD
            qt_ref[0, hd, 0:MLA_NOPE, r] = (q[:, o:o + MLA_NOPE] * qscale).T.astype(BF16)
            qr = _rope(q[:, o + MLA_NOPE:o + MLA_QK_PAD], cos_t, sin_a, sin_b)
            qt_ref[0, hd, MLA_NOPE:MLA_QK_PAD, r] = (qr * qscale).T.astype(BF16)

        ckv = _rms(u[:, U_CKV:U_MX], kvn_ref[...]).astype(BF16)
        kv = _dot(ckv, wukv_ref[...])
        kr = _rope(u[:, U_KR:U_KR + LANE], cos_t, sin_a, sin_b).astype(BF16)
        for hd in range(MLA_HEADS):
            o = hd * MLA_QK_PAD
            k_ref[0, r, o:o + MLA_NOPE] = kv[:, hd * MLA_NOPE:(hd + 1) * MLA_NOPE].astype(BF16)
            k_ref[0, r, o + MLA_NOPE:o + MLA_QK_PAD] = kr
            vo = (MLA_HEADS + hd) * MLA_NOPE
            vt_ref[0, hd, :, r] = kv[:, vo:vo + MLA_V].T.astype(BF16)

        ext_ref[halo + r0:halo + r0 + tm // ROW_GROUPS, :] = u[:, U_MX:U_MV]
        mv_ref[0, :, r] = u[:, U_MV:U_MO].T.astype(BF16)
        mo_ref[0, r, :] = jax.nn.sigmoid(u[:, U_MO:U_FIN])
        gate_ref[r, :] = u[:, U_KR:U_TOT]

        f = _dot(u[:, U_FIN:U_KR].astype(BF16), dft_ref[...])
        fr_ref[0, r, :] = f[:, :FN_W].astype(BF16)
        fi_ref[0, r, :] = f[:, FN_W:].astype(BF16)

    i = pl.program_id(1)
    xh = jnp.concatenate([hprev_ref[0], hnext_ref[0]], axis=0)
    mh = _dot(_rms(xh, g_ref[...]).astype(BF16), win_ref[:, U_MX:U_MV])
    ext_ref[0:halo, :] = jnp.where(i > 0, mh[0:halo], 0.0)
    ext_ref[halo + tm:, :] = jnp.where(i < pl.num_programs(1) - 1, mh[halo:], 0.0)
    _mlstm_pre(ext_ref, gate_ref[...], cw_ref, cb_ref, wqk_ref, gb_ref,
               xc_ref, mqt_ref, mk_ref, gi_ref, gt_ref, chunk)


def _mixin(h3, cos_t, sin_a, sin_b, g, win, qn, wuq, kvn, wukv, cw, cb, wqk, gb, dft, layer, tm, qscale, chunk):
    B, S, D = h3.shape
    nblk = tm // SUBLANE
    last = S // SUBLANE - 1
    tok = lambda w: pl.BlockSpec((1, tm, w), lambda b, i: (b, i, 0))
    tcol = lambda w: pl.BlockSpec((1, w, tm), lambda b, i: (b, 0, i))
    outs = [
        (jax.ShapeDtypeStruct((B, MLA_HEADS, MLA_QK_PAD, S), BF16),
         pl.BlockSpec((1, MLA_HEADS, MLA_QK_PAD, tm), lambda b, i: (b, 0, 0, i))),
        (jax.ShapeDtypeStruct((B, S, MLA_HEADS * MLA_QK_PAD), BF16), tok(MLA_HEADS * MLA_QK_PAD)),
        (jax.ShapeDtypeStruct((B, MLA_HEADS, MLA_V, S), BF16),
         pl.BlockSpec((1, MLA_HEADS, MLA_V, tm), lambda b, i: (b, 0, 0, i))),
        (jax.ShapeDtypeStruct((B, ML_W, S), BF16), tcol(ML_W)),
        (jax.ShapeDtypeStruct((B, S, ML_W), F32), tok(ML_W)),
        (jax.ShapeDtypeStruct((B, S, FN_W), BF16), tok(FN_W)),
        (jax.ShapeDtypeStruct((B, S, FN_W), BF16), tok(FN_W)),
        (jax.ShapeDtypeStruct((B, S, ML_W), F32), tok(ML_W)),
        (jax.ShapeDtypeStruct((B, ML_W, S), BF16), tcol(ML_W)),
        (jax.ShapeDtypeStruct((B, S, ML_W), BF16), tok(ML_W)),
        (jax.ShapeDtypeStruct((B, S, LANE), F32), tok(LANE)),
        (jax.ShapeDtypeStruct((B, GATE_ROWS, S), F32), tcol(GATE_ROWS)),
    ]
    consts = [g, win, qn, wuq, kvn, wukv, cw, cb, wqk, gb]
    return pl.pallas_call(
        functools.partial(_mixin_kernel, qscale=qscale, chunk=chunk),
        grid=(B, S // tm),
        in_specs=([tok(D),
                   pl.BlockSpec((1, SUBLANE, D), lambda b, i: (b, jnp.maximum(i * nblk - 1, 0), 0)),
                   pl.BlockSpec((1, SUBLANE, D), lambda b, i: (b, jnp.minimum((i + 1) * nblk, last), 0)),
                   tok(LANE), tok(LANE), tok(LANE)] + [_layer_spec(c, layer) for c in consts]
                  + [_const_spec(dft.shape)]),
        out_specs=[o[1] for o in outs],
        out_shape=[o[0] for o in outs],
        scratch_shapes=[pltpu.VMEM((tm + 2 * SUBLANE, ML_W), F32), pltpu.VMEM((tm, LANE), F32)],
        compiler_params=_cparams(("parallel", "parallel")),
        name="mix_in",
    )(h3, h3, h3, cos_t, sin_a, sin_b, *consts, dft)


def _attn_kernel(qt_ref, k_ref, vt_ref, o_ref, acc_ref, s_ref, *, tk, unroll):
    qt = qt_ref[0, 0]
    S = k_ref.shape[1]
    n = S // tk
    tq = qt.shape[1]
    acc_ref[...] = jnp.zeros_like(acc_ref)

    def produce(j, slot):
        ks = pl.multiple_of(j * tk, tk)
        st = _dot(k_ref[0, pl.ds(ks, tk), :], qt)
        s_ref[slot] = st
        return jnp.max(st, axis=0, keepdims=True)

    def consume(j, slot, cmax, m_old, l_old):
        ks = pl.multiple_of(j * tk, tk)
        m_new = jnp.maximum(m_old, cmax)
        p = jnp.exp2(s_ref[slot] - m_new)
        alpha = jnp.exp2(m_old - m_new)
        l_new = alpha * l_old + jnp.sum(p, axis=0, keepdims=True)
        acc_ref[...] = alpha * acc_ref[...] + _dot(vt_ref[0, 0, :, pl.ds(ks, tk)], p.astype(BF16))
        return m_new, l_new

    def body(i, carry):
        cmax, m, l = carry
        for u in range(unroll):
            j = unroll * i + u
            nxt = produce(j + 1, (u + 1) % 2)
            m, l = consume(j, u % 2, cmax, m, l)
            cmax = nxt
        return cmax, m, l

    trips = (n - 1) // unroll
    carry = (produce(0, 0), jnp.full((1, tq), -jnp.inf, F32), jnp.zeros((1, tq), F32))
    cmax, m, l = lax.fori_loop(0, trips, body, carry)
    for j in range(unroll * trips, n):
        nxt = produce(j + 1, (j + 1) % 2) if j + 1 < n else None
        m, l = consume(j, j % 2, cmax, m, l)
        cmax = nxt
    o_ref[0] = (acc_ref[...] / l).T.astype(o_ref.dtype)


def _attention(qt, k, vt, tq, tk, unroll):
    B, S, _ = k.shape
    assert S % tq == 0 and S % tk == 0 and unroll % 2 == 0, "logit buffers alternate by chunk parity"
    return pl.pallas_call(
        functools.partial(_attn_kernel, tk=tk, unroll=unroll),
        grid=(B, MLA_HEADS, S // tq),
        in_specs=[pl.BlockSpec((1, 1, MLA_QK_PAD, tq), lambda b, h, i: (b, h, 0, i)),
                  pl.BlockSpec((1, S, MLA_QK_PAD), lambda b, h, i: (b, 0, h)),
                  pl.BlockSpec((1, 1, MLA_V, S), lambda b, h, i: (b, h, 0, 0))],
        out_specs=pl.BlockSpec((1, tq, MLA_V), lambda b, h, i: (b, i, h)),
        out_shape=jax.ShapeDtypeStruct((B, S, MLA_HEADS * MLA_V), BF16),
        scratch_shapes=[pltpu.VMEM((MLA_V, tq), F32), pltpu.VMEM((2, tk, tq), F32)],
        compiler_params=_cparams(("parallel", "parallel", "arbitrary")),
        name="mla_attention",
    )(qt, k, vt)


def _mlstm_pre(ext_ref, gate, cw_ref, cb_ref, wqk_ref, gb_ref, xc_ref, qt_ref, k_ref, gi_ref, gt_ref, chunk):
    halo = SUBLANE
    ts = ext_ref.shape[0] - 2 * halo
    pad = ML_CONV // 2
    xc = cb_ref[...] + cw_ref[0:1, :] * ext_ref[halo - pad:halo - pad + ts, :]
    for j in range(1, ML_CONV):
        o = halo - pad + j
        xc = xc + cw_ref[j:j + 1, :] * ext_ref[o:o + ts, :]
    xc = xc * jax.nn.sigmoid(xc)
    xc_ref[0] = xc
    qk = _dot(xc.astype(BF16), wqk_ref[...])
    qt_ref[0] = qk[:, :ML_W].T.astype(BF16)
    k_ref[0] = qk[:, ML_W:].astype(BF16)
    g_t = (gate + gb_ref[...]).T[G_OFF:G_OFF + GATE_W, :]
    lf_t = jnp.minimum(g_t, 0.0) - jnp.log(1.0 + jnp.exp(-jnp.abs(g_t)))
    p1 = lf_t.astype(BF16)
    r1 = lf_t - p1.astype(F32)
    p2 = r1.astype(BF16)
    p3 = (r1 - p2.astype(F32)).astype(BF16)
    r = lax.broadcasted_iota(jnp.int32, (chunk, chunk), 0)
    c = lax.broadcasted_iota(jnp.int32, (chunk, chunk), 1)
    tri_f = (r <= c).astype(BF16)
    tri_b = (r >= c).astype(BF16)
    sub = lax.broadcasted_iota(jnp.int32, (GATE_W, chunk), 0)
    cols = []
    for k in range(ts // chunk):
        sl = slice(k * chunk, (k + 1) * chunk)
        cum_f = _dot(p1[:, sl], tri_f) + _dot(p2[:, sl], tri_f) + _dot(p3[:, sl], tri_f)
        cum_b = _dot(p1[:, sl], tri_b) + _dot(p2[:, sl], tri_b) + _dot(p3[:, sl], tri_b)
        cols.append(jnp.where(sub < 2 * ML_HEADS, g_t[:, sl], jnp.where(sub < 3 * ML_HEADS, cum_f, cum_b)))
    gt = jnp.concatenate(cols, axis=1) * LOG2E
    gt = jnp.concatenate([gt, gt[0:2 * ML_HEADS] - gt[2 * ML_HEADS:]], axis=0)
    gt_ref[0] = gt
    gi_ref[0] = jnp.concatenate([gt, jnp.zeros((LANE - GATE_ROWS, ts), F32)], axis=0).T


def _mlstm_chunk(blocks, c_ref, n_ref, m_ref):
    nslot = len(blocks)
    bb = nslot // 2
    L = blocks[0][1].shape[0]
    H, dh = ML_HEADS, ML_DIM
    key = lax.broadcasted_iota(jnp.int32, (L, L), 0)
    qry = lax.broadcasted_iota(jnp.int32, (L, L), 1)
    lane_w = lax.broadcasted_iota(jnp.int32, (1, ML_W), 1)
    row_w = lax.broadcasted_iota(jnp.int32, (ML_W, 1), 0)
    hlanes = [(lane_w // dh) == hd for hd in range(H)]
    hrows = [(row_w // dh) == hd for hd in range(H)]
    zero_k = jnp.zeros_like(blocks[0][1])
    zero_v = jnp.zeros_like(blocks[0][2])
    sel8 = (lax.broadcasted_iota(jnp.int32, (SUBLANE, ML_W), 0)
            == lax.broadcasted_iota(jnp.int32, (SUBLANE, ML_W), 1) // dh)
    probs_of = [(si, hd) for si in range(nslot) for hd in range(H)]
    direction = lambda si: si // bb
    gate = lambda kind, si, hd: kind * 2 * H + direction(si) * H + hd
    edge = lambda si: L - 1 if direction(si) == 0 else 0

    QT, K, VT, GI, GT = range(5)
    inter_t, qn_t = [], []
    for si, blk in enumerate(blocks):
        inter_t.append(_dot(c_ref[si].astype(BF16), blk[QT]))
        qn_t.append(_dot(n_ref[si].astype(BF16), blk[QT]))

    def rows_of(kind):
        return jnp.stack([blocks[si][GT][gate(kind, si, hd):gate(kind, si, hd) + 1, :]
                          for si, hd in probs_of])

    m_old = jnp.stack([m_ref[si, hd:hd + 1, 0:1] for si, hd in probs_of])
    bq = rows_of(1)
    imb_r = rows_of(2)
    imb_c = jnp.stack([blocks[si][GI][:, gate(2, si, hd):gate(2, si, hd) + 1]
                       for si, hd in probs_of])
    b_edge = jnp.stack([blocks[si][GT][gate(1, si, hd):gate(1, si, hd) + 1, edge(si):edge(si) + 1]
                        for si, hd in probs_of])
    qn_r = jnp.stack([qn_t[si][hd:hd + 1, :] for si, hd in probs_of])
    qk = jnp.stack([_dot(jnp.where(hlanes[hd], blocks[si][K], zero_k), blocks[si][QT])
                    for si, hd in probs_of])

    raw = imb_c + bq
    half = bb * H
    dmat = jnp.concatenate([jnp.where(key <= qry, raw[:half], -jnp.inf),
                            jnp.where(key >= qry, raw[half:], -jnp.inf)], axis=0)
    m_inter = bq + m_old
    m_t = jnp.maximum(m_inter, jnp.max(dmat, axis=1, keepdims=True))
    w_state = jnp.exp2(m_inter - m_t)
    s = qk * jnp.exp2(dmat - m_t)
    denom = jnp.sum(s, axis=1, keepdims=True) + w_state * qn_r
    scale = 1.0 / jnp.maximum(jnp.abs(denom), jnp.exp2(-m_t))
    wscale = w_state * scale
    probs = s.astype(BF16)

    lw = b_edge + imb_r
    m_new = jnp.maximum(b_edge + m_old, jnp.max(lw, axis=-1, keepdims=True))
    ws = jnp.exp2(lw - m_new)
    dec = jnp.exp2(b_edge + m_old - m_new)

    def per_head_rows(rows, si):
        return jnp.concatenate([jnp.broadcast_to(rows[si * H + hd], (dh, L)) for hd in range(H)], axis=0)

    def per_head_lanes(vals, si):
        e = vals[si * H + H - 1]
        for hd in reversed(range(H - 1)):
            e = jnp.where(hlanes[hd], vals[si * H + hd], e)
        return e

    blockdiag = (row_w // dh) == (lane_w // dh)
    outs = []
    for si, blk in enumerate(blocks):
        k, vt = blk[K], blk[VT]
        v_heads = jnp.concatenate([jnp.where(hrows[hd], vt, zero_v) for hd in range(H)], axis=1)
        intra_t = _dot(v_heads, jnp.concatenate([probs[si * H + hd] for hd in range(H)], axis=0))
        out_t = intra_t * per_head_rows(scale, si) + inter_t[si] * per_head_rows(wscale, si)
        outs.append(out_t.T)

        dec_l = per_head_lanes(dec, si)
        vw = (vt.astype(F32) * per_head_rows(ws, si)).astype(BF16)
        c_new = c_ref[si] * dec_l + _dot(vw, k)
        c_ref[si] = jnp.where(blockdiag, c_new, 0.0)
        ws8 = jnp.concatenate([ws[si * H + hd] for hd in range(H)]
                              + [jnp.zeros((SUBLANE - H, L), F32)], axis=0).astype(BF16)
        n_ref[si] = n_ref[si] * dec_l + jnp.where(sel8, _dot(ws8, k), 0.0)
        m_ref[si, 0:H, :] = jnp.concatenate(
            [jnp.broadcast_to(m_new[si * H + hd], (1, LANE)) for hd in range(H)], axis=0)
    return outs


def _mlscan_kernel(qf_ref, kf_ref, vf_ref, gif_ref, gtf_ref, qb_ref, kb_ref, vb_ref, gib_ref, gtb_ref,
                   hf_ref, hb_ref, c_ref, n_ref, m_ref, *, bb):
    @pl.when(pl.program_id(1) == 0)
    def _():
        c_ref[...] = jnp.zeros_like(c_ref)
        n_ref[...] = jnp.zeros_like(n_ref)
        m_ref[...] = jnp.zeros_like(m_ref)

    fwd = [(qf_ref[b], kf_ref[b], vf_ref[b], gif_ref[b], gtf_ref[b]) for b in range(bb)]
    bwd = [(qb_ref[b], kb_ref[b], vb_ref[b], gib_ref[b], gtb_ref[b]) for b in range(bb)]
    outs = _mlstm_chunk(fwd + bwd, c_ref, n_ref, m_ref)
    for b in range(bb):
        hf_ref[b] = outs[b]
        hb_ref[b] = outs[bb + b]


def _mlstm_scan(qt, k, vt, gi, gt, L, bb):
    B, S, W = k.shape
    nc = S // L
    fw = lambda b, c: c
    bw = lambda b, c: nc - 1 - c
    def specs(cm):
        return [pl.BlockSpec((bb, W, L), lambda b, c: (b, 0, cm(b, c))),
                pl.BlockSpec((bb, L, W), lambda b, c: (b, cm(b, c), 0)),
                pl.BlockSpec((bb, W, L), lambda b, c: (b, 0, cm(b, c))),
                pl.BlockSpec((bb, L, LANE), lambda b, c: (b, cm(b, c), 0)),
                pl.BlockSpec((bb, GATE_ROWS, L), lambda b, c: (b, 0, cm(b, c)))]
    out = jax.ShapeDtypeStruct((B, S, W), F32)
    return pl.pallas_call(
        functools.partial(_mlscan_kernel, bb=bb),
        grid=(B // bb, nc),
        in_specs=specs(fw) + specs(bw),
        out_specs=[pl.BlockSpec((bb, L, W), lambda b, c: (b, c, 0)),
                   pl.BlockSpec((bb, L, W), lambda b, c: (b, nc - 1 - c, 0))],
        out_shape=[out, out],
        scratch_shapes=[pltpu.VMEM((2 * bb, W, W), F32), pltpu.VMEM((2 * bb, SUBLANE, W), F32),
                        pltpu.VMEM((2 * bb, SUBLANE, LANE), F32)],
        compiler_params=_cparams(("parallel", "arbitrary")),
        name="mlstm_scan",
    )(qt, k, vt, gi, gt, qt, k, vt, gi, gt)


def _fnet_kernel(wr_ref, wi_ref, m1_ref, tab_ref, w_ref, b_ref, o_ref, a_ref, *, rows):
    n1, n2, W = wr_ref.shape[1:]
    for r0 in range(0, n2, rows):
        xr = wr_ref[0, :, r0:r0 + rows, :].reshape(n1, rows * W)
        xi = wi_ref[0, :, r0:r0 + rows, :].reshape(n1, rows * W)
        a = _dot(m1_ref[:, 0:n1], xr) + _dot(m1_ref[:, n1:], xi)
        a_ref[:, r0:r0 + rows, :] = a.astype(BF16).reshape(2 * n1, rows, W)
    for k1 in range(n1):
        z = jnp.concatenate([a_ref[k1], a_ref[n1 + k1]], axis=0)
        y = _dot(tab_ref[k1], z)
        o_ref[0, :, k1, :] = _dot(y.astype(BF16), w_ref[...]) + b_ref[...]


def _fnet(fr, fi, m1, tab3, wbd, bias, layer, rows):
    B, S, W = fr.shape
    n1 = FN_N1
    n2 = S // n1
    blk = pl.BlockSpec((1, n1, n2, W), lambda b: (b, 0, 0, 0))
    y = pl.pallas_call(
        functools.partial(_fnet_kernel, rows=rows),
        grid=(B,),
        in_specs=[blk, blk, _const_spec(m1.shape), _const_spec(tab3.shape),
                  _layer_spec(wbd, layer), _layer_spec(bias, layer)],
        out_specs=pl.BlockSpec((1, n2, n1, W), lambda b: (b, 0, 0, 0)),
        out_shape=jax.ShapeDtypeStruct((B, n2, n1, W), F32),
        scratch_shapes=[pltpu.VMEM((2 * n1, n2, W), BF16)],
        compiler_params=_cparams(("parallel",)),
        name="fnet_dft",
    )(fr.reshape(B, n1, n2, W), fi.reshape(B, n1, n2, W), m1, tab3, wbd, bias)
    return y.reshape(B, S, W)


def _post_kernel(h_ref, ya_ref, hf_ref, hb_ref, xc_ref, mo_ref, yf_ref, p_ref,
                 hn_ref, sk_ref, ones_ref, wo_ref, g2_ref, wg2_ref, wu2_ref, wd_ref,
                 gn_ref, wg_ref, wp_ref, pn_ref, fn_ref, o_ref, acc_ref, *, final, tf):
    tm = h_ref.shape[0]
    a_w = ya_ref.shape[1]
    m_w = a_w + ML_W
    groups = [slice(r, r + tm // ROW_GROUPS) for r in range(0, tm, tm // ROW_GROUPS)]

    def project(r):
        hm = hf_ref[r, :] + hb_ref[r, :]
        sq = hm * hm
        hi = sq.astype(BF16)
        lo = (sq - hi.astype(F32)).astype(BF16)
        ms = (_dot(hi, ones_ref[...]) + _dot(lo, ones_ref[...])) * (1.0 / ML_DIM)
        ym = (hm * lax.rsqrt(ms + EPS) * hn_ref[...] + sk_ref[...] * xc_ref[r, :]) * mo_ref[r, :]
        return (h_ref[r, :] + _dot(ya_ref[r, :], wo_ref[0:a_w, :])
                + _dot(ym.astype(BF16), wo_ref[a_w:m_w, :])
                + _dot(yf_ref[r, :].astype(BF16), wo_ref[m_w:, :]))

    h1 = jnp.concatenate([project(r) for r in groups], axis=0)
    h2 = h1 + _swiglu_half(h1, g2_ref, wg2_ref, wu2_ref, wd_ref, acc_ref, tf)

    for r in groups:
        h2r = h2[r, :]
        e = _rms(_dot(p_ref[r, :].astype(BF16), wp_ref[...]), pn_ref[...])
        gate = jax.nn.sigmoid(_dot(_rms(h2r, gn_ref[...]).astype(BF16), wg_ref[...]))
        h3 = h2r + gate * e
        if final:
            h3 = _rms(h3, fn_ref[...])
        o_ref[r, :] = h3


def _post(h, ya, hf, hb, xc, mo, yf, p, ones_bd, params, final_norm, layer, tm, tf, final):
    T, D = h.shape
    tok = lambda w: pl.BlockSpec((tm, w), lambda i: (i, 0))
    acts = [h, ya, hf, hb, xc, mo, yf]
    hn, sk, wo, g2, wg2, wu2, wd, gn, wg, wp, pn = params
    consts = [hn, sk, ones_bd, wo, g2, wg2, wu2, wd, gn, wg, wp, pn, final_norm]
    shared = (2, len(consts) - 1)
    specs = [_const_spec(c.shape) if j in shared else _layer_spec(c, layer) for j, c in enumerate(consts)]
    return pl.pallas_call(
        functools.partial(_post_kernel, final=final, tf=tf),
        grid=(T // tm,),
        in_specs=([tok(a.shape[1]) for a in acts]
                  + [pl.BlockSpec((None, tm, p.shape[2]), lambda i: (layer, i, 0))] + specs),
        out_specs=tok(D),
        out_shape=jax.ShapeDtypeStruct((T, D), F32),
        scratch_shapes=[pltpu.VMEM((tm, D), F32)],
        compiler_params=_cparams(("parallel",)),
        name="post",
    )(*acts, p, *consts)


def _block_diag(w):
    *lead, G, a, b = w.shape
    eye = jnp.eye(G, dtype=w.dtype)
    return (eye[:, None, :, None] * w[..., :, :, None, :]).reshape(*lead, G * a, G * b)


def _dft_tables(S):
    n1, n2 = FN_N1, S // FN_N1
    c = np.arange(FN_DIM)
    ang = 2.0 * np.pi * np.outer(c, c) / FN_DIM
    eye = np.eye(FN_GROUPS)
    chan = np.concatenate([np.kron(eye, np.cos(ang)), -np.kron(eye, np.sin(ang))], axis=1) / np.sqrt(FN_DIM)
    k1 = np.arange(n1)
    a1 = 2.0 * np.pi * np.outer(k1, k1) / n1
    m1 = np.block([[np.cos(a1), np.sin(a1)], [-np.sin(a1), np.cos(a1)]])
    k = k1[:, None, None] + n1 * np.arange(n2)[None, :, None]
    s2 = np.arange(n2)[None, None, :]
    a3 = 2.0 * np.pi * ((k * s2) % S) / S
    tab3 = np.concatenate([np.cos(a3), np.sin(a3)], axis=-1) / np.sqrt(S)
    bf = lambda t: jnp.asarray(t, dtype=F32).astype(BF16)
    return bf(chan), bf(m1), bf(tab3)


def _rope_consts():
    half = MLA_ROPE // 2
    inv = 1.0 / (ROPE_THETA ** (jnp.arange(0, MLA_ROPE, 2, dtype=F32) / MLA_ROPE))
    z = jnp.zeros((half,), F32)
    o = jnp.ones((half,), F32)
    rows = [jnp.concatenate([inv, inv, z, z]), jnp.concatenate([o, o, z, z]),
            jnp.concatenate([z, o, z, z]), jnp.concatenate([-o, z, z, z])]
    rows += [jnp.zeros((LANE,), F32)] * (SUBLANE - len(rows))
    return jnp.stack(rows)


def _tiles(B, S):
    T = B * S
    return dict(tm=min(512, T), tf=MXU_DIM, tmix=min(1024, S), tq=min(2048, S), tk=min(1024, S), attn_unroll=2,
                chunk=min(256, S), bb=2 if B % 2 == 0 else 1,
                dft_rows=min(32, S // FN_N1),
                trope=min(1024, T))


def kernel(x, p, positions, ffn1_norm, ffn1_w_gate, ffn1_w_up, ffn1_w_down, mix_norm, w_in, mla_q_norm, mla_w_uq, mla_kv_norm, mla_w_ukv, mlstm_conv_w, mlstm_conv_b, mlstm_w_q, mlstm_w_k, mlstm_i_bias, mlstm_f_bias, mlstm_head_norm, mlstm_skip, fnet_w, fnet_b, w_out, ffn2_norm, ffn2_w_gate, ffn2_w_up, ffn2_w_down, ple_gate_norm, ple_w_gate, ple_w_proj, ple_post_norm, final_norm):
    B, S, D = x.shape
    depth = p.shape[0]
    T = B * S
    t = _tiles(B, S)
    Ld = depth
    rows = lambda a: a.reshape(Ld, 1, -1).astype(F32)

    tab = _rope_consts()
    cos_t, sin_a, sin_b = [a.reshape(B, S, LANE) for a in
                           _rope_tables(positions.reshape(T, 1).astype(jnp.int32), tab, t["trope"])]
    dft_chan, dft_m1, dft_tab3 = _dft_tables(S)
    ones_bd = _block_diag(jnp.ones((ML_HEADS, ML_DIM, ML_DIM), BF16))
    qscale = float((MLA_NOPE + MLA_ROPE) ** -0.5 * LOG2E)

    wg1, wu1, wd1 = ffn1_w_gate.astype(BF16), ffn1_w_up.astype(BF16), ffn1_w_down.astype(BF16)
    wg2, wu2, wd2 = ffn2_w_gate.astype(BF16), ffn2_w_up.astype(BF16), ffn2_w_down.astype(BF16)
    o_kr = MLA_Q_LORA + MLA_KV_LORA
    o_mx = o_kr + MLA_ROPE
    o_g = o_mx + 3 * ML_W
    o_f = o_g + GATE_W
    wi = w_in.astype(BF16)
    zpad = lambda n: jnp.zeros((Ld, D, n), BF16)
    win = jnp.concatenate([wi[..., :o_kr], wi[..., o_mx:o_g], wi[..., o_f:], wi[..., o_kr:o_mx],
                           wi[..., o_g:o_f], zpad(LANE - MLA_ROPE - GATE_W)], axis=-1)
    wuq = jnp.pad(mla_w_uq.astype(BF16).reshape(Ld, MLA_Q_LORA, MLA_HEADS, MLA_NOPE + MLA_ROPE),
                  ((0, 0), (0, 0), (0, 0), (0, MLA_QK_PAD - MLA_NOPE - MLA_ROPE)))
    wuq = wuq.reshape(Ld, MLA_Q_LORA, MLA_HEADS * MLA_QK_PAD)
    wukv = mla_w_ukv.astype(BF16).reshape(Ld, MLA_KV_LORA, MLA_HEADS, 2, MLA_NOPE)
    wukv = wukv.transpose(0, 1, 3, 2, 4).reshape(Ld, MLA_KV_LORA, 2 * MLA_HEADS * MLA_NOPE)
    wqk = jnp.concatenate([_block_diag(mlstm_w_q) * (ML_DIM ** -0.5), _block_diag(mlstm_w_k)],
                          axis=-1).astype(BF16)
    gbias = jnp.concatenate([jnp.zeros((Ld, G_OFF), F32), mlstm_i_bias.reshape(Ld, -1),
                             mlstm_f_bias.reshape(Ld, -1), jnp.zeros((Ld, LANE - G_OFF - GATE_W), F32)],
                            axis=-1).reshape(Ld, 1, LANE)
    fw_bd = _block_diag(fnet_w).astype(BF16)
    post_params = [rows(mlstm_head_norm), rows(mlstm_skip), w_out.astype(BF16), rows(ffn2_norm), wg2, wu2, wd2,
                   rows(ple_gate_norm), ple_w_gate.astype(BF16), ple_w_proj.astype(BF16),
                   rows(ple_post_norm)]
    g1, gmix, gq, gkv = rows(ffn1_norm), rows(mix_norm), rows(mla_q_norm), rows(mla_kv_norm)
    conv_w, conv_b, fnet_bias = mlstm_conv_w.astype(F32), rows(mlstm_conv_b), rows(fnet_b)
    p_flat = p.reshape(Ld, T, p.shape[-1])
    fin = final_norm.reshape(1, -1).astype(F32)
    flat = lambda a: a.reshape(T, a.shape[-1])

    h = x.reshape(T, D)
    for i in range(depth):
        h = _ffn(h, g1, wg1, wu1, wd1, i, 2 * t["tm"], t["tf"])
        qt, k, vt, mv, mo, fr, fi, xc, mqt, mk, gi, gt = _mixin(
            h.reshape(B, S, D), cos_t, sin_a, sin_b, gmix, win, gq, wuq, gkv, wukv, conv_w, conv_b, wqk,
            gbias, dft_chan, i, t["tmix"], qscale, t["chunk"])
        y_mla = _attention(qt, k, vt, t["tq"], t["tk"], t["attn_unroll"])
        hf, hb = _mlstm_scan(mqt, mk, mv, gi, gt, t["chunk"], t["bb"])
        y_fnet = _fnet(fr, fi, dft_m1, dft_tab3, fw_bd, fnet_bias, i, t["dft_rows"])
        h = _post(h, flat(y_mla), flat(hf), flat(hb), flat(xc), flat(mo), flat(y_fnet), p_flat, ones_bd,
                  post_params, fin, i, t["tm"], t["tf"], final=(i == depth - 1))
    return h.reshape(B, S, D)
```

```python
import functools

import numpy as np
import jax
import jax.numpy as jnp
from jax import lax
from jax.experimental import pallas as pl
from jax.experimental.pallas import tpu as pltpu

F32 = jnp.float32
BF16 = jnp.bfloat16

EPS = 1e-6
ROPE_THETA = 10000.0
LOG2E = float(np.log2(np.e))
LANE = 128
SUBLANE = 8
MXU_DIM = 256
VMEM_LIMIT = 56 * 1024 * 1024

MLA_HEADS = 4
MLA_Q_LORA = 384
MLA_KV_LORA = 256
MLA_NOPE = 128
MLA_ROPE = 64
MLA_V = 128
MLA_QK_PAD = MXU_DIM
ML_HEADS = 4
ML_DIM = 64
ML_W = ML_HEADS * ML_DIM
ML_CONV = 5
FN_GROUPS = 4
FN_DIM = 64
FN_W = FN_GROUPS * FN_DIM
FN_N1 = 64
GATE_W = 4 * ML_HEADS
GATE_ROWS = 6 * ML_HEADS
ROW_GROUPS = 2

U_CQ = 0
U_CKV = U_CQ + MLA_Q_LORA
U_MX = U_CKV + MLA_KV_LORA
U_MV = U_MX + ML_W
U_MO = U_MV + ML_W
U_FIN = U_MO + ML_W
U_KR = U_FIN + FN_W
G_OFF = MLA_ROPE
U_TOT = U_KR + LANE


def _cparams(sem):
    return pltpu.CompilerParams(dimension_semantics=sem, vmem_limit_bytes=VMEM_LIMIT)


def _const_spec(shape):
    nd = len(shape)
    return pl.BlockSpec(shape, lambda *_: (0,) * nd, pipeline_mode=pl.Buffered(1))


def _layer_spec(a, layer):
    nd = a.ndim - 1
    return pl.BlockSpec((None,) + a.shape[1:], lambda *_: (layer,) + (0,) * nd,
                        pipeline_mode=pl.Buffered(1))


def _rms(x, g):
    ms = jnp.mean(x * x, axis=-1, keepdims=True)
    return x * lax.rsqrt(ms + EPS) * g


def _dot(a, b):
    return jnp.dot(a, b, preferred_element_type=F32)


def _swiglu_half(x, g_ref, wg_ref, wu_ref, wd_ref, acc_ref, tf):
    xn = _rms(x, g_ref[...]).astype(BF16)
    for c in range(wd_ref.shape[0] // tf):
        g = _dot(xn, wg_ref[:, c * tf:(c + 1) * tf])
        u = _dot(xn, wu_ref[:, c * tf:(c + 1) * tf])
        a = (g * jax.nn.sigmoid(g) * u).astype(BF16)
        part = _dot(a, wd_ref[c * tf:(c + 1) * tf, :])
        if c == 0:
            acc_ref[...] = part
        else:
            acc_ref[...] += part
    return 0.5 * acc_ref[...]


def _ffn_kernel(h_ref, g_ref, wg_ref, wu_ref, wd_ref, o_ref, acc_ref, *, tf):
    x = h_ref[...]
    o_ref[...] = x + _swiglu_half(x, g_ref, wg_ref, wu_ref, wd_ref, acc_ref, tf)


def _ffn(h, g, wg, wu, wd, layer, tm, tf):
    T, D = h.shape
    return pl.pallas_call(
        functools.partial(_ffn_kernel, tf=tf),
        grid=(T // tm,),
        in_specs=[pl.BlockSpec((tm, D), lambda i: (i, 0)),
                  _layer_spec(g, layer), _layer_spec(wg, layer), _layer_spec(wu, layer),
                  _layer_spec(wd, layer)],
        out_specs=pl.BlockSpec((tm, D), lambda i: (i, 0)),
        out_shape=jax.ShapeDtypeStruct((T, D), F32),
        scratch_shapes=[pltpu.VMEM((tm, D), F32)],
        compiler_params=_cparams(("parallel",)),
        name="ffn",
    )(h, g, wg, wu, wd)


def _rope_kernel(pos_ref, tab_ref, cos_ref, sina_ref, sinb_ref):
    ang = pos_ref[...].astype(F32) * tab_ref[0:1, :]
    c = jnp.cos(ang)
    s = jnp.sin(ang)
    cos_ref[...] = c * tab_ref[1:2, :]
    sina_ref[...] = s * tab_ref[2:3, :]
    sinb_ref[...] = s * tab_ref[3:4, :]


def _rope_tables(pos, tab, tm):
    T = pos.shape[0]
    out = jax.ShapeDtypeStruct((T, LANE), F32)
    spec = pl.BlockSpec((tm, LANE), lambda i: (i, 0))
    return pl.pallas_call(
        _rope_kernel,
        grid=(T // tm,),
        in_specs=[pl.BlockSpec((tm, 1), lambda i: (i, 0)), _const_spec(tab.shape)],
        out_specs=[spec, spec, spec],
        out_shape=[out, out, out],
        compiler_params=_cparams(("parallel",)),
        name="rope_tables",
    )(pos, tab)


def _rope(blk, cos_t, sin_a, sin_b):
    half = MLA_ROPE // 2
    return (blk * cos_t + pltpu.roll(blk, half, 1) * sin_a
            + pltpu.roll(blk, LANE - half, 1) * sin_b)


def _mixin_kernel(h_ref, hprev_ref, hnext_ref, cos_ref, sina_ref, sinb_ref, g_ref, win_ref, qn_ref, wuq_ref,
                  kvn_ref, wukv_ref, cw_ref, cb_ref, wqk_ref, gb_ref, dft_ref,
                  qt_ref, k_ref, vt_ref, mv_ref, mo_ref, fr_ref, fi_ref, xc_ref, mqt_ref, mk_ref, gi_ref,
                  gt_ref, ext_ref, gate_ref, *, qscale, chunk):
    tm = h_ref.shape[1]
    halo = SUBLANE
    for r0 in range(0, tm, tm // ROW_GROUPS):
        r = slice(r0, r0 + tm // ROW_GROUPS)
        x = h_ref[0, r, :]
        xn = _rms(x, g_ref[...]).astype(BF16)
        u = _dot(xn, win_ref[...])
        cos_t, sin_a, sin_b = cos_ref[0, r, :], sina_ref[0, r, :], sinb_ref[0, r, :]

        cq = _rms(u[:, U_CQ:U_CKV], qn_ref[...]).astype(BF16)
        q = _dot(cq, wuq_ref[...])
        for hd in range(MLA_HEADS):
            o = hd * MLA_QK_PAD
            qt_ref[0, hd, 0:MLA_NOPE, r] = (q[:, o:o + MLA_NOPE] * qscale).T.astype(BF16)
            qr = _rope(q[:, o + MLA_NOPE:o + MLA_QK_PAD], cos_t, sin_a, sin_b)
            qt_ref[0, hd, MLA_NOPE:MLA_QK_PAD, r] = (qr * qscale).T.astype(BF16)

        ckv = _rms(u[:, U_CKV:U_MX], kvn_ref[...]).astype(BF16)
        kv = _dot(ckv, wukv_ref[...])
        kr = _rope(u[:, U_KR:U_KR + LANE], cos_t, sin_a, sin_b).astype(BF16)
        for hd in range(MLA_HEADS):
            o = hd * MLA_QK_PAD
            k_ref[0, r, o:o + MLA_NOPE] = kv[:, hd * MLA_NOPE:(hd + 1) * MLA_NOPE].astype(BF16)
            k_ref[0, r, o + MLA_NOPE:o + MLA_QK_PAD] = kr
            vo = (MLA_HEADS + hd) * MLA_NOPE
            vt_ref[0, hd, :, r] = kv[:, vo:vo + MLA_V].T.astype(BF16)

        ext_ref[halo + r0:halo + r0 + tm // ROW_GROUPS, :] = u[:, U_MX:U_MV]
        mv_ref[0, :, r] = u[:, U_MV:U_MO].T.astype(BF16)
        mo_ref[0, r, :] = jax.nn.sigmoid(u[:, U_MO:U_FIN])
        gate_ref[r, :] = u[:, U_KR:U_TOT]

        f = _dot(u[:, U_FIN:U_KR].astype(BF16), dft_ref[...])
        fr_ref[0, r, :] = f[:, :FN_W].astype(BF16)
        fi_ref[0, r, :] = f[:, FN_W:].astype(BF16)

    i = pl.program_id(1)
    xh = jnp.concatenate([hprev_ref[0], hnext_ref[0]], axis=0)
    mh = _dot(_rms(xh, g_ref[...]).astype(BF16), win_ref[:, U_MX:U_MV])
    ext_ref[0:halo, :] = jnp.where(i > 0, mh[0:halo], 0.0)
    ext_ref[halo + tm:, :] = jnp.where(i < pl.num_programs(1) - 1, mh[halo:], 0.0)
    _mlstm_pre(ext_ref, gate_ref[...], cw_ref, cb_ref, wqk_ref, gb_ref,
               xc_ref, mqt_ref, mk_ref, gi_ref, gt_ref, chunk)


def _mixin(h3, cos_t, sin_a, sin_b, g, win, qn, wuq, kvn, wukv, cw, cb, wqk, gb, dft, layer, tm, qscale, chunk):
    B, S, D = h3.shape
    nblk = tm // SUBLANE
    last = S // SUBLANE - 1
    tok = lambda w: pl.BlockSpec((1, tm, w), lambda b, i: (b, i, 0))
    tcol = lambda w: pl.BlockSpec((1, w, tm), lambda b, i: (b, 0, i))
    outs = [
        (jax.ShapeDtypeStruct((B, MLA_HEADS, MLA_QK_PAD, S), BF16),
         pl.BlockSpec((1, MLA_HEADS, MLA_QK_PAD, tm), lambda b, i: (b, 0, 0, i))),
        (jax.ShapeDtypeStruct((B, S, MLA_HEADS * MLA_QK_PAD), BF16), tok(MLA_HEADS * MLA_QK_PAD)),
        (jax.ShapeDtypeStruct((B, MLA_HEADS, MLA_V, S), BF16),
         pl.BlockSpec((1, MLA_HEADS, MLA_V, tm), lambda b, i: (b, 0, 0, i))),
        (jax.ShapeDtypeStruct((B, ML_W, S), BF16), tcol(ML_W)),
        (jax.ShapeDtypeStruct((B, S, ML_W), F32), tok(ML_W)),
        (jax.ShapeDtypeStruct((B, S, FN_W), BF16), tok(FN_W)),
        (jax.ShapeDtypeStruct((B, S, FN_W), BF16), tok(FN_W)),
        (jax.ShapeDtypeStruct((B, S, ML_W), F32), tok(ML_W)),
        (jax.ShapeDtypeStruct((B, ML_W, S), BF16), tcol(ML_W)),
        (jax.ShapeDtypeStruct((B, S, ML_W), BF16), tok(ML_W)),
        (jax.ShapeDtypeStruct((B, S, LANE), F32), tok(LANE)),
        (jax.ShapeDtypeStruct((B, GATE_ROWS, S), F32), tcol(GATE_ROWS)),
    ]
    consts = [g, win, qn, wuq, kvn, wukv, cw, cb, wqk, gb]
    return pl.pallas_call(
        functools.partial(_mixin_kernel, qscale=qscale, chunk=chunk),
        grid=(B, S // tm),
        in_specs=([tok(D),
                   pl.BlockSpec((1, SUBLANE, D), lambda b, i: (b, jnp.maximum(i * nblk - 1, 0), 0)),
                   pl.BlockSpec((1, SUBLANE, D), lambda b, i: (b, jnp.minimum((i + 1) * nblk, last), 0)),
                   tok(LANE), tok(LANE), tok(LANE)] + [_layer_spec(c, layer) for c in consts]
                  + [_const_spec(dft.shape)]),
        out_specs=[o[1] for o in outs],
        out_shape=[o[0] for o in outs],
        scratch_shapes=[pltpu.VMEM((tm + 2 * SUBLANE, ML_W), F32), pltpu.VMEM((tm, LANE), F32)],
        compiler_params=_cparams(("parallel", "parallel")),
        name="mix_in",
    )(h3, h3, h3, cos_t, sin_a, sin_b, *consts, dft)


def _attn_kernel(qt_ref, k_ref, vt_ref, o_ref, acc_ref, s_ref, *, tk, unroll):
    qt = qt_ref[0, 0]
    S = k_ref.shape[1]
    n = S // tk
    tq = qt.shape[1]
    acc_ref[...] = jnp.zeros_like(acc_ref)

    def produce(j, slot):
        ks = pl.multiple_of(j * tk, tk)
        st = _dot(k_ref[0, pl.ds(ks, tk), :], qt)
        s_ref[slot] = st
        return jnp.max(st, axis=0, keepdims=True)

    def consume(j, slot, cmax, m_old, l_old):
        ks = pl.multiple_of(j * tk, tk)
        m_new = jnp.maximum(m_old, cmax)
        p = jnp.exp2(s_ref[slot] - m_new)
        alpha = jnp.exp2(m_old - m_new)
        l_new = alpha * l_old + jnp.sum(p, axis=0, keepdims=True)
        acc_ref[...] = alpha * acc_ref[...] + _dot(vt_ref[0, 0, :, pl.ds(ks, tk)], p.astype(BF16))
        return m_new, l_new

    def body(i, carry):
        cmax, m, l = carry
        for u in range(unroll):
            j = unroll * i + u
            nxt = produce(j + 1, (u + 1) % 2)
            m, l = consume(j, u % 2, cmax, m, l)
            cmax = nxt
        return cmax, m, l

    trips = (n - 1) // unroll
    carry = (produce(0, 0), jnp.full((1, tq), -jnp.inf, F32), jnp.zeros((1, tq), F32))
    cmax, m, l = lax.fori_loop(0, trips, body, carry)
    for j in range(unroll * trips, n):
        nxt = produce(j + 1, (j + 1) % 2) if j + 1 < n else None
        m, l = consume(j, j % 2, cmax, m, l)
        cmax = nxt
    o_ref[0] = (acc_ref[...] / l).T.astype(o_ref.dtype)


def _attention(qt, k, vt, tq, tk, unroll):
    B, S, _ = k.shape
    assert S % tq == 0 and S % tk == 0 and unroll % 2 == 0, "logit buffers alternate by chunk parity"
    return pl.pallas_call(
        functools.partial(_attn_kernel, tk=tk, unroll=unroll),
        grid=(B, MLA_HEADS, S // tq),
        in_specs=[pl.BlockSpec((1, 1, MLA_QK_PAD, tq), lambda b, h, i: (b, h, 0, i)),
                  pl.BlockSpec((1, S, MLA_QK_PAD), lambda b, h, i: (b, 0, h)),
                  pl.BlockSpec((1, 1, MLA_V, S), lambda b, h, i: (b, h, 0, 0))],
        out_specs=pl.BlockSpec((1, tq, MLA_V), lambda b, h, i: (b, i, h)),
        out_shape=jax.ShapeDtypeStruct((B, S, MLA_HEADS * MLA_V), BF16),
        scratch_shapes=[pltpu.VMEM((MLA_V, tq), F32), pltpu.VMEM((2, tk, tq), F32)],
        compiler_params=_cparams(("parallel", "parallel", "arbitrary")),
        name="mla_attention",
    )(qt, k, vt)


def _mlstm_pre(ext_ref, gate, cw_ref, cb_ref, wqk_ref, gb_ref, xc_ref, qt_ref, k_ref, gi_ref, gt_ref, chunk):
    halo = SUBLANE
    ts = ext_ref.shape[0] - 2 * halo
    pad = ML_CONV // 2
    xc = cb_ref[...] + cw_ref[0:1, :] * ext_ref[halo - pad:halo - pad + ts, :]
    for j in range(1, ML_CONV):
        o = halo - pad + j
        xc = xc + cw_ref[j:j + 1, :] * ext_ref[o:o + ts, :]
    xc = xc * jax.nn.sigmoid(xc)
    xc_ref[0] = xc
    qk = _dot(xc.astype(BF16), wqk_ref[...])
    qt_ref[0] = qk[:, :ML_W].T.astype(BF16)
    k_ref[0] = qk[:, ML_W:].astype(BF16)
    g_t = (gate + gb_ref[...]).T[G_OFF:G_OFF + GATE_W, :]
    lf_t = jnp.minimum(g_t, 0.0) - jnp.log(1.0 + jnp.exp(-jnp.abs(g_t)))
    p1 = lf_t.astype(BF16)
    r1 = lf_t - p1.astype(F32)
    p2 = r1.astype(BF16)
    p3 = (r1 - p2.astype(F32)).astype(BF16)
    r = lax.broadcasted_iota(jnp.int32, (chunk, chunk), 0)
    c = lax.broadcasted_iota(jnp.int32, (chunk, chunk), 1)
    tri_f = (r <= c).astype(BF16)
    tri_b = (r >= c).astype(BF16)
    sub = lax.broadcasted_iota(jnp.int32, (GATE_W, chunk), 0)
    cols = []
    for k in range(ts // chunk):
        sl = slice(k * chunk, (k + 1) * chunk)
        cum_f = _dot(p1[:, sl], tri_f) + _dot(p2[:, sl], tri_f) + _dot(p3[:, sl], tri_f)
        cum_b = _dot(p1[:, sl], tri_b) + _dot(p2[:, sl], tri_b) + _dot(p3[:, sl], tri_b)
        cols.append(jnp.where(sub < 2 * ML_HEADS, g_t[:, sl], jnp.where(sub < 3 * ML_HEADS, cum_f, cum_b)))
    gt = jnp.concatenate(cols, axis=1) * LOG2E
    gt = jnp.concatenate([gt, gt[0:2 * ML_HEADS] - gt[2 * ML_HEADS:]], axis=0)
    gt_ref[0] = gt
    gi_ref[0] = jnp.concatenate([gt, jnp.zeros((LANE - GATE_ROWS, ts), F32)], axis=0).T


def _mlstm_chunk(blocks, c_ref, n_ref, m_ref):
    nslot = len(blocks)
    bb = nslot // 2
    L = blocks[0][1].shape[0]
    H, dh = ML_HEADS, ML_DIM
    lane_w = lax.broadcasted_iota(jnp.int32, (1, ML_W), 1)
    row_w = lax.broadcasted_iota(jnp.int32, (ML_W, 1), 0)
    hlanes = [(lane_w // dh) == hd for hd in range(H)]
    hrows = [(row_w // dh) == hd for hd in range(H)]
    zero_k = jnp.zeros_like(blocks[0][1])
    zero_v = jnp.zeros_like(blocks[0][2])
    sel8 = (lax.broadcasted_iota(jnp.int32, (SUBLANE, ML_W), 0)
            == lax.broadcasted_iota(jnp.int32, (SUBLANE, ML_W), 1) // dh)
    probs_of = [(si, hd) for si in range(nslot) for hd in range(H)]
    direction = lambda si: si // bb
    gate = lambda kind, si, hd: kind * 2 * H + direction(si) * H + hd
    edge = lambda si: L - 1 if direction(si) == 0 else 0

    QT, K, VT, GI, GT = range(5)
    inter_t, qn_t = [], []
    for si, blk in enumerate(blocks):
        inter_t.append(_dot(c_ref[si].astype(BF16), blk[QT]))
        qn_t.append(_dot(n_ref[si].astype(BF16), blk[QT]))

    m_old = jnp.stack([m_ref[si, hd:hd + 1, 0:1] for si, hd in probs_of])
    imb_r = jnp.stack([blocks[si][GT][gate(2, si, hd):gate(2, si, hd) + 1, :]
                       for si, hd in probs_of])
    b_edge = jnp.stack([blocks[si][GT][gate(1, si, hd):gate(1, si, hd) + 1, edge(si):edge(si) + 1]
                        for si, hd in probs_of])
    k_head = {(si, hd): jnp.where(hlanes[hd], blocks[si][K], zero_k) for si, hd in probs_of}

    half = bb * H
    hl = L // 2

    def intra(sl, causal):
        kk = lax.broadcasted_iota(jnp.int32, (hl, hl), 0)
        qq = lax.broadcasted_iota(jnp.int32, (hl, hl), 1)
        keep = (kk <= qq) if causal else (kk >= qq)
        mine = probs_of[sl]
        col = lambda lo: jnp.stack([blocks[si][GI][lo:lo + hl, gate(2, si, hd):gate(2, si, hd) + 1]
                                    for si, hd in mine])
        row = lambda lo: jnp.stack([blocks[si][GT][gate(1, si, hd):gate(1, si, hd) + 1, lo:lo + hl]
                                    for si, hd in mine])
        c0, c1, r0, r1 = col(0), col(hl), row(0), row(hl)
        m_prev = jnp.stack([m_ref[si, hd:hd + 1, 0:1] for si, hd in mine])
        qk_all = jnp.stack([_dot(k_head[si, hd], blocks[si][QT]) for si, hd in mine])
        qk = lambda ks, qs: qk_all[:, ks:ks + hl, qs:qs + hl]
        d00 = jnp.where(keep, c0 + r0, -jnp.inf)
        d11 = jnp.where(keep, c1 + r1, -jnp.inf)
        dof = (c0 + r1) if causal else (c1 + r0)
        col_max = lambda d: jnp.max(d, axis=1, keepdims=True)
        mi0, mi1 = r0 + m_prev, r1 + m_prev
        if causal:
            mt0, mt1 = jnp.maximum(mi0, col_max(d00)), jnp.maximum(mi1, jnp.maximum(col_max(d11), col_max(dof)))
            sof = qk(0, hl) * jnp.exp2(dof - mt1)
        else:
            mt0, mt1 = jnp.maximum(mi0, jnp.maximum(col_max(d00), col_max(dof))), jnp.maximum(mi1, col_max(d11))
            sof = qk(hl, 0) * jnp.exp2(dof - mt0)
        s00 = qk(0, 0) * jnp.exp2(d00 - mt0)
        s11 = qk(hl, hl) * jnp.exp2(d11 - mt1)
        col_sum = lambda s: jnp.sum(s, axis=1, keepdims=True)
        if causal:
            den0, den1 = col_sum(s00), col_sum(s11) + col_sum(sof)
        else:
            den0, den1 = col_sum(s00) + col_sum(sof), col_sum(s11)
        qn0 = jnp.stack([qn_t[si][hd:hd + 1, 0:hl] for si, hd in mine])
        qn1 = jnp.stack([qn_t[si][hd:hd + 1, hl:L] for si, hd in mine])
        ws0, ws1 = jnp.exp2(mi0 - mt0), jnp.exp2(mi1 - mt1)
        sc0 = 1.0 / jnp.maximum(jnp.abs(den0 + ws0 * qn0), jnp.exp2(-mt0))
        sc1 = 1.0 / jnp.maximum(jnp.abs(den1 + ws1 * qn1), jnp.exp2(-mt1))
        wsc0, wsc1 = ws0 * sc0, ws1 * sc1
        p00, p11, pof = s00.astype(BF16), s11.astype(BF16), sof.astype(BF16)
        zero = jnp.zeros((hl, hl), BF16)
        probs_l, scale_l, wscale_l = [], [], []
        for i in range(len(mine)):
            if causal:
                top = jnp.concatenate([p00[i], pof[i]], axis=1)
                bot = jnp.concatenate([zero, p11[i]], axis=1)
            else:
                top = jnp.concatenate([p00[i], zero], axis=1)
                bot = jnp.concatenate([pof[i], p11[i]], axis=1)
            probs_l.append(jnp.concatenate([top, bot], axis=0))
            scale_l.append(jnp.concatenate([sc0[i], sc1[i]], axis=1))
            wscale_l.append(jnp.concatenate([wsc0[i], wsc1[i]], axis=1))
        return probs_l, scale_l, wscale_l

    probs_f, scale_f, wscale_f = intra(slice(0, half), True)
    probs_b, scale_b, wscale_b = intra(slice(half, 2 * half), False)
    probs, scale, wscale = probs_f + probs_b, scale_f + scale_b, wscale_f + wscale_b
    lw = b_edge + imb_r
    m_new = jnp.maximum(b_edge + m_old, jnp.max(lw, axis=-1, keepdims=True))
    ws = jnp.exp2(lw - m_new)
    dec = jnp.exp2(b_edge + m_old - m_new)

    def per_head_rows(rows, si):
        return jnp.concatenate([jnp.broadcast_to(rows[si * H + hd], (dh, L)) for hd in range(H)], axis=0)

    def per_head_lanes(vals, si):
        e = vals[si * H + H - 1]
        for hd in reversed(range(H - 1)):
            e = jnp.where(hlanes[hd], vals[si * H + hd], e)
        return e

    blockdiag = (row_w // dh) == (lane_w // dh)
    outs = []
    for si, blk in enumerate(blocks):
        k, vt = blk[K], blk[VT]
        v_heads = jnp.concatenate([jnp.where(hrows[hd], vt, zero_v) for hd in range(H)], axis=1)
        intra_t = _dot(v_heads, jnp.concatenate([probs[si * H + hd] for hd in range(H)], axis=0))
        out_t = intra_t * per_head_rows(scale, si) + inter_t[si] * per_head_rows(wscale, si)
        outs.append(out_t.T)

        dec_l = per_head_lanes(dec, si)
        vw = (vt.astype(F32) * per_head_rows(ws, si)).astype(BF16)
        c_new = c_ref[si] * dec_l + _dot(vw, k)
        c_ref[si] = jnp.where(blockdiag, c_new, 0.0)
        ws8 = jnp.concatenate([ws[si * H + hd] for hd in range(H)]
                              + [jnp.zeros((SUBLANE - H, L), F32)], axis=0).astype(BF16)
        n_ref[si] = n_ref[si] * dec_l + jnp.where(sel8, _dot(ws8, k), 0.0)
        m_ref[si, 0:H, :] = jnp.concatenate(
            [jnp.broadcast_to(m_new[si * H + hd], (1, LANE)) for hd in range(H)], axis=0)
    return outs


def _mlscan_kernel(qf_ref, kf_ref, vf_ref, gif_ref, gtf_ref, qb_ref, kb_ref, vb_ref, gib_ref, gtb_ref,
                   hf_ref, hb_ref, c_ref, n_ref, m_ref, *, bb):
    @pl.when(pl.program_id(1) == 0)
    def _():
        c_ref[...] = jnp.zeros_like(c_ref)
        n_ref[...] = jnp.zeros_like(n_ref)
        m_ref[...] = jnp.zeros_like(m_ref)

    fwd = [(qf_ref[b], kf_ref[b], vf_ref[b], gif_ref[b], gtf_ref[b]) for b in range(bb)]
    bwd = [(qb_ref[b], kb_ref[b], vb_ref[b], gib_ref[b], gtb_ref[b]) for b in range(bb)]
    outs = _mlstm_chunk(fwd + bwd, c_ref, n_ref, m_ref)
    for b in range(bb):
        hf_ref[b] = outs[b]
        hb_ref[b] = outs[bb + b]


def _mlstm_scan(qt, k, vt, gi, gt, L, bb):
    B, S, W = k.shape
    nc = S // L
    fw = lambda b, c: c
    bw = lambda b, c: nc - 1 - c
    def specs(cm):
        return [pl.BlockSpec((bb, W, L), lambda b, c: (b, 0, cm(b, c))),
                pl.BlockSpec((bb, L, W), lambda b, c: (b, cm(b, c), 0)),
                pl.BlockSpec((bb, W, L), lambda b, c: (b, 0, cm(b, c))),
                pl.BlockSpec((bb, L, LANE), lambda b, c: (b, cm(b, c), 0)),
                pl.BlockSpec((bb, GATE_ROWS, L), lambda b, c: (b, 0, cm(b, c)))]
    out = jax.ShapeDtypeStruct((B, S, W), F32)
    return pl.pallas_call(
        functools.partial(_mlscan_kernel, bb=bb),
        grid=(B // bb, nc),
        in_specs=specs(fw) + specs(bw),
        out_specs=[pl.BlockSpec((bb, L, W), lambda b, c: (b, c, 0)),
                   pl.BlockSpec((bb, L, W), lambda b, c: (b, nc - 1 - c, 0))],
        out_shape=[out, out],
        scratch_shapes=[pltpu.VMEM((2 * bb, W, W), F32), pltpu.VMEM((2 * bb, SUBLANE, W), F32),
                        pltpu.VMEM((2 * bb, SUBLANE, LANE), F32)],
        compiler_params=_cparams(("parallel", "arbitrary")),
        name="mlstm_scan",
    )(qt, k, vt, gi, gt, qt, k, vt, gi, gt)


def _fnet_kernel(wr_ref, wi_ref, m1_ref, tab_ref, w_ref, b_ref, o_ref, a_ref, *, rows):
    n1, n2, W = wr_ref.shape[1:]
    kb = tab_ref.shape[0]
    g = pl.program_id(1)

    @pl.when(g == 0)
    def _():
        for r0 in range(0, n2, rows):
            xr = wr_ref[0, :, r0:r0 + rows, :].reshape(n1, rows * W)
            xi = wi_ref[0, :, r0:r0 + rows, :].reshape(n1, rows * W)
            a = _dot(m1_ref[:, 0:n1], xr) + _dot(m1_ref[:, n1:], xi)
            a_ref[:, r0:r0 + rows, :] = a.astype(BF16).reshape(2 * n1, rows, W)

    for j in range(kb):
        k1 = g * kb + j
        z = jnp.concatenate([a_ref[k1], a_ref[n1 + k1]], axis=0)
        y = _dot(tab_ref[j], z)
        o_ref[0, :, j, :] = _dot(y.astype(BF16), w_ref[...]) + b_ref[...]


def _fnet(fr, fi, m1, tab3, wbd, bias, layer, rows):
    B, S, W = fr.shape
    n1 = FN_N1
    n2 = S // n1
    kb = 2 * SUBLANE
    blk = pl.BlockSpec((1, n1, n2, W), lambda b, g: (b, 0, 0, 0))
    y = pl.pallas_call(
        functools.partial(_fnet_kernel, rows=rows),
        grid=(B, n1 // kb),
        in_specs=[blk, blk, _const_spec(m1.shape),
                  pl.BlockSpec((kb, n2, 2 * n2), lambda b, g: (g, 0, 0)),
                  _layer_spec(wbd, layer), _layer_spec(bias, layer)],
        out_specs=pl.BlockSpec((1, n2, kb, W), lambda b, g: (b, 0, g, 0)),
        out_shape=jax.ShapeDtypeStruct((B, n2, n1, W), F32),
        scratch_shapes=[pltpu.VMEM((2 * n1, n2, W), BF16)],
        compiler_params=_cparams(("parallel", "arbitrary")),
        name="fnet_dft",
    )(fr.reshape(B, n1, n2, W), fi.reshape(B, n1, n2, W), m1, tab3, wbd, bias)
    return y.reshape(B, S, W)


def _post_kernel(h_ref, ya_ref, hf_ref, hb_ref, xc_ref, mo_ref, yf_ref, p_ref,
                 hn_ref, sk_ref, ones_ref, wo_ref, g2_ref, wg2_ref, wu2_ref, wd_ref,
                 gn_ref, wg_ref, wp_ref, pn_ref, fn_ref, o_ref, acc_ref, *, final, tf):
    tm = h_ref.shape[0]
    a_w = ya_ref.shape[1]
    m_w = a_w + ML_W
    groups = [slice(r, r + tm // ROW_GROUPS) for r in range(0, tm, tm // ROW_GROUPS)]

    def project(r):
        hm = hf_ref[r, :] + hb_ref[r, :]
        sq = hm * hm
        hi = sq.astype(BF16)
        lo = (sq - hi.astype(F32)).astype(BF16)
        ms = (_dot(hi, ones_ref[...]) + _dot(lo, ones_ref[...])) * (1.0 / ML_DIM)
        ym = (hm * lax.rsqrt(ms + EPS) * hn_ref[...] + sk_ref[...] * xc_ref[r, :]) * mo_ref[r, :]
        return (h_ref[r, :] + _dot(ya_ref[r, :], wo_ref[0:a_w, :])
                + _dot(ym.astype(BF16), wo_ref[a_w:m_w, :])
                + _dot(yf_ref[r, :].astype(BF16), wo_ref[m_w:, :]))

    h1 = jnp.concatenate([project(r) for r in groups], axis=0)
    h2 = h1 + _swiglu_half(h1, g2_ref, wg2_ref, wu2_ref, wd_ref, acc_ref, tf)

    for r in groups:
        h2r = h2[r, :]
        e = _rms(_dot(p_ref[r, :].astype(BF16), wp_ref[...]), pn_ref[...])
        gate = jax.nn.sigmoid(_dot(_rms(h2r, gn_ref[...]).astype(BF16), wg_ref[...]))
        h3 = h2r + gate * e
        if final:
            h3 = _rms(h3, fn_ref[...])
        o_ref[r, :] = h3


def _post(h, ya, hf, hb, xc, mo, yf, p, ones_bd, params, final_norm, layer, tm, tf, final):
    T, D = h.shape
    tok = lambda w: pl.BlockSpec((tm, w), lambda i: (i, 0))
    acts = [h, ya, hf, hb, xc, mo, yf]
    hn, sk, wo, g2, wg2, wu2, wd, gn, wg, wp, pn = params
    consts = [hn, sk, ones_bd, wo, g2, wg2, wu2, wd, gn, wg, wp, pn, final_norm]
    shared = (2, len(consts) - 1)
    specs = [_const_spec(c.shape) if j in shared else _layer_spec(c, layer) for j, c in enumerate(consts)]
    return pl.pallas_call(
        functools.partial(_post_kernel, final=final, tf=tf),
        grid=(T // tm,),
        in_specs=([tok(a.shape[1]) for a in acts]
                  + [pl.BlockSpec((None, tm, p.shape[2]), lambda i: (layer, i, 0))] + specs),
        out_specs=tok(D),
        out_shape=jax.ShapeDtypeStruct((T, D), F32),
        scratch_shapes=[pltpu.VMEM((tm, D), F32)],
        compiler_params=_cparams(("parallel",)),
        name="post",
    )(*acts, p, *consts)


def _block_diag(w):
    *lead, G, a, b = w.shape
    eye = jnp.eye(G, dtype=w.dtype)
    return (eye[:, None, :, None] * w[..., :, :, None, :]).reshape(*lead, G * a, G * b)


def _dft_tables(S):
    n1, n2 = FN_N1, S // FN_N1
    c = np.arange(FN_DIM)
    ang = 2.0 * np.pi * np.outer(c, c) / FN_DIM
    eye = np.eye(FN_GROUPS)
    chan = np.concatenate([np.kron(eye, np.cos(ang)), -np.kron(eye, np.sin(ang))], axis=1) / np.sqrt(FN_DIM)
    k1 = np.arange(n1)
    a1 = 2.0 * np.pi * np.outer(k1, k1) / n1
    m1 = np.block([[np.cos(a1), np.sin(a1)], [-np.sin(a1), np.cos(a1)]])
    k = k1[:, None, None] + n1 * np.arange(n2)[None, :, None]
    s2 = np.arange(n2)[None, None, :]
    a3 = 2.0 * np.pi * ((k * s2) % S) / S
    tab3 = np.concatenate([np.cos(a3), np.sin(a3)], axis=-1) / np.sqrt(S)
    bf = lambda t: jnp.asarray(t, dtype=F32).astype(BF16)
    return bf(chan), bf(m1), bf(tab3)


def _rope_consts():
    half = MLA_ROPE // 2
    inv = 1.0 / (ROPE_THETA ** (jnp.arange(0, MLA_ROPE, 2, dtype=F32) / MLA_ROPE))
    z = jnp.zeros((half,), F32)
    o = jnp.ones((half,), F32)
    rows = [jnp.concatenate([inv, inv, z, z]), jnp.concatenate([o, o, z, z]),
            jnp.concatenate([z, o, z, z]), jnp.concatenate([-o, z, z, z])]
    rows += [jnp.zeros((LANE,), F32)] * (SUBLANE - len(rows))
    return jnp.stack(rows)


def _tiles(B, S):
    T = B * S
    return dict(tm=min(512, T), tf=MXU_DIM, tmix=min(1024, S), tq=min(2048, S), tk=min(1024, S), attn_unroll=2,
                chunk=min(256, S), bb=2 if B % 2 == 0 else 1,
                dft_rows=min(32, S // FN_N1),
                trope=min(1024, T))


def kernel(x, p, positions, ffn1_norm, ffn1_w_gate, ffn1_w_up, ffn1_w_down, mix_norm, w_in, mla_q_norm, mla_w_uq, mla_kv_norm, mla_w_ukv, mlstm_conv_w, mlstm_conv_b, mlstm_w_q, mlstm_w_k, mlstm_i_bias, mlstm_f_bias, mlstm_head_norm, mlstm_skip, fnet_w, fnet_b, w_out, ffn2_norm, ffn2_w_gate, ffn2_w_up, ffn2_w_down, ple_gate_norm, ple_w_gate, ple_w_proj, ple_post_norm, final_norm):
    B, S, D = x.shape
    depth = p.shape[0]
    T = B * S
    t = _tiles(B, S)
    Ld = depth
    rows = lambda a: a.reshape(Ld, 1, -1).astype(F32)

    tab = _rope_consts()
    cos_t, sin_a, sin_b = [a.reshape(B, S, LANE) for a in
                           _rope_tables(positions.reshape(T, 1).astype(jnp.int32), tab, t["trope"])]
    dft_chan, dft_m1, dft_tab3 = _dft_tables(S)
    ones_bd = _block_diag(jnp.ones((ML_HEADS, ML_DIM, ML_DIM), BF16))
    qscale = float((MLA_NOPE + MLA_ROPE) ** -0.5 * LOG2E)

    wg1, wu1, wd1 = ffn1_w_gate.astype(BF16), ffn1_w_up.astype(BF16), ffn1_w_down.astype(BF16)
    wg2, wu2, wd2 = ffn2_w_gate.astype(BF16), ffn2_w_up.astype(BF16), ffn2_w_down.astype(BF16)
    o_kr = MLA_Q_LORA + MLA_KV_LORA
    o_mx = o_kr + MLA_ROPE
    o_g = o_mx + 3 * ML_W
    o_f = o_g + GATE_W
    wi = w_in.astype(BF16)
    zpad = lambda n: jnp.zeros((Ld, D, n), BF16)
    win = jnp.concatenate([wi[..., :o_kr], wi[..., o_mx:o_g], wi[..., o_f:], wi[..., o_kr:o_mx],
                           wi[..., o_g:o_f], zpad(LANE - MLA_ROPE - GATE_W)], axis=-1)
    wuq = jnp.pad(mla_w_uq.astype(BF16).reshape(Ld, MLA_Q_LORA, MLA_HEADS, MLA_NOPE + MLA_ROPE),
                  ((0, 0), (0, 0), (0, 0), (0, MLA_QK_PAD - MLA_NOPE - MLA_ROPE)))
    wuq = wuq.reshape(Ld, MLA_Q_LORA, MLA_HEADS * MLA_QK_PAD)
    wukv = mla_w_ukv.astype(BF16).reshape(Ld, MLA_KV_LORA, MLA_HEADS, 2, MLA_NOPE)
    wukv = wukv.transpose(0, 1, 3, 2, 4).reshape(Ld, MLA_KV_LORA, 2 * MLA_HEADS * MLA_NOPE)
    wqk = jnp.concatenate([_block_diag(mlstm_w_q) * (ML_DIM ** -0.5), _block_diag(mlstm_w_k)],
                          axis=-1).astype(BF16)
    gbias = jnp.concatenate([jnp.zeros((Ld, G_OFF), F32), mlstm_i_bias.reshape(Ld, -1),
                             mlstm_f_bias.reshape(Ld, -1), jnp.zeros((Ld, LANE - G_OFF - GATE_W), F32)],
                            axis=-1).reshape(Ld, 1, LANE)
    fw_bd = _block_diag(fnet_w).astype(BF16)
    post_params = [rows(mlstm_head_norm), rows(mlstm_skip), w_out.astype(BF16), rows(ffn2_norm), wg2, wu2, wd2,
                   rows(ple_gate_norm), ple_w_gate.astype(BF16), ple_w_proj.astype(BF16),
                   rows(ple_post_norm)]
    g1, gmix, gq, gkv = rows(ffn1_norm), rows(mix_norm), rows(mla_q_norm), rows(mla_kv_norm)
    conv_w, conv_b, fnet_bias = mlstm_conv_w.astype(F32), rows(mlstm_conv_b), rows(fnet_b)
    p_flat = p.reshape(Ld, T, p.shape[-1])
    fin = final_norm.reshape(1, -1).astype(F32)
    flat = lambda a: a.reshape(T, a.shape[-1])

    h = x.reshape(T, D)
    for i in range(depth):
        h = _ffn(h, g1, wg1, wu1, wd1, i, 2 * t["tm"], t["tf"])
        qt, k, vt, mv, mo, fr, fi, xc, mqt, mk, gi, gt = _mixin(
            h.reshape(B, S, D), cos_t, sin_a, sin_b, gmix, win, gq, wuq, gkv, wukv, conv_w, conv_b, wqk,
            gbias, dft_chan, i, t["tmix"], qscale, t["chunk"])
        y_mla = _attention(qt, k, vt, t["tq"], t["tk"], t["attn_unroll"])
        hf, hb = _mlstm_scan(mqt, mk, mv, gi, gt, t["chunk"], t["bb"])
        y_fnet = _fnet(fr, fi, dft_m1, dft_tab3, fw_bd, fnet_bias, i, t["dft_rows"])
        h = _post(h, flat(y_mla), flat(hf), flat(hb), flat(xc), flat(mo), flat(y_fnet), p_flat, ones_bd,
                  post_params, fin, i, t["tm"], t["tf"], final=(i == depth - 1))
    return h.reshape(B, S, D)
```
